```python
import jax, jax.numpy as jnp
from jax import lax
import numpy as np

D_MODEL = 2048
BATCH = 4
SEQ = 2048
DEPTH = 4
DEC_BATCH = 128
DEC_SEQ = 8
PAST_LEN = 16384
PAGE_SIZE = 128

D_RG = D_MODEL // 2
RG_BLOCKS = 8
RG_BW = D_RG // RG_BLOCKS
CONV_W = 4
RG_C = 8.0
D_ML = D_MODEL - D_RG
ML_HEADS = 4
ML_DV = D_ML // ML_HEADS
ML_DK = ML_DV // 2
ML_CHUNK = 64
N_GROUPS = 4
EXP_PER_GROUP = 8
N_EXPERTS = N_GROUPS * EXP_PER_GROUP
TOP_K_IN_GROUP = 2
D_EXPERT = 512
EPS = 1e-6
D_IN = 2 * D_RG + 2 * ML_HEADS * ML_DK + 2 * D_ML + 2 * ML_HEADS

kernel_name = "hymba_rglru_mlstm_hmoe_adaln_step"


def _rms(x):
    xf = x.astype(jnp.float32)
    return (xf * lax.rsqrt(jnp.mean(xf * xf, axis=-1, keepdims=True) + EPS)).astype(x.dtype)


def _split_in(u):
    sizes = [D_RG, D_RG, ML_HEADS * ML_DK, ML_HEADS * ML_DK, D_ML, D_ML, ML_HEADS, ML_HEADS]
    offs, acc = [], 0
    for s in sizes[:-1]:
        acc += s
        offs.append(acc)
    return jnp.split(u, offs, axis=-1)


def _causal_conv(xr, buf, w, bias):
    L = xr.shape[1]
    xpad = jnp.concatenate([buf.astype(xr.dtype), xr], axis=1)
    out = bias + xpad[:, 0:L] * w[0]
    for j in range(1, CONV_W):
        out = out + xpad[:, j:j + L] * w[j]
    return out, xpad[:, -(CONV_W - 1):]


def _linear_scan(a, bx, h0):
    def comb(l, r):
        return (l[0] * r[0], r[0] * l[1] + r[1])
    A, Bc = lax.associative_scan(comb, (a, bx), axis=1)
    return A * h0[:, None] + Bc


def _rglru(xc, h0, w_ra, b_ra, w_ri, b_ri, lam):
    Bn, L, _ = xc.shape
    f32 = jnp.float32
    xf = xc.astype(f32)
    xb = xf.reshape(Bn, L, RG_BLOCKS, RG_BW)
    r = jax.nn.sigmoid(jnp.einsum('blnd,nde->blne', xb, w_ra.astype(f32)).reshape(Bn, L, D_RG) + b_ra)
    i = jax.nn.sigmoid(jnp.einsum('blnd,nde->blne', xb, w_ri.astype(f32)).reshape(Bn, L, D_RG) + b_ri)
    log_a = -RG_C * r * jax.nn.softplus(-lam.astype(f32))
    a = jnp.exp(log_a)
    mult = jnp.sqrt(-jnp.expm1(2.0 * log_a))
    h = _linear_scan(a, mult * (i * xf), h0.astype(f32))
    return h, h[:, -1]


def _mlstm_chunk(q, k, v, ig, lf, C, n, m):
    L = q.shape[2]
    b = jnp.cumsum(lf, axis=-1)
    dmat = b[..., :, None] - b[..., None, :] + ig[..., None, :]
    causal = jnp.tril(jnp.ones((L, L), dtype=bool))
    dmat = jnp.where(causal, dmat, -jnp.inf)
    m_inter = b + m[..., None]
    m_t = jnp.maximum(m_inter, jnp.max(dmat, axis=-1))
    s = jnp.einsum('bhtk,bhsk->bhts', q, k) * jnp.exp(dmat - m_t[..., None])
    w_inter = jnp.exp(m_inter - m_t)
    num = jnp.einsum('bhts,bhsv->bhtv', s, v) + w_inter[..., None] * jnp.einsum('bhtk,bhkv->bhtv', q, C)
    den = jnp.sum(s, axis=-1) + w_inter * jnp.einsum('bhtk,bhk->bht', q, n)
    h = num / jnp.maximum(jnp.abs(den), jnp.exp(-m_t))[..., None]
    m_new = m_t[..., -1]
    w_last = jnp.exp(b[..., -1:] - b + ig - m_new[..., None])
    decay = jnp.exp(m_inter[..., -1] - m_new)
    C_new = decay[..., None, None] * C + jnp.einsum('bhs,bhsk,bhsv->bhkv', w_last, k, v)
    n_new = decay[..., None] * n + jnp.einsum('bhs,bhsk->bhk', w_last, k)
    return h, (C_new, n_new, m_new)


def _mlstm_scan(q, k, v, ig, lf, C, n, m):
    Bn, H, L = q.shape[:3]
    nc = L // ML_CHUNK

    def chunks(t):
        return jnp.moveaxis(t.reshape(Bn, H, nc, ML_CHUNK, *t.shape[3:]), 2, 0)

    def step(carry, xs):
        h, carry = _mlstm_chunk(*xs, *carry)
        return carry, h

    state, hs = lax.scan(step, (C, n, m), (chunks(q), chunks(k), chunks(v), chunks(ig), chunks(lf)))
    h = jnp.moveaxis(hs, 0, 2).reshape(Bn, H, L, hs.shape[-1])
    return h, state


def _moe(h, w_grp, b_grp, w_er, b_er, w1, w3, w2):
    f32 = jnp.float32
    Bn, L, D = h.shape
    t = h.reshape(Bn * L, D)
    gl = jnp.einsum('td,dg->tg', t, w_grp).astype(f32) + b_grp
    g_sel = jnp.argmax(gl, axis=-1)
    p_g = jnp.max(jax.nn.softmax(gl, axis=-1), axis=-1)
    el = jnp.einsum('td,gde->tge', t, w_er).astype(f32) + b_er
    el = jnp.take_along_axis(el, g_sel[:, None, None], axis=1)[:, 0]
    top_v, top_i = lax.top_k(el, TOP_K_IN_GROUP)
    top_w = jax.nn.softmax(top_v, axis=-1) * p_g[:, None]
    eid = g_sel[:, None] * EXP_PER_GROUP + top_i
    gate = jnp.einsum('tk,tke->te', top_w, jax.nn.one_hot(eid, N_EXPERTS, dtype=f32))
    out = jnp.zeros((Bn * L, D), f32)
    for g in range(N_GROUPS):
        sl = slice(g * EXP_PER_GROUP, (g + 1) * EXP_PER_GROUP)
        a = jnp.einsum('td,edf->tef', t, w1[sl])
        bb = jnp.einsum('td,edf->tef', t, w3[sl])
        hid = (jax.nn.silu(a) * bb).astype(f32) * gate[:, sl, None]
        out = out + jnp.einsum('tef,efd->td', hid.astype(t.dtype), w2[sl]).astype(f32)
    return out.reshape(Bn, L, D).astype(h.dtype)


def _block(x, c, conv, hrg, C, n, m, W):
    (w_mod, b_mod, w_in, conv_w, conv_b, w_ra, b_ra, w_ri, b_ri, lam, g_rg,
     b_ig, b_fg, g_ml, w_out, w_grp, b_grp, w_er, b_er, w1, w3, w2) = W
    f32 = jnp.float32
    Bn, L, _ = x.shape
    mod = jnp.einsum('bd,de->be', jax.nn.silu(c), w_mod) + b_mod
    sh_a, sc_a, gt_a, sh_f, sc_f, gt_f = jnp.split(mod[:, None, :].astype(x.dtype), 6, axis=-1)
    h = _rms(x) * (1 + sc_a) + sh_a
    u = jnp.einsum('bld,de->ble', h, w_in)
    xr, yr, q, k, v, o, ig, fg = _split_in(u)
    xc, conv_new = _causal_conv(xr, conv, conv_w, conv_b)
    hr, h_new = _rglru(xc, hrg, w_ra, b_ra, w_ri, b_ri, lam)
    rg_out = _rms(hr) * g_rg * jax.nn.gelu(yr.astype(f32))
    qh = jnp.transpose(q.reshape(Bn, L, ML_HEADS, ML_DK).astype(f32), (0, 2, 1, 3)) * (ML_DK ** -0.5)
    kh = jnp.transpose(k.reshape(Bn, L, ML_HEADS, ML_DK).astype(f32), (0, 2, 1, 3))
    vh = jnp.transpose(v.reshape(Bn, L, ML_HEADS, ML_DV).astype(f32), (0, 2, 1, 3))
    igh = jnp.transpose(ig.astype(f32) + b_ig, (0, 2, 1))
    lfh = jnp.transpose(jax.nn.log_sigmoid(fg.astype(f32) + b_fg), (0, 2, 1))
    if L % ML_CHUNK == 0:
        hm, (C_new, n_new, m_new) = _mlstm_scan(qh, kh, vh, igh, lfh, C, n, m)
    else:
        hm, (C_new, n_new, m_new) = _mlstm_chunk(qh, kh, vh, igh, lfh, C, n, m)
    hm = jnp.transpose(hm, (0, 2, 1, 3))
    ml_out = (_rms(hm) * g_ml.reshape(ML_HEADS, ML_DV)).reshape(Bn, L, D_ML) * jax.nn.sigmoid(o.astype(f32))
    mix = jnp.concatenate([rg_out, ml_out], axis=-1).astype(x.dtype)
    x = x + gt_a * jnp.einsum('ble,ed->bld', mix, w_out)
    h2 = _rms(x) * (1 + sc_f) + sh_f
    x = x + gt_f * _moe(h2, w_grp, b_grp, w_er, b_er, w1, w3, w2)
    return x, (conv_new, h_new, C_new, n_new, m_new)


def setup_inputs(seed: int = 0) -> dict:
    key = jax.random.key(seed)
    ks = iter(jax.random.split(key, 48))
    f32 = jnp.float32

    def nrm(shape, scale):
        return jax.random.normal(next(ks), shape, f32) * scale

    def unif(shape, lo, hi):
        return jax.random.uniform(next(ks), shape, f32, lo, hi)

    a_target = unif((DEPTH, D_RG), 0.9, 0.999)
    a_base = a_target ** (1.0 / RG_C)
    lam = jnp.log(a_base) - jnp.log1p(-a_base)
    return {
        "x_prompt": nrm((BATCH, SEQ, D_MODEL), 1.0),
        "x_sample": nrm((DEC_BATCH, DEC_SEQ, D_MODEL), 1.0),
        "c_prompt": nrm((BATCH, D_MODEL), 1.0),
        "c_sample": nrm((DEC_BATCH, D_MODEL), 1.0),
        "state_rg_conv": nrm((DEPTH, DEC_BATCH, CONV_W - 1, D_RG), 1.0),
        "state_rg_h": nrm((DEPTH, DEC_BATCH, D_RG), 0.5),
        "state_mlstm_C": nrm((DEPTH, DEC_BATCH, ML_HEADS, ML_DK, ML_DV), 1.0),
        "state_mlstm_n": nrm((DEPTH, DEC_BATCH, ML_HEADS, ML_DK), 1.0),
        "state_mlstm_m": unif((DEPTH, DEC_BATCH, ML_HEADS), 0.0, 4.0),
        "w_mod": nrm((DEPTH, D_MODEL, 6 * D_MODEL), 0.5 * D_MODEL ** -0.5),
        "b_mod": nrm((DEPTH, 6 * D_MODEL), 0.02),
        "w_in": nrm((DEPTH, D_MODEL, D_IN), D_MODEL ** -0.5),
        "conv_w": nrm((DEPTH, CONV_W, D_RG), CONV_W ** -0.5),
        "conv_b": nrm((DEPTH, D_RG), 0.01),
        "w_ra": nrm((DEPTH, RG_BLOCKS, RG_BW, RG_BW), RG_BW ** -0.5),
        "b_ra": nrm((DEPTH, D_RG), 0.01),
        "w_ri": nrm((DEPTH, RG_BLOCKS, RG_BW, RG_BW), RG_BW ** -0.5),
        "b_ri": nrm((DEPTH, D_RG), 0.01),
        "lam": lam,
        "g_rg": 1.0 + nrm((DEPTH, D_RG), 0.02),
        "b_ig": nrm((DEPTH, ML_HEADS), 0.1),
        "b_fg": unif((DEPTH, ML_HEADS), 3.0, 6.0),
        "g_ml": 1.0 + nrm((DEPTH, D_ML), 0.02),
        "w_out": nrm((DEPTH, D_RG + D_ML, D_MODEL), (D_RG + D_ML) ** -0.5),
        "w_grp": nrm((DEPTH, D_MODEL, N_GROUPS), D_MODEL ** -0.5),
        "b_grp": nrm((DEPTH, N_GROUPS), 0.01),
        "w_er": nrm((DEPTH, N_GROUPS, D_MODEL, EXP_PER_GROUP), D_MODEL ** -0.5),
        "b_er": nrm((DEPTH, N_GROUPS, EXP_PER_GROUP), 0.01),
        "w1": nrm((DEPTH, N_EXPERTS, D_MODEL, D_EXPERT), D_MODEL ** -0.5),
        "w3": nrm((DEPTH, N_EXPERTS, D_MODEL, D_EXPERT), D_MODEL ** -0.5),
        "w2": nrm((DEPTH, N_EXPERTS, D_EXPERT, D_MODEL), D_EXPERT ** -0.5),
        "g_final": 1.0 + nrm((D_MODEL,), 0.02),
    }


def reference(x_prompt, x_sample, c_prompt, c_sample, state_rg_conv, state_rg_h, state_mlstm_C,
              state_mlstm_n, state_mlstm_m, w_mod, b_mod, w_in, conv_w, conv_b, w_ra, b_ra, w_ri, b_ri,
              lam, g_rg, b_ig, b_fg, g_ml, w_out, w_grp, b_grp, w_er, b_er, w1, w3, w2, g_final):
    f32 = jnp.float32
    bp = x_prompt.shape[0]
    zero_conv = jnp.zeros((bp, CONV_W - 1, D_RG), x_prompt.dtype)
    zero_h = jnp.zeros((bp, D_RG), f32)
    zero_C = jnp.zeros((bp, ML_HEADS, ML_DK, ML_DV), f32)
    zero_n = jnp.zeros((bp, ML_HEADS, ML_DK), f32)
    zero_m = jnp.zeros((bp, ML_HEADS), f32)
    xp, xs = x_prompt, x_sample
    new_p = [[], [], [], [], []]
    new_s = [[], [], [], [], []]
    for l in range(DEPTH):
        W = (w_mod[l], b_mod[l], w_in[l], conv_w[l], conv_b[l], w_ra[l], b_ra[l], w_ri[l], b_ri[l],
             lam[l], g_rg[l], b_ig[l], b_fg[l], g_ml[l], w_out[l], w_grp[l], b_grp[l], w_er[l], b_er[l],
             w1[l], w3[l], w2[l])
        xp, st_p = _block(xp, c_prompt, zero_conv, zero_h, zero_C, zero_n, zero_m, W)
        xs, st_s = _block(xs, c_sample, state_rg_conv[l], state_rg_h[l].astype(f32),
                          state_mlstm_C[l].astype(f32), state_mlstm_n[l].astype(f32),
                          state_mlstm_m[l].astype(f32), W)
        for lst, s in zip(new_p, st_p):
            lst.append(s)
        for lst, s in zip(new_s, st_s):
            lst.append(s)
    y_prompt = _rms(xp) * g_final.astype(xp.dtype)
    y_sample = _rms(xs) * g_final.astype(xs.dtype)
    pd = x_prompt.dtype
    return (y_prompt, y_sample,
            jnp.stack(new_p[0]).astype(pd), jnp.stack(new_p[1]).astype(pd), jnp.stack(new_p[2]).astype(pd),
            jnp.stack(new_p[3]).astype(pd), jnp.stack(new_p[4]).astype(pd),
            jnp.stack(new_s[0]).astype(state_rg_conv.dtype), jnp.stack(new_s[1]).astype(state_rg_h.dtype),
            jnp.stack(new_s[2]).astype(state_mlstm_C.dtype), jnp.stack(new_s[3]).astype(state_mlstm_n.dtype),
            jnp.stack(new_s[4]).astype(state_mlstm_m.dtype))
```

```python
import functools

import numpy as np
import jax
import jax.numpy as jnp
from jax import lax
from jax.experimental import pallas as pl
from jax.experimental.pallas import tpu as pltpu

F32 = jnp.float32
BF16 = jnp.bfloat16
EPS = 1e-6
RG_C = 8.0
SUBLANES = 8
LANES = 128
VMEM_LIMIT = 56 * 1024 * 1024
HI = lax.Precision.HIGHEST


def _cparams(*sem):
    return pltpu.CompilerParams(dimension_semantics=sem, vmem_limit_bytes=VMEM_LIMIT)


def _softplus(z):
    return jnp.maximum(z, 0.0) + jnp.log1p(jnp.exp(-jnp.abs(z)))


def _rows3(x):
    r, c = x.shape
    return x.reshape(r // SUBLANES, SUBLANES, c)


def _mod_kernel(c_ref, w_ref, b_ref, o_ref):
    c = c_ref[...]
    s = (c * jax.nn.sigmoid(c)).astype(BF16)
    o_ref[0] = jnp.dot(s, w_ref[0].astype(BF16), preferred_element_type=F32) + b_ref[0]


def _mod_call(c_all, w_mod, b_mod, tn):
    depth, d, n = w_mod.shape
    mc = c_all.shape[0]
    return pl.pallas_call(
        _mod_kernel,
        grid=(depth, n // tn),
        in_specs=[
            pl.BlockSpec((mc, d), lambda l, j: (0, 0)),
            pl.BlockSpec((1, d, tn), lambda l, j: (l, 0, j)),
            pl.BlockSpec((1, 1, tn), lambda l, j: (l, 0, j)),
        ],
        out_specs=pl.BlockSpec((1, mc, tn), lambda l, j: (l, 0, j)),
        out_shape=jax.ShapeDtypeStruct((depth, mc, n), F32),
        compiler_params=_cparams("arbitrary", "arbitrary"),
        name="mod",
    )(c_all, w_mod, b_mod.reshape(depth, 1, n))


def _in_kernel(x_ref, sh_ref, sc_ref, w_ref, wg_ref, bg_ref, u_ref, g_ref, h_scr, *, heads):
    j = pl.program_id(1)

    @pl.when(j == 0)
    def _():
        x = x_ref[...]
        tm, d = x.shape
        xn = x * lax.rsqrt(jnp.mean(x * x, axis=-1, keepdims=True) + EPS)
        h = _rows3(xn) * (1.0 + sc_ref[...]) + sh_ref[...]
        hb = h.reshape(tm, d).astype(BF16)
        h_scr[...] = hb
        g = jnp.dot(hb, wg_ref[...], preferred_element_type=F32) + bg_ref[...]
        lane = lax.broadcasted_iota(jnp.int32, g.shape, 1)
        is_forget = (lane >= heads) & (lane < 2 * heads)
        g_ref[...] = jnp.where(is_forget, -_softplus(-g), g)

    u_ref[...] = jnp.dot(h_scr[...], w_ref[...], preferred_element_type=F32)


def _in_call(x, mod8, w_main, w_gate, b_gate, heads, tm, tn):
    t, d = x.shape
    n = w_main.shape[1]
    g8 = tm // SUBLANES
    return pl.pallas_call(
        functools.partial(_in_kernel, heads=heads),
        grid=(t // tm, n // tn),
        in_specs=[
            pl.BlockSpec((tm, d), lambda i, j: (i, 0)),
            pl.BlockSpec((g8, 1, d), lambda i, j: (i, 0, 0)),
            pl.BlockSpec((g8, 1, d), lambda i, j: (i, 0, 1)),
            pl.BlockSpec((d, tn), lambda i, j: (0, j)),
            pl.BlockSpec((d, LANES), lambda i, j: (0, 0)),
            pl.BlockSpec((1, LANES), lambda i, j: (0, 0)),
        ],
        out_specs=[
            pl.BlockSpec((tm, tn), lambda i, j: (i, j)),
            pl.BlockSpec((tm, LANES), lambda i, j: (i, 0)),
        ],
        out_shape=[
            jax.ShapeDtypeStruct((t, n), F32),
            jax.ShapeDtypeStruct((t, LANES), F32),
        ],
        scratch_shapes=[pltpu.VMEM((tm, d), BF16)],
        compiler_params=_cparams("arbitrary", "arbitrary"),
        name="in_proj",
    )(x, mod8, mod8, w_main, w_gate, b_gate)


def _rg_kernel(xr_ref, yr_ref, prev_ref, h0_ref, cw_ref, cb_ref, wra_ref, bra_ref, wri_ref, bri_ref,
               lam_ref, gain_ref, hr_ref, mix_ref, xprev_scr, hprev_scr, hstart_scr, a_scr, b_scr,
               *, n_prompt_tiles, tiles_per_seq):
    i = pl.program_id(0)
    tl, c = xr_ref.shape
    groups = tl // SUBLANES
    nblk, bw, _ = wra_ref.shape
    is_prompt = i < n_prompt_tiles

    @pl.when(is_prompt & (i % tiles_per_seq == 0))
    def _():
        xprev_scr[...] = jnp.zeros_like(xprev_scr)
        hprev_scr[...] = jnp.zeros_like(hprev_scr)

    x = xr_ref[...]
    prev_prompt = jnp.concatenate([xprev_scr[...], x[:tl - SUBLANES]], axis=0)
    prev = jnp.where(is_prompt, prev_prompt, prev_ref[...])
    xprev_scr[...] = x[tl - SUBLANES:]

    x3 = _rows3(x)
    p3 = _rows3(prev)
    row = lax.broadcasted_iota(jnp.int32, x3.shape, 1)
    cw = cw_ref[...]
    conv_w = cw.shape[0]
    xc = cb_ref[...] + x3 * cw[conv_w - 1:conv_w]
    for dlt in range(1, conv_w):
        shifted = jnp.where(row >= dlt, pltpu.roll(x3, dlt, 1), pltpu.roll(p3, dlt, 1))
        xc = xc + shifted * cw[conv_w - 1 - dlt:conv_w - dlt]

    xc2 = xc.reshape(tl, c)
    xcb = xc2.astype(BF16)
    r_parts, i_parts = [], []
    for nb in range(nblk):
        blk = xcb[:, nb * bw:(nb + 1) * bw]
        r_parts.append(jnp.dot(blk, wra_ref[nb], preferred_element_type=F32))
        i_parts.append(jnp.dot(blk, wri_ref[nb], preferred_element_type=F32))
    r = jax.nn.sigmoid(jnp.concatenate(r_parts, axis=-1) + bra_ref[...])
    ig = jax.nn.sigmoid(jnp.concatenate(i_parts, axis=-1) + bri_ref[...])
    log_a = (-RG_C) * r * _softplus(-lam_ref[...])
    a = jnp.exp(log_a)
    th = jnp.tanh(log_a)
    mult = jnp.sqrt(-2.0 * th / (1.0 - th))
    bx = mult * (ig * xc2)

    av = _rows3(a)
    bv = _rows3(bx)
    for s in (1, 2, 4):
        a_sh = jnp.where(row >= s, pltpu.roll(av, s, 1), 1.0)
        b_sh = jnp.where(row >= s, pltpu.roll(bv, s, 1), 0.0)
        bv = av * b_sh + bv
        av = av * a_sh

    @pl.when(is_prompt)
    def _():
        a_scr[...] = av.reshape(tl, c)
        b_scr[...] = bv.reshape(tl, c)

        def body(g, h):
            hstart_scr[g] = h
            last = g * SUBLANES + (SUBLANES - 1)
            return a_scr[pl.ds(last, 1), :] * h + b_scr[pl.ds(last, 1), :]

        hprev_scr[...] = lax.fori_loop(0, groups, body, hprev_scr[...])

    @pl.when(jnp.logical_not(is_prompt))
    def _():
        hstart_scr[...] = h0_ref[...]

    h3 = av * hstart_scr[...] + bv
    hr = h3.reshape(tl, c)
    hr_ref[...] = hr
    y = yr_ref[...]
    gelu = 0.5 * y * (1.0 + jnp.tanh(np.sqrt(2.0 / np.pi).astype(np.float32) * (y + 0.044715 * (y * y * y))))
    hn = hr * lax.rsqrt(jnp.mean(hr * hr, axis=-1, keepdims=True) + EPS)
    mix_ref[...] = (hn * gain_ref[...] * gelu).astype(BF16)


def _rg_call(u, prev_s, h0_s, conv_w, conv_b, w_ra, b_ra, w_ri, b_ri, lam, g_rg, t_prompt, seq, tl):
    t = u.shape[0]
    c = conv_w.shape[1]
    nblk, bw, _ = w_ra.shape
    n_p = t_prompt // tl
    groups = tl // SUBLANES
    samp = lambda i: jnp.maximum(i - n_p, 0)
    full2 = lambda i: (0, 0)
    full3 = lambda i: (0, 0, 0)
    return pl.pallas_call(
        functools.partial(_rg_kernel, n_prompt_tiles=n_p, tiles_per_seq=seq // tl),
        grid=(t // tl,),
        in_specs=[
            pl.BlockSpec((tl, c), lambda i: (i, 0)),
            pl.BlockSpec((tl, c), lambda i: (i, 1)),
            pl.BlockSpec((tl, c), lambda i: (samp(i), 0)),
            pl.BlockSpec((groups, 1, c), lambda i: (samp(i), 0, 0)),
            pl.BlockSpec(conv_w.shape, full2),
            pl.BlockSpec((1, c), full2),
            pl.BlockSpec((nblk, bw, bw), full3),
            pl.BlockSpec((1, c), full2),
            pl.BlockSpec((nblk, bw, bw), full3),
            pl.BlockSpec((1, c), full2),
            pl.BlockSpec((1, c), full2),
            pl.BlockSpec((1, c), full2),
        ],
        out_specs=[
            pl.BlockSpec((tl, c), lambda i: (i, 0)),
            pl.BlockSpec((tl, c), lambda i: (i, 0)),
        ],
        out_shape=[
            jax.ShapeDtypeStruct((t, c), F32),
            jax.ShapeDtypeStruct((t, c), BF16),
        ],
        scratch_shapes=[
            pltpu.VMEM((SUBLANES, c), F32),
            pltpu.VMEM((1, c), F32),
            pltpu.VMEM((groups, 1, c), F32),
            pltpu.VMEM((tl, c), F32),
            pltpu.VMEM((tl, c), F32),
        ],
        compiler_params=_cparams("arbitrary"),
        name="rg_lru",
    )(u, u, prev_s, h0_s, conv_w, conv_b, w_ra, b_ra, w_ri, b_ri, lam, g_rg)


def _ml_chunk(nseq, heads, dk, dv, q_ref, k_ref, v_ref, o_ref, gc, gr, mprev, gain_ref, mix_ref,
              get_c, get_n, put_state):
    tc = q_ref.shape[0]
    ls = tc // nseq
    shift = int(np.log2(ls))
    t_idx = lax.broadcasted_iota(jnp.int32, (tc, tc), 0)
    s_idx = lax.broadcasted_iota(jnp.int32, (tc, tc), 1)
    mask = (lax.shift_right_logical(t_idx, shift) == lax.shift_right_logical(s_idx, shift)) & (s_idx <= t_idx)
    m_cum = mask.astype(F32)
    bcol = jnp.dot(m_cum, gc, precision=HI, preferred_element_type=F32)
    brow = lax.dot_general(gr, m_cum, (((1,), (1,)), ((), ())), precision=HI,
                           preferred_element_type=F32)
    lane = lax.broadcasted_iota(jnp.int32, (tc, LANES), 1)
    scale = np.float32(dk ** -0.5)

    def seg_last(col):
        if nseq == 1:
            return col[tc - 1:tc, :].reshape(1, 1, 1)
        return col.reshape(nseq, ls, 1)[:, ls - 1:ls, :]

    def seg_bcast(val):
        return jnp.broadcast_to(val, (nseq, ls, 1)).reshape(tc, 1)

    m_out = jnp.zeros((tc, LANES), F32)
    for h in range(heads):
        bc = bcol[:, heads + h:heads + h + 1]
        br = brow[heads + h:heads + h + 1, :]
        igr = gr[h:h + 1, :]
        igc = gc[:, h:h + 1]
        mp = mprev[:, h:h + 1]
        dmat = jnp.where(mask, bc - br + igr, -jnp.inf)
        m_inter = bc + mp
        m_t = jnp.maximum(m_inter, jnp.max(dmat, axis=-1, keepdims=True))
        p = jnp.exp(dmat - m_t)
        qh = q_ref[:, h * dk:(h + 1) * dk] * scale
        kh = k_ref[:, h * dk:(h + 1) * dk]
        vh = v_ref[:, h * dv:(h + 1) * dv]
        qb = qh.astype(BF16)
        kb = kh.astype(BF16)
        s = lax.dot_general(qb, kb, (((1,), (1,)), ((), ())), preferred_element_type=F32) * p
        w_inter = jnp.exp(m_inter - m_t)
        c_old = get_c(h)
        n_old = get_n(h)
        q3 = qh.reshape(nseq, ls, dk)
        inter = lax.dot_general(q3.astype(BF16), c_old.astype(BF16), (((2,), (1,)), ((0,), (0,))),
                                preferred_element_type=F32).reshape(tc, dv)
        qn = jnp.sum(q3 * n_old, axis=-1, keepdims=True).reshape(tc, 1)
        num = jnp.dot(s.astype(BF16), vh.astype(BF16), preferred_element_type=F32) + w_inter * inter
        den = jnp.sum(s, axis=-1, keepdims=True) + w_inter * qn
        hh = num / jnp.maximum(jnp.abs(den), jnp.exp(-m_t))
        hn = hh * lax.rsqrt(jnp.mean(hh * hh, axis=-1, keepdims=True) + EPS)
        og = jax.nn.sigmoid(o_ref[:, h * dv:(h + 1) * dv])
        mix_ref[:, h * dv:(h + 1) * dv] = (hn * gain_ref[:, h * dv:(h + 1) * dv] * og).astype(BF16)

        m_new = seg_last(m_t)
        b_last = seg_last(bc)
        wl = jnp.exp(seg_bcast(b_last) - bc + igc - seg_bcast(m_new))
        decay = jnp.exp(b_last + seg_last(mp) - m_new)
        wv3 = (wl * vh).reshape(nseq, ls, dv).astype(BF16)
        k3 = kh.reshape(nseq, ls, dk).astype(BF16)
        upd = lax.dot_general(k3, wv3, (((1,), (1,)), ((0,), (0,))), preferred_element_type=F32)
        c_new = decay * c_old + upd
        n_new = decay * n_old + jnp.sum((wl * kh).reshape(nseq, ls, dk), axis=1, keepdims=True)
        put_state(h, c_new, n_new)
        m_out = jnp.where(lane == h, seg_bcast(m_new), m_out)
    return m_out


def _ml_kernel(q_ref, k_ref, v_ref, o_ref, gc_ref, gr_ref, cs_ref, ns_ref, ms_ref, gain_ref,
               mix_ref, cp_out, np_out, mp_out, cs_out, ns_out, ms_out, c_scr, n_scr, m_scr,
               *, n_prompt_chunks, chunks_per_seq, heads, dk, dv, nseq_sample):
    i = pl.program_id(0)
    tc = q_ref.shape[0]
    is_prompt = i < n_prompt_chunks

    @pl.when(is_prompt & (i % chunks_per_seq == 0))
    def _():
        c_scr[...] = jnp.zeros_like(c_scr)
        n_scr[...] = jnp.zeros_like(n_scr)
        m_scr[...] = jnp.zeros_like(m_scr)

    @pl.when(is_prompt)
    def _():
        def put(h, c_new, n_new):
            c_scr[h] = c_new[0]
            n_scr[h] = n_new[0]

        mprev = jnp.broadcast_to(m_scr[...], (tc, LANES))
        m_out = _ml_chunk(1, heads, dk, dv, q_ref, k_ref, v_ref, o_ref, gc_ref[...], gr_ref[...], mprev,
                          gain_ref, mix_ref, lambda h: c_scr[h][None], lambda h: n_scr[h][None], put)
        m_scr[...] = m_out[:1, :]
        cp_out[0] = c_scr[...]
        np_out[0] = n_scr[...]
        mp_out[0] = m_out[:1, :]

    @pl.when(jnp.logical_not(is_prompt))
    def _():
        def put(h, c_new, n_new):
            cs_out[:, h] = c_new
            ns_out[:, h] = n_new

        m_out = _ml_chunk(nseq_sample, heads, dk, dv, q_ref, k_ref, v_ref, o_ref, gc_ref[...], gr_ref[...],
                          ms_ref[...], gain_ref, mix_ref, lambda h: cs_ref[:, h], lambda h: ns_ref[:, h], put)
        ms_out[...] = m_out


def _ml_call(u, gcol, grow, c_s, n_s, m_s_col, g_ml, t_prompt, seq, dec_seq, q_off, k_off, v_off, o_off, tc):
    t = u.shape[0]
    bs, heads, dk, dv = c_s.shape
    bp = t_prompt // seq
    hdk, hdv = heads * dk, heads * dv
    n_p = t_prompt // tc
    cps = seq // tc
    nseq = tc // dec_seq
    samp = lambda i: jnp.maximum(i - n_p, 0)
    pseq = lambda i: jnp.minimum(i // cps, bp - 1)
    t_s = t - t_prompt
    return pl.pallas_call(
        functools.partial(_ml_kernel, n_prompt_chunks=n_p, chunks_per_seq=cps, heads=heads, dk=dk, dv=dv,
                          nseq_sample=nseq),
        grid=(t // tc,),
        in_specs=[
            pl.BlockSpec((tc, hdk), lambda i: (i, q_off // hdk)),
            pl.BlockSpec((tc, hdk), lambda i: (i, k_off // hdk)),
            pl.BlockSpec((tc, hdv), lambda i: (i, v_off // hdv)),
            pl.BlockSpec((tc, hdv), lambda i: (i, o_off // hdv)),
            pl.BlockSpec((tc, LANES), lambda i: (i, 0)),
            pl.BlockSpec((2 * heads, tc), lambda i: (0, i)),
            pl.BlockSpec((nseq, heads, dk, dv), lambda i: (samp(i), 0, 0, 0)),
            pl.BlockSpec((nseq, heads, 1, dk), lambda i: (samp(i), 0, 0, 0)),
            pl.BlockSpec((tc, LANES), lambda i: (samp(i), 0)),
            pl.BlockSpec((1, hdv), lambda i: (0, 0)),
        ],
        out_specs=[
            pl.BlockSpec((tc, hdv), lambda i: (i, 0)),
            pl.BlockSpec((1, heads, dk, dv), lambda i: (pseq(i), 0, 0, 0)),
            pl.BlockSpec((1, heads, 1, dk), lambda i: (pseq(i), 0, 0, 0)),
            pl.BlockSpec((1, 1, LANES), lambda i: (pseq(i), 0, 0)),
            pl.BlockSpec((nseq, heads, dk, dv), lambda i: (samp(i), 0, 0, 0)),
            pl.BlockSpec((nseq, heads, 1, dk), lambda i: (samp(i), 0, 0, 0)),
            pl.BlockSpec((tc, LANES), lambda i: (samp(i), 0)),
        ],
        out_shape=[
            jax.ShapeDtypeStruct((t, hdv), BF16),
            jax.ShapeDtypeStruct((bp, heads, dk, dv), F32),
            jax.ShapeDtypeStruct((bp, heads, 1, dk), F32),
            jax.ShapeDtypeStruct((bp, 1, LANES), F32),
            jax.ShapeDtypeStruct((bs, heads, dk, dv), F32),
            jax.ShapeDtypeStruct((bs, heads, 1, dk), F32),
            jax.ShapeDtypeStruct((t_s, LANES), F32),
        ],
        scratch_shapes=[
            pltpu.VMEM((heads, dk, dv), F32),
            pltpu.VMEM((heads, 1, dk), F32),
            pltpu.VMEM((1, LANES), F32),
        ],
        compiler_params=_cparams("arbitrary"),
        name="mlstm",
    )(u, u, u, u, gcol, grow, c_s, n_s, m_s_col, g_ml)


def _out_kernel(x_ref, mr_ref, mm_ref, gt_ref, sh_ref, sc_ref, wo_ref, wr_ref, br_ref,
                xn_ref, h2_ref, eid_ref, wt_ref, *, n_groups, per_group):
    tm, d = x_ref.shape
    c = mr_ref.shape[1]
    acc = jnp.dot(mr_ref[...], wo_ref[:c, :], preferred_element_type=F32)
    acc = acc + jnp.dot(mm_ref[...], wo_ref[c:, :], preferred_element_type=F32)
    xn3 = _rows3(x_ref[...]) + gt_ref[...] * _rows3(acc)
    rs = lax.rsqrt(jnp.mean(xn3 * xn3, axis=-1, keepdims=True) + EPS)
    h2 = (xn3 * rs * (1.0 + sc_ref[...]) + sh_ref[...]).reshape(tm, d)
    xn_ref[...] = xn3.reshape(tm, d)
    h2_ref[...] = h2

    logits = jnp.dot(h2.astype(BF16), wr_ref[...], preferred_element_type=F32) + br_ref[...]
    lane = lax.broadcasted_iota(jnp.int32, logits.shape, 1)
    lane_f = lane.astype(F32)
    big = np.float32(LANES)

    def first_max(vals):
        vmax = jnp.max(vals, axis=-1, keepdims=True)
        idx = jnp.min(jnp.where(vals == vmax, lane_f, big), axis=-1, keepdims=True)
        return vmax, idx.astype(jnp.int32)

    is_grp = lane < n_groups
    gl = jnp.where(is_grp, logits, -jnp.inf)
    gmax, g_sel = first_max(gl)
    p_g = 1.0 / jnp.sum(jnp.exp(gl - gmax), axis=-1, keepdims=True)
    lo = n_groups + g_sel * per_group
    el = jnp.where((lane >= lo) & (lane < lo + per_group), logits, -jnp.inf)
    v1, i1 = first_max(el)
    v2, i2 = first_max(jnp.where(lane == i1, -jnp.inf, el))
    e21 = jnp.exp(v2 - v1)
    w1 = 1.0 / (1.0 + e21)
    w2 = e21 * w1
    eid_ref[...] = jnp.where(lane == 0, i1 - n_groups, jnp.where(lane == 1, i2 - n_groups, 0))
    wt_ref[...] = jnp.where(lane == 0, w1 * p_g, jnp.where(lane == 1, w2 * p_g, 0.0))


def _out_call(x, mix_rg, mix_ml, mod8, w_out, w_router, b_router, n_groups, per_group, tm):
    t, d = x.shape
    c = mix_rg.shape[1]
    cm = mix_ml.shape[1]
    g8 = tm // SUBLANES
    return pl.pallas_call(
        functools.partial(_out_kernel, n_groups=n_groups, per_group=per_group),
        grid=(t // tm,),
        in_specs=[
            pl.BlockSpec((tm, d), lambda i: (i, 0)),
            pl.BlockSpec((tm, c), lambda i: (i, 0)),
            pl.BlockSpec((tm, cm), lambda i: (i, 0)),
            pl.BlockSpec((g8, 1, d), lambda i: (i, 0, 2)),
            pl.BlockSpec((g8, 1, d), lambda i: (i, 0, 3)),
            pl.BlockSpec((g8, 1, d), lambda i: (i, 0, 4)),
            pl.BlockSpec((c + cm, d), lambda i: (0, 0)),
            pl.BlockSpec((d, LANES), lambda i: (0, 0)),
            pl.BlockSpec((1, LANES), lambda i: (0, 0)),
        ],
        out_specs=[
            pl.BlockSpec((tm, d), lambda i: (i, 0)),
            pl.BlockSpec((tm, d), lambda i: (i, 0)),
            pl.BlockSpec((tm, LANES), lambda i: (i, 0)),
            pl.BlockSpec((tm, LANES), lambda i: (i, 0)),
        ],
        out_shape=[
            jax.ShapeDtypeStruct((t, d), F32),
            jax.ShapeDtypeStruct((t, d), F32),
            jax.ShapeDtypeStruct((t, LANES), jnp.int32),
            jax.ShapeDtypeStruct((t, LANES), F32),
        ],
        compiler_params=_cparams("arbitrary"),
        name="out_proj_router",
    )(x, mix_rg, mix_ml, mod8, mod8, mod8, w_out, w_router, b_router)


def _gather_rows(idx_ref, base, n_rows, src_hbm, dst, sem, stride=1, offset=0):
    def body(r, carry):
        tok = idx_ref[base + r * stride + offset]
        pltpu.make_async_copy(src_hbm.at[pl.ds(tok, 1)], dst.at[pl.ds(r, 1)], sem).start()
        return carry

    lax.fori_loop(0, n_rows, body, 0)


def _moe_kernel(te_ref, nu_ref, src_ref, h2_hbm, ws_ref, w1_ref, w3_ref, w2_ref, y_ref, xbuf, sem):
    j = pl.program_id(0)
    tm = xbuf.shape[1]
    n_used = nu_ref[0]

    def issue(tile, slot):
        _gather_rows(src_ref, tile * tm, tm, h2_hbm, xbuf.at[slot], sem.at[slot])

    @pl.when(j == 0)
    def _():
        issue(0, 0)

    @pl.when(j + 1 < n_used)
    def _():
        issue(j + 1, (j + 1) % 2)

    @pl.when(j < n_used)
    def _():
        slot = j % 2
        pltpu.make_async_copy(h2_hbm.at[pl.ds(0, tm)], xbuf.at[slot], sem.at[slot]).wait()
        xb = xbuf[slot].astype(BF16)
        a = jnp.dot(xb, w1_ref[0], preferred_element_type=F32)
        b = jnp.dot(xb, w3_ref[0], preferred_element_type=F32)
        hid = (a * jax.nn.sigmoid(a)) * b * ws_ref[...]
        y_ref[...] = jnp.dot(hid.astype(BF16), w2_ref[0], preferred_element_type=F32)

    @pl.when(j >= n_used)
    def _():
        y_ref[...] = jnp.zeros_like(y_ref)


def _moe_call(tile_exp, n_used, src_tok, h2, w_sorted, w1, w3, w2, tm):
    t, d = h2.shape
    ne, _, f = w1.shape
    n_tiles = tile_exp.shape[0]
    grid_spec = pltpu.PrefetchScalarGridSpec(
        num_scalar_prefetch=3,
        grid=(n_tiles,),
        in_specs=[
            pl.BlockSpec(memory_space=pl.ANY),
            pl.BlockSpec((tm, 1), lambda j, te, nu, src: (j, 0)),
            pl.BlockSpec((1, d, f), lambda j, te, nu, src: (te[j], 0, 0)),
            pl.BlockSpec((1, d, f), lambda j, te, nu, src: (te[j], 0, 0)),
            pl.BlockSpec((1, f, d), lambda j, te, nu, src: (te[j], 0, 0)),
        ],
        out_specs=pl.BlockSpec((tm, d), lambda j, te, nu, src: (j, 0)),
        scratch_shapes=[pltpu.VMEM((2, tm, d), F32), pltpu.SemaphoreType.DMA((2,))],
    )
    return pl.pallas_call(
        _moe_kernel,
        grid_spec=grid_spec,
        out_shape=jax.ShapeDtypeStruct((n_tiles * tm, d), F32),
        compiler_params=_cparams("arbitrary"),
        name="experts",
    )(tile_exp, n_used, src_tok, h2, w_sorted, w1, w3, w2)


def _comb_kernel(dest_ref, xn_ref, gt_ref, gf_ref, y_hbm, xo_ref, ybuf, sem, *, final):
    i = pl.program_id(0)
    n = pl.num_programs(0)
    tm, d = xn_ref.shape

    def issue(tile, slot):
        for k in range(2):
            _gather_rows(dest_ref, tile * tm * 2, tm, y_hbm, ybuf.at[slot, k], sem.at[slot], stride=2, offset=k)

    @pl.when(i == 0)
    def _():
        issue(0, 0)

    @pl.when(i + 1 < n)
    def _():
        issue(i + 1, (i + 1) % 2)

    slot = i % 2
    for k in range(2):
        pltpu.make_async_copy(y_hbm.at[pl.ds(0, tm)], ybuf.at[slot, k], sem.at[slot]).wait()
    y = ybuf[slot, 0] + ybuf[slot, 1]
    xo = (_rows3(xn_ref[...]) + gt_ref[...] * _rows3(y)).reshape(tm, d)
    if final:
        xo = xo * lax.rsqrt(jnp.mean(xo * xo, axis=-1, keepdims=True) + EPS) * gf_ref[...]
    xo_ref[...] = xo


def _comb_call(dest, xn, mod8, g_final, y, tm, final):
    t, d = xn.shape
    g8 = tm // SUBLANES
    grid_spec = pltpu.PrefetchScalarGridSpec(
        num_scalar_prefetch=1,
        grid=(t // tm,),
        in_specs=[
            pl.BlockSpec((tm, d), lambda i, dst: (i, 0)),
            pl.BlockSpec((g8, 1, d), lambda i, dst: (i, 0, 5)),
            pl.BlockSpec((1, d), lambda i, dst: (0, 0)),
            pl.BlockSpec(memory_space=pl.ANY),
        ],
        out_specs=pl.BlockSpec((tm, d), lambda i, dst: (i, 0)),
        scratch_shapes=[pltpu.VMEM((2, 2, tm, d), F32), pltpu.SemaphoreType.DMA((2,))],
    )
    return pl.pallas_call(
        functools.partial(_comb_kernel, final=final),
        grid_spec=grid_spec,
        out_shape=jax.ShapeDtypeStruct((t, d), F32),
        compiler_params=_cparams("arbitrary"),
        name="combine",
    )(dest, xn, mod8, g_final, y)


def _route_plan(eid, wts, n_experts, tm):
    t = eid.shape[0]
    e_flat = eid[:, :2].reshape(-1)
    w_flat = wts[:, :2].reshape(-1)
    n_pairs = 2 * t
    n_tiles = -(-n_pairs // tm) + n_experts
    onehot = (e_flat[:, None] == jnp.arange(n_experts, dtype=jnp.int32)[None, :]).astype(jnp.int32)
    csum = jnp.cumsum(onehot, axis=0)
    rank = jnp.sum(csum * onehot, axis=1) - 1
    counts = csum[-1]
    padded = ((counts + tm - 1) // tm) * tm
    ends = jnp.cumsum(padded)
    starts = ends - padded
    dest = (jnp.sum(starts[None, :] * onehot, axis=1) + rank).astype(jnp.int32)
    src_tok = jnp.zeros((n_tiles * tm,), jnp.int32).at[dest].set(jnp.arange(n_pairs, dtype=jnp.int32) // 2)
    w_sorted = jnp.zeros((n_tiles * tm,), F32).at[dest].set(w_flat)
    n_used = (ends[-1] // tm).astype(jnp.int32)
    tile_id = jnp.minimum(jnp.arange(n_tiles, dtype=jnp.int32), n_used - 1)
    tile_exp = jnp.sum((ends[None, :] <= (tile_id * tm)[:, None]).astype(jnp.int32), axis=1)
    tile_exp = jnp.minimum(tile_exp, n_experts - 1).astype(jnp.int32)
    return tile_exp, n_used.reshape(1), src_tok, w_sorted.reshape(-1, 1), dest


def _pick_tile(total, pref):
    tile = min(pref, total)
    while total % tile:
        tile //= 2
    return tile


def kernel(x_prompt, x_sample, c_prompt, c_sample, state_rg_conv, state_rg_h, state_mlstm_C, state_mlstm_n,
           state_mlstm_m, w_mod, b_mod, w_in, conv_w, conv_b, w_ra, b_ra, w_ri, b_ri, lam, g_rg, b_ig, b_fg,
           g_ml, w_out, w_grp, b_grp, w_er, b_er, w1, w3, w2, g_final):
    bp, seq, d = x_prompt.shape
    bs, dec_seq, _ = x_sample.shape
    depth = w_mod.shape[0]
    d_rg = conv_w.shape[2]
    conv_width = conv_w.shape[1]
    _, _, heads, dk, dv = state_mlstm_C.shape
    d_ml = heads * dv
    n_groups, per_group = w_er.shape[1], w_er.shape[3]
    n_experts = w1.shape[1]
    t_p, t_s = bp * seq, bs * dec_seq
    t = t_p + t_s
    n_main = 2 * d_rg + 2 * heads * dk + 2 * d_ml
    q_off, k_off, v_off, o_off = 2 * d_rg, 2 * d_rg + heads * dk, 2 * d_rg + 2 * heads * dk, n_main - d_ml
    assert dec_seq == SUBLANES and seq % LANES == 0 and conv_width - 1 <= SUBLANES
    assert 2 * heads <= LANES and n_groups * (per_group + 1) <= LANES
    assert q_off % (heads * dk) == 0 and k_off % (heads * dk) == 0 and v_off % d_ml == 0 and o_off % d_ml == 0

    tile_tok = _pick_tile(np.gcd(seq, t_s), 256)
    tm_in = _pick_tile(np.gcd(t_p, t_s), 512)
    tc = LANES
    assert t_s % tc == 0 and seq % tc == 0

    mc = -(-(bp + bs) // SUBLANES) * SUBLANES
    c_all = jnp.zeros((mc, d), F32).at[:bp].set(c_prompt).at[bp:bp + bs].set(c_sample)
    mod = _mod_call(c_all, w_mod, b_mod, _pick_tile(6 * d, 1024))
    grp_seq = np.concatenate([np.repeat(np.arange(bp), seq // SUBLANES),
                              bp + np.repeat(np.arange(bs), dec_seq // SUBLANES)]).astype(np.int32)
    mod8_all = jnp.take(mod, jnp.asarray(grp_seq), axis=1).reshape(depth, t // SUBLANES, 1, 6 * d)

    w_main = w_in[:, :, :n_main].astype(BF16)
    w_gate = jnp.zeros((depth, d, LANES), F32).at[:, :, :2 * heads].set(w_in[:, :, n_main:]).astype(BF16)
    b_gate = jnp.zeros((depth, 1, LANES), F32).at[:, 0, :heads].set(b_ig).at[:, 0, heads:2 * heads].set(b_fg)
    w_ra_b, w_ri_b = w_ra.astype(BF16), w_ri.astype(BF16)
    w_out_b = w_out.astype(BF16)
    n_rt = n_groups * (per_group + 1)
    w_router = jnp.concatenate([w_grp, jnp.moveaxis(w_er, 1, 2).reshape(depth, d, n_groups * per_group)], axis=-1)
    w_router = jnp.zeros((depth, d, LANES), F32).at[:, :, :n_rt].set(w_router).astype(BF16)
    b_router = jnp.concatenate([b_grp, b_er.reshape(depth, -1)], axis=-1)
    b_router = jnp.zeros((depth, 1, LANES), F32).at[:, 0, :n_rt].set(b_router)
    w1_b, w3_b, w2_b = w1.astype(BF16), w3.astype(BF16), w2.astype(BF16)

    pad_rows = SUBLANES - (conv_width - 1)
    prev_s = jnp.pad(state_rg_conv, ((0, 0), (0, 0), (pad_rows, 0), (0, 0))).reshape(depth, bs * SUBLANES, d_rg)
    h0_s = state_rg_h.reshape(depth, bs, 1, d_rg)
    n_s = state_mlstm_n.reshape(depth, bs, heads, 1, dk)
    m_s_col = jnp.pad(jnp.repeat(state_mlstm_m, dec_seq, axis=1), ((0, 0), (0, 0), (0, LANES - heads)))

    x = jnp.concatenate([x_prompt.reshape(t_p, d), x_sample.reshape(t_s, d)], axis=0)
    outs_p = [[] for _ in range(5)]
    outs_s = [[] for _ in range(5)]
    for l in range(depth):
        mod8 = mod8_all[l]
        u, gcol = _in_call(x, mod8, w_main[l], w_gate[l], b_gate[l], heads, tm_in, _pick_tile(n_main, 1024))
        hr, mix_rg = _rg_call(u, prev_s[l], h0_s[l], conv_w[l], conv_b[l][None], w_ra_b[l], b_ra[l][None],
                              w_ri_b[l], b_ri[l][None], lam[l][None], g_rg[l][None], t_p, seq, tile_tok)
        grow = gcol[:, :2 * heads].T
        mix_ml, c_p, n_p, m_p, c_s, n_so, m_so = _ml_call(
            u, gcol, grow, state_mlstm_C[l], n_s[l], m_s_col[l], g_ml[l][None], t_p, seq, dec_seq,
            q_off, k_off, v_off, o_off, tc)
        xn, h2, eid, wts = _out_call(x, mix_rg, mix_ml, mod8, w_out_b[l], w_router[l], b_router[l],
                                     n_groups, per_group, tile_tok)
        tile_exp, n_used, src_tok, w_sorted, dest = _route_plan(eid, wts, n_experts, tile_tok)
        y = _moe_call(tile_exp, n_used, src_tok, h2, w_sorted, w1_b[l], w3_b[l], w2_b[l], tile_tok)
        x = _comb_call(dest, xn, mod8, g_final[None], y, tile_tok, l == depth - 1)

        xr_p = u[:t_p, :d_rg].reshape(bp, seq, d_rg)
        xr_s = u[t_p:, :d_rg].reshape(bs, dec_seq, d_rg)
        outs_p[0].append(xr_p[:, seq - (conv_width - 1):])
        outs_p[1].append(hr[:t_p].reshape(bp, seq, d_rg)[:, -1])
        outs_p[2].append(c_p)
        outs_p[3].append(n_p.reshape(bp, heads, dk))
        outs_p[4].append(m_p[:, 0, :heads])
        outs_s[0].append(xr_s[:, dec_seq - (conv_width - 1):])
        outs_s[1].append(hr[t_p:].reshape(bs, dec_seq, d_rg)[:, -1])
        outs_s[2].append(c_s)
        outs_s[3].append(n_so.reshape(bs, heads, dk))
        outs_s[4].append(m_so.reshape(bs, dec_seq, LANES)[:, -1, :heads])

    y_prompt = x[:t_p].reshape(bp, seq, d)
    y_sample = x[t_p:].reshape(bs, dec_seq, d)
    return (y_prompt, y_sample, *[jnp.stack(o) for o in outs_p], *[jnp.stack(o) for o in outs_s])
```

```python
import functools

import numpy as np
import jax
import jax.numpy as jnp
from jax import lax
from jax.experimental import pallas as pl
from jax.experimental.pallas import tpu as pltpu

F32 = jnp.float32
BF16 = jnp.bfloat16
EPS = 1e-6
RG_C = 8.0
SUBLANES = 8
LANES = 128
VMEM_LIMIT = 56 * 1024 * 1024
HI = lax.Precision.HIGHEST
GATHER_UNROLL = 8
SH_A, SC_A, GT_A, SH_F, SC_F, GT_F = range(6)


def _cparams(*sem):
    return pltpu.CompilerParams(dimension_semantics=sem, vmem_limit_bytes=VMEM_LIMIT)


def _softplus(z):
    return jnp.maximum(z, 0.0) + jnp.log1p(jnp.exp(-jnp.abs(z)))


def _rows3(x):
    r, c = x.shape
    return x.reshape(r // SUBLANES, SUBLANES, c)


class _Tokens:
    def __init__(self, t_prompt, t_sample, seq, tile):
        self.tile = tile
        self.groups = tile // SUBLANES
        self.n_prompt_tiles = t_prompt // tile
        self.tiles_per_seq = seq // tile
        self.n_prompt_seqs = t_prompt // seq

    def prompt_seq(self, i):
        return jnp.minimum(i // self.tiles_per_seq, self.n_prompt_seqs - 1)

    def sample_tile(self, i):
        return jnp.maximum(i - self.n_prompt_tiles, 0)

    def mod_specs(self, layer, chunk, d, tile_axis=0):
        def prompt_map(*idx):
            return (layer, self.prompt_seq(idx[tile_axis]), 0, chunk)

        def sample_map(*idx):
            return (layer, self.sample_tile(idx[tile_axis]), 0, chunk)

        return [pl.BlockSpec((1, 1, 1, d), prompt_map), pl.BlockSpec((1, self.groups, 1, d), sample_map)]


def _pick_mod(is_prompt, mp_ref, ms_ref):
    return jnp.where(is_prompt, mp_ref[0], ms_ref[0])


def _mod_kernel(c_ref, w_ref, b_ref, o_ref):
    c = c_ref[...]
    s = (c * jax.nn.sigmoid(c)).astype(BF16)
    o_ref[0] = jnp.dot(s, w_ref[0].astype(BF16), preferred_element_type=F32) + b_ref[0]


def _mod_call(c_all, w_mod, b_mod, tn):
    depth, d, n = w_mod.shape
    mc = c_all.shape[0]
    return pl.pallas_call(
        _mod_kernel,
        grid=(depth, n // tn),
        in_specs=[
            pl.BlockSpec((mc, d), lambda l, j: (0, 0)),
            pl.BlockSpec((1, d, tn), lambda l, j: (l, 0, j)),
            pl.BlockSpec((1, 1, tn), lambda l, j: (l, 0, j)),
        ],
        out_specs=pl.BlockSpec((1, mc, tn), lambda l, j: (l, 0, j)),
        out_shape=jax.ShapeDtypeStruct((depth, mc, n), F32),
        compiler_params=_cparams("arbitrary", "arbitrary"),
        name="mod",
    )(c_all, w_mod, b_mod.reshape(depth, 1, n))


def _norm_gates(x3, sc, sh, wg_ref, bg_ref, h_ref, g_ref, heads):
    groups, _, d = x3.shape
    xn = x3 * lax.rsqrt(jnp.mean(x3 * x3, axis=-1, keepdims=True) + EPS)
    hb = (xn * (1.0 + sc) + sh).reshape(groups * SUBLANES, d).astype(BF16)
    h_ref[...] = hb
    g = jnp.dot(hb, wg_ref[0], preferred_element_type=F32) + bg_ref[0]
    lane = lax.broadcasted_iota(jnp.int32, g.shape, 1)
    is_forget = (lane >= heads) & (lane < 2 * heads)
    g_ref[...] = jnp.where(is_forget, -_softplus(-g), g)


def _prenorm_kernel(x_ref, shp_ref, shs_ref, scp_ref, scs_ref, wg_ref, bg_ref, h_ref, g_ref, *, heads, n_prompt_tiles):
    is_prompt = pl.program_id(0) < n_prompt_tiles
    _norm_gates(_rows3(x_ref[...]), _pick_mod(is_prompt, scp_ref, scs_ref), _pick_mod(is_prompt, shp_ref, shs_ref),
                wg_ref, bg_ref, h_ref, g_ref, heads)


def _prenorm_call(x, mod_p, mod_s, w_gate, b_gate, heads, tok):
    t, d = x.shape
    tm = tok.tile
    return pl.pallas_call(
        functools.partial(_prenorm_kernel, heads=heads, n_prompt_tiles=tok.n_prompt_tiles),
        grid=(t // tm,),
        in_specs=[
            pl.BlockSpec((tm, d), lambda i: (i, 0)),
            *tok.mod_specs(0, SH_A, d),
            *tok.mod_specs(0, SC_A, d),
            pl.BlockSpec((1, d, LANES), lambda i: (0, 0, 0)),
            pl.BlockSpec((1, 1, LANES), lambda i: (0, 0, 0)),
        ],
        out_specs=[
            pl.BlockSpec((tm, d), lambda i: (i, 0)),
            pl.BlockSpec((tm, LANES), lambda i: (i, 0)),
        ],
        out_shape=[
            jax.ShapeDtypeStruct((t, d), BF16),
            jax.ShapeDtypeStruct((t, LANES), F32),
        ],
        compiler_params=_cparams("arbitrary"),
        name="prenorm",
    )(x, mod_p, mod_s, mod_p, mod_s, w_gate, b_gate)


def _in_kernel(h_ref, w_ref, u_ref, wb_scr):
    @pl.when(pl.program_id(1) == 0)
    def _():
        wb_scr[...] = w_ref[0].astype(BF16)

    u_ref[...] = jnp.dot(h_ref[...], wb_scr[...], preferred_element_type=F32)


def _in_call(h, w_in, layer, n_main, tm, tn):
    t, d = h.shape
    return pl.pallas_call(
        _in_kernel,
        grid=(n_main // tn, t // tm),
        in_specs=[
            pl.BlockSpec((tm, d), lambda j, i: (i, 0)),
            pl.BlockSpec((1, d, tn), lambda j, i: (layer, 0, j)),
        ],
        out_specs=pl.BlockSpec((tm, tn), lambda j, i: (i, j)),
        out_shape=jax.ShapeDtypeStruct((t, n_main), F32),
        scratch_shapes=[pltpu.VMEM((d, tn), BF16)],
        compiler_params=_cparams("arbitrary", "arbitrary"),
        name="in_proj",
    )(h, w_in)


def _rg_kernel(xr_ref, yr_ref, prev_ref, h0_ref, cw_ref, cb_ref, wra_ref, bra_ref, wri_ref, bri_ref,
               lam_ref, gain_ref, hr_ref, mix_ref, xprev_scr, hprev_scr, hstart_scr, a_scr, b_scr,
               *, n_prompt_tiles, tiles_per_seq):
    i = pl.program_id(0)
    tl, c = xr_ref.shape
    groups = tl // SUBLANES
    _, nblk, bw, _ = wra_ref.shape
    is_prompt = i < n_prompt_tiles

    @pl.when(is_prompt & (i % tiles_per_seq == 0))
    def _():
        xprev_scr[...] = jnp.zeros_like(xprev_scr)
        hprev_scr[...] = jnp.zeros_like(hprev_scr)

    x = xr_ref[...]
    prev_prompt = jnp.concatenate([xprev_scr[...], x[:tl - SUBLANES]], axis=0)
    prev = jnp.where(is_prompt, prev_prompt, prev_ref[0])
    xprev_scr[...] = x[tl - SUBLANES:]

    x3 = _rows3(x)
    p3 = _rows3(prev)
    row = lax.broadcasted_iota(jnp.int32, x3.shape, 1)
    cw = cw_ref[0]
    conv_w = cw.shape[0]
    xc = cb_ref[0] + x3 * cw[conv_w - 1:conv_w]
    for dlt in range(1, conv_w):
        shifted = jnp.where(row >= dlt, pltpu.roll(x3, dlt, 1), pltpu.roll(p3, dlt, 1))
        xc = xc + shifted * cw[conv_w - 1 - dlt:conv_w - dlt]

    xc2 = xc.reshape(tl, c)
    xcb = xc2.astype(BF16)
    r_parts, i_parts = [], []
    for nb in range(nblk):
        blk = xcb[:, nb * bw:(nb + 1) * bw]
        r_parts.append(jnp.dot(blk, wra_ref[0, nb], preferred_element_type=F32))
        i_parts.append(jnp.dot(blk, wri_ref[0, nb], preferred_element_type=F32))
    r = jax.nn.sigmoid(jnp.concatenate(r_parts, axis=-1) + bra_ref[0])
    ig = jax.nn.sigmoid(jnp.concatenate(i_parts, axis=-1) + bri_ref[0])
    log_a = (-RG_C) * r * _softplus(-lam_ref[0])
    a = jnp.exp(log_a)
    th = jnp.tanh(log_a)
    mult = jnp.sqrt(-2.0 * th / (1.0 - th))
    bx = mult * (ig * xc2)

    av = _rows3(a)
    bv = _rows3(bx)
    for s in (1, 2, 4):
        a_sh = jnp.where(row >= s, pltpu.roll(av, s, 1), 1.0)
        b_sh = jnp.where(row >= s, pltpu.roll(bv, s, 1), 0.0)
        bv = av * b_sh + bv
        av = av * a_sh

    @pl.when(is_prompt)
    def _():
        a_scr[...] = av.reshape(tl, c)
        b_scr[...] = bv.reshape(tl, c)

        def body(g, h):
            hstart_scr[g] = h
            last = g * SUBLANES + (SUBLANES - 1)
            return a_scr[pl.ds(last, 1), :] * h + b_scr[pl.ds(last, 1), :]

        hprev_scr[...] = lax.fori_loop(0, groups, body, hprev_scr[...])

    @pl.when(jnp.logical_not(is_prompt))
    def _():
        hstart_scr[...] = h0_ref[0]

    h3 = av * hstart_scr[...] + bv
    hr = h3.reshape(tl, c)
    hr_ref[...] = hr
    y = yr_ref[...]
    gelu = 0.5 * y * (1.0 + jnp.tanh(np.sqrt(2.0 / np.pi).astype(np.float32) * (y + 0.044715 * (y * y * y))))
    hn = hr * lax.rsqrt(jnp.mean(hr * hr, axis=-1, keepdims=True) + EPS)
    mix_ref[...] = (hn * gain_ref[0] * gelu).astype(BF16)


def _rg_call(u, prev_s, h0_s, conv_w, conv_b, w_ra, b_ra, w_ri, b_ri, lam, g_rg, layer, tok):
    t = u.shape[0]
    _, conv_width, c = conv_w.shape
    _, nblk, bw, _ = w_ra.shape
    tl, groups = tok.tile, tok.groups
    vec = pl.BlockSpec((1, 1, c), lambda i: (layer, 0, 0))
    return pl.pallas_call(
        functools.partial(_rg_kernel, n_prompt_tiles=tok.n_prompt_tiles, tiles_per_seq=tok.tiles_per_seq),
        grid=(t // tl,),
        in_specs=[
            pl.BlockSpec((tl, c), lambda i: (i, 0)),
            pl.BlockSpec((tl, c), lambda i: (i, 1)),
            pl.BlockSpec((1, tl, c), lambda i: (layer, tok.sample_tile(i), 0)),
            pl.BlockSpec((1, groups, 1, c), lambda i: (layer, tok.sample_tile(i), 0, 0)),
            pl.BlockSpec((1, conv_width, c), lambda i: (layer, 0, 0)),
            vec,
            pl.BlockSpec((1, nblk, bw, bw), lambda i: (layer, 0, 0, 0)),
            vec,
            pl.BlockSpec((1, nblk, bw, bw), lambda i: (layer, 0, 0, 0)),
            vec,
            vec,
            vec,
        ],
        out_specs=[
            pl.BlockSpec((tl, c), lambda i: (i, 0)),
            pl.BlockSpec((tl, c), lambda i: (i, 0)),
        ],
        out_shape=[
            jax.ShapeDtypeStruct((t, c), F32),
            jax.ShapeDtypeStruct((t, c), BF16),
        ],
        scratch_shapes=[
            pltpu.VMEM((SUBLANES, c), F32),
            pltpu.VMEM((1, c), F32),
            pltpu.VMEM((groups, 1, c), F32),
            pltpu.VMEM((tl, c), F32),
            pltpu.VMEM((tl, c), F32),
        ],
        compiler_params=_cparams("arbitrary"),
        name="rg_lru",
    )(u, u, prev_s, h0_s, conv_w, conv_b, w_ra, b_ra, w_ri, b_ri, lam, g_rg)


def _ml_chunk(nseq, heads, dk, dv, q_ref, k_ref, v_ref, o_ref, gc, gr, mprev, gain_ref, mix_ref,
              get_c, get_n, put_state):
    tc = q_ref.shape[0]
    ls = tc // nseq
    shift = int(np.log2(ls))
    t_idx = lax.broadcasted_iota(jnp.int32, (tc, tc), 0)
    s_idx = lax.broadcasted_iota(jnp.int32, (tc, tc), 1)
    mask = (lax.shift_right_logical(t_idx, shift) == lax.shift_right_logical(s_idx, shift)) & (s_idx <= t_idx)
    m_cum = mask.astype(F32)
    bcol = jnp.dot(m_cum, gc, precision=HI, preferred_element_type=F32)
    brow = lax.dot_general(gr, m_cum, (((1,), (1,)), ((), ())), precision=HI,
                           preferred_element_type=F32)
    lane = lax.broadcasted_iota(jnp.int32, (tc, LANES), 1)
    scale = np.float32(dk ** -0.5)

    def seg_last(col):
        if nseq == 1:
            return col[tc - 1:tc, :].reshape(1, 1, 1)
        return col.reshape(nseq, ls, 1)[:, ls - 1:ls, :]

    def seg_bcast(val):
        return jnp.broadcast_to(val, (nseq, ls, 1)).reshape(tc, 1)

    m_out = jnp.zeros((tc, LANES), F32)
    for h in range(heads):
        bc = bcol[:, heads + h:heads + h + 1]
        br = brow[heads + h:heads + h + 1, :]
        igr = gr[h:h + 1, :]
        igc = gc[:, h:h + 1]
        mp = mprev[:, h:h + 1]
        dmat = jnp.where(mask, bc - br + igr, -jnp.inf)
        m_inter = bc + mp
        m_t = jnp.maximum(m_inter, jnp.max(dmat, axis=-1, keepdims=True))
        p = jnp.exp(dmat - m_t)
        qh = q_ref[:, h * dk:(h + 1) * dk] * scale
        kh = k_ref[:, h * dk:(h + 1) * dk]
        vh = v_ref[:, h * dv:(h + 1) * dv]
        qb = qh.astype(BF16)
        kb = kh.astype(BF16)
        s = lax.dot_general(qb, kb, (((1,), (1,)), ((), ())), preferred_element_type=F32) * p
        w_inter = jnp.exp(m_inter - m_t)
        c_old = get_c(h)
        n_old = get_n(h)
        q3 = qh.reshape(nseq, ls, dk)
        inter = lax.dot_general(q3.astype(BF16), c_old.astype(BF16), (((2,), (1,)), ((0,), (0,))),
                                preferred_element_type=F32).reshape(tc, dv)
        qn = jnp.sum(q3 * n_old, axis=-1, keepdims=True).reshape(tc, 1)
        num = jnp.dot(s.astype(BF16), vh.astype(BF16), preferred_element_type=F32) + w_inter * inter
        den = jnp.sum(s, axis=-1, keepdims=True) + w_inter * qn
        hh = num / jnp.maximum(jnp.abs(den), jnp.exp(-m_t))
        hn = hh * lax.rsqrt(jnp.mean(hh * hh, axis=-1, keepdims=True) + EPS)
        og = jax.nn.sigmoid(o_ref[:, h * dv:(h + 1) * dv])
        mix_ref[:, h * dv:(h + 1) * dv] = (hn * gain_ref[0, :, h * dv:(h + 1) * dv] * og).astype(BF16)

        m_new = seg_last(m_t)
        b_last = seg_last(bc)
        wl = jnp.exp(seg_bcast(b_last) - bc + igc - seg_bcast(m_new))
        decay = jnp.exp(b_last + seg_last(mp) - m_new)
        wv3 = (wl * vh).reshape(nseq, ls, dv).astype(BF16)
        k3 = kh.reshape(nseq, ls, dk).astype(BF16)
        upd = lax.dot_general(k3, wv3, (((1,), (1,)), ((0,), (0,))), preferred_element_type=F32)
        c_new = decay * c_old + upd
        n_new = decay * n_old + jnp.sum((wl * kh).reshape(nseq, ls, dk), axis=1, keepdims=True)
        put_state(h, c_new, n_new)
        m_out = jnp.where(lane == h, seg_bcast(m_new), m_out)
    return m_out


def _ml_kernel(q_ref, k_ref, v_ref, o_ref, gc_ref, gr_ref, cs_ref, ns_ref, ms_ref, gain_ref, cs_alias,
               mix_ref, cp_out, np_out, mp_out, cs_out, ns_out, ms_out, c_scr, n_scr, m_scr,
               *, n_prompt_chunks, chunks_per_seq, heads, dk, dv, nseq_sample):
    del cs_alias
    i = pl.program_id(0)
    tc = q_ref.shape[0]
    is_prompt = i < n_prompt_chunks

    @pl.when(is_prompt & (i % chunks_per_seq == 0))
    def _():
        c_scr[...] = jnp.zeros_like(c_scr)
        n_scr[...] = jnp.zeros_like(n_scr)
        m_scr[...] = jnp.zeros_like(m_scr)

    @pl.when(is_prompt)
    def _():
        def put(h, c_new, n_new):
            c_scr[h] = c_new[0]
            n_scr[h] = n_new[0]

        mprev = jnp.broadcast_to(m_scr[...], (tc, LANES))
        m_out = _ml_chunk(1, heads, dk, dv, q_ref, k_ref, v_ref, o_ref, gc_ref[...], gr_ref[...], mprev,
                          gain_ref, mix_ref, lambda h: c_scr[h][None], lambda h: n_scr[h][None], put)
        m_scr[...] = m_out[:1, :]
        cp_out[0] = c_scr[...]
        np_out[0] = n_scr[...]
        mp_out[0] = m_out[:1, :]

    @pl.when(jnp.logical_not(is_prompt))
    def _():
        def put(h, c_new, n_new):
            cs_out[0, :, h] = c_new
            ns_out[:, h] = n_new

        m_out = _ml_chunk(nseq_sample, heads, dk, dv, q_ref, k_ref, v_ref, o_ref, gc_ref[...], gr_ref[...],
                          ms_ref[0], gain_ref, mix_ref, lambda h: cs_ref[0, :, h], lambda h: ns_ref[0, :, h], put)
        ms_out[...] = m_out


def _ml_call(u, gcol, grow, c_s, n_s, m_s_col, g_ml, cs_stack, layer, t_prompt, seq, dec_seq,
             q_off, k_off, v_off, o_off, tc):
    t = u.shape[0]
    depth, bs, heads, dk, dv = c_s.shape
    bp = t_prompt // seq
    hdk, hdv = heads * dk, heads * dv
    n_p = t_prompt // tc
    cps = seq // tc
    nseq = tc // dec_seq
    samp = lambda i: jnp.maximum(i - n_p, 0)
    pseq = lambda i: jnp.minimum(i // cps, bp - 1)
    t_s = t - t_prompt
    in_specs = [
        pl.BlockSpec((tc, hdk), lambda i: (i, q_off // hdk)),
        pl.BlockSpec((tc, hdk), lambda i: (i, k_off // hdk)),
        pl.BlockSpec((tc, hdv), lambda i: (i, v_off // hdv)),
        pl.BlockSpec((tc, hdv), lambda i: (i, o_off // hdv)),
        pl.BlockSpec((tc, LANES), lambda i: (i, 0)),
        pl.BlockSpec((2 * heads, tc), lambda i: (0, i)),
        pl.BlockSpec((1, nseq, heads, dk, dv), lambda i: (layer, samp(i), 0, 0, 0)),
        pl.BlockSpec((1, nseq, heads, 1, dk), lambda i: (layer, samp(i), 0, 0, 0)),
        pl.BlockSpec((1, tc, LANES), lambda i: (layer, samp(i), 0)),
        pl.BlockSpec((1, 1, hdv), lambda i: (layer, 0, 0)),
        pl.BlockSpec(memory_space=pl.ANY),
    ]
    args = [u, u, u, u, gcol, grow, c_s, n_s, m_s_col, g_ml, cs_stack]
    return pl.pallas_call(
        functools.partial(_ml_kernel, n_prompt_chunks=n_p, chunks_per_seq=cps, heads=heads, dk=dk, dv=dv,
                          nseq_sample=nseq),
        grid=(t // tc,),
        in_specs=in_specs,
        out_specs=[
            pl.BlockSpec((tc, hdv), lambda i: (i, 0)),
            pl.BlockSpec((1, heads, dk, dv), lambda i: (pseq(i), 0, 0, 0)),
            pl.BlockSpec((1, heads, 1, dk), lambda i: (pseq(i), 0, 0, 0)),
            pl.BlockSpec((1, 1, LANES), lambda i: (pseq(i), 0, 0)),
            pl.BlockSpec((1, nseq, heads, dk, dv), lambda i: (layer, samp(i), 0, 0, 0)),
            pl.BlockSpec((nseq, heads, 1, dk), lambda i: (samp(i), 0, 0, 0)),
            pl.BlockSpec((tc, LANES), lambda i: (samp(i), 0)),
        ],
        out_shape=[
            jax.ShapeDtypeStruct((t, hdv), BF16),
            jax.ShapeDtypeStruct((bp, heads, dk, dv), F32),
            jax.ShapeDtypeStruct((bp, heads, 1, dk), F32),
            jax.ShapeDtypeStruct((bp, 1, LANES), F32),
            jax.ShapeDtypeStruct((depth, bs, heads, dk, dv), F32),
            jax.ShapeDtypeStruct((bs, heads, 1, dk), F32),
            jax.ShapeDtypeStruct((t_s, LANES), F32),
        ],
        scratch_shapes=[
            pltpu.VMEM((heads, dk, dv), F32),
            pltpu.VMEM((heads, 1, dk), F32),
            pltpu.VMEM((1, LANES), F32),
        ],
        input_output_aliases={len(args) - 1: 4},
        compiler_params=_cparams("arbitrary"),
        name="mlstm",
    )(*args)


def _out_kernel(x_ref, mr_ref, mm_ref, gtp_ref, gts_ref, shp_ref, shs_ref, scp_ref, scs_ref, wo_hbm, wr_ref, br_ref,
                xn_ref, h2_ref, eid_ref, wt_ref, wo_scr, stage, sem, *, layer, n_groups, per_group, n_prompt_tiles):
    i = pl.program_id(0)
    tm, d = x_ref.shape
    c = mr_ref.shape[1]
    rows = stage.shape[0]

    @pl.when(i == 0)
    def _():
        for part in range(wo_scr.shape[0] // rows):
            cp = pltpu.make_async_copy(wo_hbm.at[layer, pl.ds(part * rows, rows)], stage, sem)
            cp.start()
            cp.wait()
            wo_scr[pl.ds(part * rows, rows), :] = stage[...].astype(BF16)

    is_prompt = i < n_prompt_tiles
    acc = jnp.dot(mr_ref[...], wo_scr[:c, :], preferred_element_type=F32)
    acc = acc + jnp.dot(mm_ref[...], wo_scr[c:, :], preferred_element_type=F32)
    xn3 = _rows3(x_ref[...]) + _pick_mod(is_prompt, gtp_ref, gts_ref) * _rows3(acc)
    rs = lax.rsqrt(jnp.mean(xn3 * xn3, axis=-1, keepdims=True) + EPS)
    h2 = (xn3 * rs * (1.0 + _pick_mod(is_prompt, scp_ref, scs_ref))
          + _pick_mod(is_prompt, shp_ref, shs_ref)).reshape(tm, d)
    xn_ref[...] = xn3.reshape(tm, d)
    h2_ref[...] = h2

    logits = jnp.dot(h2.astype(BF16), wr_ref[0], preferred_element_type=F32) + br_ref[0]
    lane = lax.broadcasted_iota(jnp.int32, logits.shape, 1)
    lane_f = lane.astype(F32)
    big = np.float32(LANES)

    def first_max(vals):
        vmax = jnp.max(vals, axis=-1, keepdims=True)
        idx = jnp.min(jnp.where(vals == vmax, lane_f, big), axis=-1, keepdims=True)
        return vmax, idx.astype(jnp.int32)

    is_grp = lane < n_groups
    gl = jnp.where(is_grp, logits, -jnp.inf)
    gmax, g_sel = first_max(gl)
    p_g = 1.0 / jnp.sum(jnp.exp(gl - gmax), axis=-1, keepdims=True)
    lo = n_groups + g_sel * per_group
    el = jnp.where((lane >= lo) & (lane < lo + per_group), logits, -jnp.inf)
    v1, i1 = first_max(el)
    v2, i2 = first_max(jnp.where(lane == i1, -jnp.inf, el))
    e21 = jnp.exp(v2 - v1)
    w1 = 1.0 / (1.0 + e21)
    w2 = e21 * w1
    eid_ref[...] = jnp.where(lane == 0, i1 - n_groups, jnp.where(lane == 1, i2 - n_groups, 0))
    wt_ref[...] = jnp.where(lane == 0, w1 * p_g, jnp.where(lane == 1, w2 * p_g, 0.0))


def _out_call(x, mix_rg, mix_ml, mod_p, mod_s, w_out, w_router, b_router, layer, n_groups, per_group, tok):
    t, d = x.shape
    c = mix_rg.shape[1]
    cm = mix_ml.shape[1]
    tm = tok.tile
    stage_rows = min(256, c + cm)
    return pl.pallas_call(
        functools.partial(_out_kernel, layer=layer, n_groups=n_groups, per_group=per_group,
                          n_prompt_tiles=tok.n_prompt_tiles),
        grid=(t // tm,),
        in_specs=[
            pl.BlockSpec((tm, d), lambda i: (i, 0)),
            pl.BlockSpec((tm, c), lambda i: (i, 0)),
            pl.BlockSpec((tm, cm), lambda i: (i, 0)),
            *tok.mod_specs(layer, GT_A, d),
            *tok.mod_specs(layer, SH_F, d),
            *tok.mod_specs(layer, SC_F, d),
            pl.BlockSpec(memory_space=pl.ANY),
            pl.BlockSpec((1, d, LANES), lambda i: (layer, 0, 0)),
            pl.BlockSpec((1, 1, LANES), lambda i: (layer, 0, 0)),
        ],
        out_specs=[
            pl.BlockSpec((tm, d), lambda i: (i, 0)),
            pl.BlockSpec((tm, d), lambda i: (i, 0)),
            pl.BlockSpec((tm, LANES), lambda i: (i, 0)),
            pl.BlockSpec((tm, LANES), lambda i: (i, 0)),
        ],
        out_shape=[
            jax.ShapeDtypeStruct((t, d), F32),
            jax.ShapeDtypeStruct((t, d), F32),
            jax.ShapeDtypeStruct((t, LANES), jnp.int32),
            jax.ShapeDtypeStruct((t, LANES), F32),
        ],
        scratch_shapes=[
            pltpu.VMEM((c + cm, d), BF16),
            pltpu.VMEM((stage_rows, d), F32),
            pltpu.SemaphoreType.DMA(()),
        ],
        compiler_params=_cparams("arbitrary"),
        name="out_proj_router",
    )(x, mix_rg, mix_ml, mod_p, mod_s, mod_p, mod_s, mod_p, mod_s, w_out, w_router, b_router)


def _gather_rows(idx_ref, base, n_rows, src_hbm, dst, sem, stride=1, offset=0):
    def body(r, carry):
        tok = idx_ref[base + r * stride + offset]
        pltpu.make_async_copy(src_hbm.at[pl.ds(tok, 1)], dst.at[pl.ds(r, 1)], sem).start()
        return carry

    lax.fori_loop(0, n_rows, body, 0, unroll=GATHER_UNROLL)


def _moe_kernel(te_ref, nu_ref, src_ref, h2_hbm, w1_ref, w3_ref, w2_ref, y_ref, xbuf, w1_scr, w3_scr, w2_scr, sem):
    j = pl.program_id(0)
    tm = xbuf.shape[1]
    n_used = nu_ref[0]

    def issue(tile, slot):
        _gather_rows(src_ref, tile * tm, tm, h2_hbm, xbuf.at[slot], sem.at[slot])

    @pl.when(j == 0)
    def _():
        issue(0, 0)

    @pl.when(j + 1 < n_used)
    def _():
        issue(j + 1, (j + 1) % 2)

    @pl.when((j == 0) | (te_ref[j] != te_ref[jnp.maximum(j - 1, 0)]))
    def _():
        w1_scr[...] = w1_ref[0, 0].astype(BF16)
        w3_scr[...] = w3_ref[0, 0].astype(BF16)
        w2_scr[...] = w2_ref[0, 0].astype(BF16)

    @pl.when(j < n_used)
    def _():
        slot = j % 2
        pltpu.make_async_copy(h2_hbm.at[pl.ds(0, tm)], xbuf.at[slot], sem.at[slot]).wait()
        xb = xbuf[slot].astype(BF16)
        a = jnp.dot(xb, w1_scr[...], preferred_element_type=F32)
        b = jnp.dot(xb, w3_scr[...], preferred_element_type=F32)
        hid = (a * jax.nn.sigmoid(a)) * b
        y_ref[...] = jnp.dot(hid.astype(BF16), w2_scr[...], preferred_element_type=F32)

    @pl.when(j >= n_used)
    def _():
        y_ref[...] = jnp.zeros_like(y_ref)


def _moe_call(tile_exp, n_used, src_tok, h2, w1, w3, w2, layer, tm):
    t, d = h2.shape
    _, ne, _, f = w1.shape
    n_tiles = tile_exp.shape[0]
    grid_spec = pltpu.PrefetchScalarGridSpec(
        num_scalar_prefetch=3,
        grid=(n_tiles,),
        in_specs=[
            pl.BlockSpec(memory_space=pl.ANY),
            pl.BlockSpec((1, 1, d, f), lambda j, te, nu, src: (layer, te[j], 0, 0)),
            pl.BlockSpec((1, 1, d, f), lambda j, te, nu, src: (layer, te[j], 0, 0)),
            pl.BlockSpec((1, 1, f, d), lambda j, te, nu, src: (layer, te[j], 0, 0)),
        ],
        out_specs=pl.BlockSpec((tm, d), lambda j, te, nu, src: (j, 0)),
        scratch_shapes=[
            pltpu.VMEM((2, tm, d), F32),
            pltpu.VMEM((d, f), BF16),
            pltpu.VMEM((d, f), BF16),
            pltpu.VMEM((f, d), BF16),
            pltpu.SemaphoreType.DMA((2,)),
        ],
    )
    return pl.pallas_call(
        _moe_kernel,
        grid_spec=grid_spec,
        out_shape=jax.ShapeDtypeStruct((n_tiles * tm, d), F32),
        compiler_params=_cparams("arbitrary"),
        name="experts",
    )(tile_exp, n_used, src_tok, h2, w1, w3, w2)


def _comb_kernel(dest_ref, xn_ref, wt_ref, gtp_ref, gts_ref, *rest, final, heads, n_prompt_tiles):
    if final:
        gf_ref, y_hbm, yp_ref, ys_ref, ybuf, sem = rest
    else:
        shp_ref, shs_ref, scp_ref, scs_ref, wg_ref, bg_ref, y_hbm, xo_ref, h_ref, g_ref, ybuf, sem = rest
    i = pl.program_id(0)
    n = pl.num_programs(0)
    tm, d = xn_ref.shape
    is_prompt = i < n_prompt_tiles

    def issue(tile, slot):
        for k in range(2):
            _gather_rows(dest_ref, tile * tm * 2, tm, y_hbm, ybuf.at[slot, k], sem.at[slot], stride=2, offset=k)

    @pl.when(i == 0)
    def _():
        issue(0, 0)

    @pl.when(i + 1 < n)
    def _():
        issue(i + 1, (i + 1) % 2)

    slot = i % 2
    for k in range(2):
        pltpu.make_async_copy(y_hbm.at[pl.ds(0, tm)], ybuf.at[slot, k], sem.at[slot]).wait()
    wt = wt_ref[...]
    y = wt[:, 0:1] * ybuf[slot, 0] + wt[:, 1:2] * ybuf[slot, 1]
    xo3 = _rows3(xn_ref[...]) + _pick_mod(is_prompt, gtp_ref, gts_ref) * _rows3(y)
    if final:
        xo = xo3.reshape(tm, d)
        yo = xo * lax.rsqrt(jnp.mean(xo * xo, axis=-1, keepdims=True) + EPS) * gf_ref[...]

        @pl.when(is_prompt)
        def _():
            yp_ref[...] = yo

        @pl.when(jnp.logical_not(is_prompt))
        def _():
            ys_ref[...] = yo
    else:
        xo_ref[...] = xo3.reshape(tm, d)
        _norm_gates(xo3, _pick_mod(is_prompt, scp_ref, scs_ref), _pick_mod(is_prompt, shp_ref, shs_ref),
                    wg_ref, bg_ref, h_ref, g_ref, heads)


def _comb_call(dest, xn, wts, mod_p, mod_s, w_gate, b_gate, g_final, y, layer, heads, tok, final):
    t, d = xn.shape
    tm = tok.tile
    full = lambda i, dst: (0, 0)
    in_specs = [
        pl.BlockSpec((tm, d), lambda i, dst: (i, 0)),
        pl.BlockSpec((tm, LANES), lambda i, dst: (i, 0)),
        *tok.mod_specs(layer, GT_F, d),
    ]
    args = [xn, wts, mod_p, mod_s]
    if final:
        n_p = tok.n_prompt_tiles
        in_specs.append(pl.BlockSpec((1, d), full))
        args.append(g_final)
        out_specs = [pl.BlockSpec((tm, d), lambda i, dst: (jnp.minimum(i, n_p - 1), 0)),
                     pl.BlockSpec((tm, d), lambda i, dst: (tok.sample_tile(i), 0))]
        out_shape = [jax.ShapeDtypeStruct((n_p * tm, d), F32), jax.ShapeDtypeStruct((t - n_p * tm, d), F32)]
    else:
        out_specs = [pl.BlockSpec((tm, d), lambda i, dst: (i, 0))]
        out_shape = [jax.ShapeDtypeStruct((t, d), F32)]
        nxt = layer + 1
        in_specs += [
            *tok.mod_specs(nxt, SH_A, d),
            *tok.mod_specs(nxt, SC_A, d),
            pl.BlockSpec((1, d, LANES), lambda i, dst: (nxt, 0, 0)),
            pl.BlockSpec((1, 1, LANES), lambda i, dst: (nxt, 0, 0)),
        ]
        args += [mod_p, mod_s, mod_p, mod_s, w_gate, b_gate]
        out_specs += [pl.BlockSpec((tm, d), lambda i, dst: (i, 0)), pl.BlockSpec((tm, LANES), lambda i, dst: (i, 0))]
        out_shape += [jax.ShapeDtypeStruct((t, d), BF16), jax.ShapeDtypeStruct((t, LANES), F32)]
    in_specs.append(pl.BlockSpec(memory_space=pl.ANY))
    args.append(y)
    grid_spec = pltpu.PrefetchScalarGridSpec(
        num_scalar_prefetch=1,
        grid=(t // tm,),
        in_specs=in_specs,
        out_specs=out_specs,
        scratch_shapes=[pltpu.VMEM((2, 2, tm, d), F32), pltpu.SemaphoreType.DMA((2,))],
    )
    return pl.pallas_call(
        functools.partial(_comb_kernel, final=final, heads=heads, n_prompt_tiles=tok.n_prompt_tiles),
        grid_spec=grid_spec,
        out_shape=out_shape,
        compiler_params=_cparams("arbitrary"),
        name="combine",
    )(dest, *args)


def _route_plan(eid, n_experts, tm):
    t = eid.shape[0]
    e_flat = eid[:, :2].reshape(-1)
    n_pairs = 2 * t
    n_tiles = -(-n_pairs // tm) + n_experts
    onehot = (e_flat[:, None] == jnp.arange(n_experts, dtype=jnp.int32)[None, :]).astype(jnp.int32)
    csum = jnp.cumsum(onehot, axis=0)
    rank = jnp.sum(csum * onehot, axis=1) - 1
    counts = csum[-1]
    padded = ((counts + tm - 1) // tm) * tm
    ends = jnp.cumsum(padded)
    starts = ends - padded
    dest = (jnp.sum(starts[None, :] * onehot, axis=1) + rank).astype(jnp.int32)
    src_tok = jnp.zeros((n_tiles * tm,), jnp.int32).at[dest].set(jnp.arange(n_pairs, dtype=jnp.int32) // 2)
    n_used = (ends[-1] // tm).astype(jnp.int32)
    tile_id = jnp.minimum(jnp.arange(n_tiles, dtype=jnp.int32), n_used - 1)
    tile_exp = jnp.sum((ends[None, :] <= (tile_id * tm)[:, None]).astype(jnp.int32), axis=1)
    tile_exp = jnp.minimum(tile_exp, n_experts - 1).astype(jnp.int32)
    return tile_exp, n_used.reshape(1), src_tok, dest


def _pick_tile(total, pref):
    tile = min(pref, total)
    while total % tile:
        tile //= 2
    return tile


def kernel(x_prompt, x_sample, c_prompt, c_sample, state_rg_conv, state_rg_h, state_mlstm_C, state_mlstm_n,
           state_mlstm_m, w_mod, b_mod, w_in, conv_w, conv_b, w_ra, b_ra, w_ri, b_ri, lam, g_rg, b_ig, b_fg,
           g_ml, w_out, w_grp, b_grp, w_er, b_er, w1, w3, w2, g_final):
    bp, seq, d = x_prompt.shape
    bs, dec_seq, _ = x_sample.shape
    depth = w_mod.shape[0]
    d_rg = conv_w.shape[2]
    conv_width = conv_w.shape[1]
    _, _, heads, dk, dv = state_mlstm_C.shape
    d_ml = heads * dv
    n_groups, per_group = w_er.shape[1], w_er.shape[3]
    n_experts = w1.shape[1]
    t_p, t_s = bp * seq, bs * dec_seq
    t = t_p + t_s
    n_main = 2 * d_rg + 2 * heads * dk + 2 * d_ml
    q_off, k_off, v_off, o_off = 2 * d_rg, 2 * d_rg + heads * dk, 2 * d_rg + 2 * heads * dk, n_main - d_ml
    assert dec_seq == SUBLANES and seq % LANES == 0 and conv_width - 1 <= SUBLANES
    assert 2 * heads <= LANES and n_groups * (per_group + 1) <= LANES
    assert q_off % (heads * dk) == 0 and k_off % (heads * dk) == 0 and v_off % d_ml == 0 and o_off % d_ml == 0

    tok = _Tokens(t_p, t_s, seq, _pick_tile(np.gcd(seq, t_s), 256))
    tm_in = _pick_tile(np.gcd(t_p, t_s), 1024)
    tc = LANES
    assert t_s % tc == 0 and seq % tc == 0

    mc = -(-(bp + bs) // SUBLANES) * SUBLANES
    c_all = jnp.zeros((mc, d), F32).at[:bp].set(c_prompt).at[bp:bp + bs].set(c_sample)
    mod = _mod_call(c_all, w_mod, b_mod, _pick_tile(6 * d, 1024))
    mod_p = mod[:, :bp].reshape(depth, bp, 1, 6 * d)
    mod_s = mod[:, bp:bp + bs].reshape(depth, bs, 1, 6 * d)

    w_gate = jnp.zeros((depth, d, LANES), F32).at[:, :, :2 * heads].set(w_in[:, :, n_main:]).astype(BF16)
    b_gate = jnp.zeros((depth, 1, LANES), F32).at[:, 0, :heads].set(b_ig).at[:, 0, heads:2 * heads].set(b_fg)
    w_ra_b, w_ri_b = w_ra.astype(BF16), w_ri.astype(BF16)
    n_rt = n_groups * (per_group + 1)
    w_router = jnp.concatenate([w_grp, jnp.moveaxis(w_er, 1, 2).reshape(depth, d, n_groups * per_group)], axis=-1)
    w_router = jnp.zeros((depth, d, LANES), F32).at[:, :, :n_rt].set(w_router).astype(BF16)
    b_router = jnp.concatenate([b_grp, b_er.reshape(depth, -1)], axis=-1)
    b_router = jnp.zeros((depth, 1, LANES), F32).at[:, 0, :n_rt].set(b_router)
    vec = lambda p: p.reshape(depth, 1, -1)

    pad_rows = SUBLANES - (conv_width - 1)
    prev_s = jnp.pad(state_rg_conv, ((0, 0), (0, 0), (pad_rows, 0), (0, 0))).reshape(depth, bs * SUBLANES, d_rg)
    h0_s = state_rg_h.reshape(depth, bs, 1, d_rg)
    n_s = state_mlstm_n.reshape(depth, bs, heads, 1, dk)
    m_s_col = jnp.pad(jnp.repeat(state_mlstm_m, dec_seq, axis=1), ((0, 0), (0, 0), (0, LANES - heads)))

    x = jnp.concatenate([x_prompt.reshape(t_p, d), x_sample.reshape(t_s, d)], axis=0)
    h, gcol = _prenorm_call(x, mod_p, mod_s, w_gate, b_gate, heads, tok)
    outs_p = [[] for _ in range(5)]
    outs_s = [[] for _ in range(5)]
    cs_stack = jnp.zeros_like(state_mlstm_C)
    for l in range(depth):
        u = _in_call(h, w_in, l, n_main, tm_in, _pick_tile(n_main, 1024))
        hr, mix_rg = _rg_call(u, prev_s, h0_s, conv_w, vec(conv_b), w_ra_b, vec(b_ra), w_ri_b, vec(b_ri),
                              vec(lam), vec(g_rg), l, tok)
        grow = gcol[:, :2 * heads].T
        mix_ml, c_p, n_p, m_p, cs_stack, n_so, m_so = _ml_call(
            u, gcol, grow, state_mlstm_C, n_s, m_s_col, vec(g_ml), cs_stack, l, t_p, seq, dec_seq,
            q_off, k_off, v_off, o_off, tc)
        xn, h2, eid, wts = _out_call(x, mix_rg, mix_ml, mod_p, mod_s, w_out, w_router, b_router, l,
                                     n_groups, per_group, tok)
        tile_exp, n_used, src_tok, dest = _route_plan(eid, n_experts, tok.tile)
        y = _moe_call(tile_exp, n_used, src_tok, h2, w1, w3, w2, l, tok.tile)
        final = l == depth - 1
        res = _comb_call(dest, xn, wts, mod_p, mod_s, w_gate, b_gate, g_final[None], y, l, heads, tok, final)
        if final:
            y_p, y_s = res
        else:
            x, h, gcol = res

        tail = conv_width - 1
        outs_p[0].append(jnp.stack([u[(b + 1) * seq - tail:(b + 1) * seq, :d_rg] for b in range(bp)]))
        outs_p[1].append(jnp.stack([hr[(b + 1) * seq - 1] for b in range(bp)]))
        outs_p[2].append(c_p)
        outs_p[3].append(n_p.reshape(bp, heads, dk))
        outs_p[4].append(m_p[:, 0, :heads])
        outs_s[0].append(u[t_p:, :d_rg].reshape(bs, dec_seq, d_rg)[:, dec_seq - tail:])
        outs_s[1].append(hr[t_p:].reshape(bs, dec_seq, d_rg)[:, -1])
        outs_s[3].append(n_so.reshape(bs, heads, dk))
        outs_s[4].append(m_so.reshape(bs, dec_seq, LANES)[:, -1, :heads])

    stack = lambda parts: jnp.stack(parts)
    return (y_p.reshape(bp, seq, d), y_s.reshape(bs, dec_seq, d), *[stack(o) for o in outs_p],
            stack(outs_s[0]), stack(outs_s[1]), cs_stack, stack(outs_s[3]), stack(outs_s[4]))
```

```python
import functools

import numpy as np
import jax
import jax.numpy as jnp
from jax import lax
from jax.experimental import pallas as pl
from jax.experimental.pallas import tpu as pltpu

F32 = jnp.float32
BF16 = jnp.bfloat16
EPS = 1e-6
RG_C = 8.0
SUBLANES = 8
LANES = 128
VMEM_LIMIT = 56 * 1024 * 1024
HI = lax.Precision.HIGHEST
GATHER_UNROLL = 8
DMA_THREADS = 2
SH_A, SC_A, GT_A, SH_F, SC_F, GT_F = range(6)


def _cparams(*sem):
    return pltpu.CompilerParams(dimension_semantics=sem, vmem_limit_bytes=VMEM_LIMIT)


def _softplus(z):
    return jnp.maximum(z, 0.0) + jnp.log1p(jnp.exp(-jnp.abs(z)))


def _rows3(x):
    r, c = x.shape
    return x.reshape(r // SUBLANES, SUBLANES, c)


class _Tokens:
    def __init__(self, t_prompt, t_sample, seq, tile):
        self.tile = tile
        self.groups = tile // SUBLANES
        self.n_prompt_tiles = t_prompt // tile
        self.tiles_per_seq = seq // tile
        self.n_prompt_seqs = t_prompt // seq

    def prompt_seq(self, i):
        return jnp.minimum(i // self.tiles_per_seq, self.n_prompt_seqs - 1)

    def sample_tile(self, i):
        return jnp.maximum(i - self.n_prompt_tiles, 0)

    def mod_specs(self, layer, chunk, d, tile_axis=0):
        def prompt_map(*idx):
            return (layer, self.prompt_seq(idx[tile_axis]), 0, chunk)

        def sample_map(*idx):
            return (layer, self.sample_tile(idx[tile_axis]), 0, chunk)

        return [pl.BlockSpec((1, 1, 1, d), prompt_map), pl.BlockSpec((1, self.groups, 1, d), sample_map)]


def _pick_mod(is_prompt, mp_ref, ms_ref):
    return jnp.where(is_prompt, mp_ref[0], ms_ref[0])


def _mod_kernel(c_ref, w_ref, b_ref, o_ref):
    c = c_ref[...]
    s = (c * jax.nn.sigmoid(c)).astype(BF16)
    o_ref[0] = jnp.dot(s, w_ref[0].astype(BF16), preferred_element_type=F32) + b_ref[0]


def _mod_call(c_all, w_mod, b_mod, tn):
    depth, d, n = w_mod.shape
    mc = c_all.shape[0]
    return pl.pallas_call(
        _mod_kernel,
        grid=(depth, n // tn),
        in_specs=[
            pl.BlockSpec((mc, d), lambda l, j: (0, 0)),
            pl.BlockSpec((1, d, tn), lambda l, j: (l, 0, j)),
            pl.BlockSpec((1, 1, tn), lambda l, j: (l, 0, j)),
        ],
        out_specs=pl.BlockSpec((1, mc, tn), lambda l, j: (l, 0, j)),
        out_shape=jax.ShapeDtypeStruct((depth, mc, n), F32),
        compiler_params=_cparams("arbitrary", "arbitrary"),
        name="mod",
    )(c_all, w_mod, b_mod.reshape(depth, 1, n))


def _norm_gates(x3, sc, sh, wg_ref, bg_ref, h_ref, g_ref, heads):
    groups, _, d = x3.shape
    xn = x3 * lax.rsqrt(jnp.mean(x3 * x3, axis=-1, keepdims=True) + EPS)
    hb = (xn * (1.0 + sc) + sh).reshape(groups * SUBLANES, d).astype(BF16)
    h_ref[...] = hb
    g = jnp.dot(hb, wg_ref[0], preferred_element_type=F32) + bg_ref[0]
    lane = lax.broadcasted_iota(jnp.int32, g.shape, 1)
    is_forget = (lane >= heads) & (lane < 2 * heads)
    g_ref[...] = jnp.where(is_forget, -_softplus(-g), g)


def _prenorm_kernel(x_ref, shp_ref, shs_ref, scp_ref, scs_ref, wg_ref, bg_ref, h_ref, g_ref, *, heads, n_prompt_tiles):
    is_prompt = pl.program_id(0) < n_prompt_tiles
    _norm_gates(_rows3(x_ref[...]), _pick_mod(is_prompt, scp_ref, scs_ref), _pick_mod(is_prompt, shp_ref, shs_ref),
                wg_ref, bg_ref, h_ref, g_ref, heads)


def _prenorm_call(x, mod_p, mod_s, w_gate, b_gate, heads, tok):
    t, d = x.shape
    tm = tok.tile
    return pl.pallas_call(
        functools.partial(_prenorm_kernel, heads=heads, n_prompt_tiles=tok.n_prompt_tiles),
        grid=(t // tm,),
        in_specs=[
            pl.BlockSpec((tm, d), lambda i: (i, 0)),
            *tok.mod_specs(0, SH_A, d),
            *tok.mod_specs(0, SC_A, d),
            pl.BlockSpec((1, d, LANES), lambda i: (0, 0, 0)),
            pl.BlockSpec((1, 1, LANES), lambda i: (0, 0, 0)),
        ],
        out_specs=[
            pl.BlockSpec((tm, d), lambda i: (i, 0)),
            pl.BlockSpec((tm, LANES), lambda i: (i, 0)),
        ],
        out_shape=[
            jax.ShapeDtypeStruct((t, d), BF16),
            jax.ShapeDtypeStruct((t, LANES), F32),
        ],
        compiler_params=_cparams("arbitrary"),
        name="prenorm",
    )(x, mod_p, mod_s, mod_p, mod_s, w_gate, b_gate)


def _in_kernel(h_ref, w_ref, u_ref, wb_scr):
    @pl.when(pl.program_id(1) == 0)
    def _():
        wb_scr[...] = w_ref[0].astype(BF16)

    u_ref[...] = lax.dot_general(h_ref[...], wb_scr[...], (((1,), (1,)), ((), ())), preferred_element_type=F32)


def _in_call(h, w_in_t, layer, n_main, tm, tn):
    t, d = h.shape
    return pl.pallas_call(
        _in_kernel,
        grid=(n_main // tn, t // tm),
        in_specs=[
            pl.BlockSpec((tm, d), lambda j, i: (i, 0)),
            pl.BlockSpec((1, tn, d), lambda j, i: (layer, j, 0)),
        ],
        out_specs=pl.BlockSpec((tm, tn), lambda j, i: (i, j)),
        out_shape=jax.ShapeDtypeStruct((t, n_main), F32),
        scratch_shapes=[pltpu.VMEM((tn, d), BF16)],
        compiler_params=_cparams("arbitrary", "arbitrary"),
        name="in_proj",
    )(h, w_in_t)


def _rg_kernel(xr_ref, yr_ref, prev_ref, h0_ref, cw_ref, cb_ref, wra_ref, bra_ref, wri_ref, bri_ref,
               lam_ref, gain_ref, hr_ref, mix_ref, xprev_scr, hprev_scr, hstart_scr, a_scr, b_scr,
               *, n_prompt_tiles, tiles_per_seq):
    i = pl.program_id(0)
    tl, c = xr_ref.shape
    groups = tl // SUBLANES
    _, nblk, bw, _ = wra_ref.shape
    is_prompt = i < n_prompt_tiles

    @pl.when(is_prompt & (i % tiles_per_seq == 0))
    def _():
        xprev_scr[...] = jnp.zeros_like(xprev_scr)
        hprev_scr[...] = jnp.zeros_like(hprev_scr)

    x = xr_ref[...]
    prev_prompt = jnp.concatenate([xprev_scr[...], x[:tl - SUBLANES]], axis=0)
    prev = jnp.where(is_prompt, prev_prompt, prev_ref[0])
    xprev_scr[...] = x[tl - SUBLANES:]

    x3 = _rows3(x)
    p3 = _rows3(prev)
    row = lax.broadcasted_iota(jnp.int32, x3.shape, 1)
    cw = cw_ref[0]
    conv_w = cw.shape[0]
    xc = cb_ref[0] + x3 * cw[conv_w - 1:conv_w]
    for dlt in range(1, conv_w):
        shifted = jnp.where(row >= dlt, pltpu.roll(x3, dlt, 1), pltpu.roll(p3, dlt, 1))
        xc = xc + shifted * cw[conv_w - 1 - dlt:conv_w - dlt]

    xc2 = xc.reshape(tl, c)
    xcb = xc2.astype(BF16)
    r_parts, i_parts = [], []
    for nb in range(nblk):
        blk = xcb[:, nb * bw:(nb + 1) * bw]
        r_parts.append(jnp.dot(blk, wra_ref[0, nb], preferred_element_type=F32))
        i_parts.append(jnp.dot(blk, wri_ref[0, nb], preferred_element_type=F32))
    r = jax.nn.sigmoid(jnp.concatenate(r_parts, axis=-1) + bra_ref[0])
    ig = jax.nn.sigmoid(jnp.concatenate(i_parts, axis=-1) + bri_ref[0])
    log_a = (-RG_C) * r * _softplus(-lam_ref[0])
    a = jnp.exp(log_a)
    th = jnp.tanh(log_a)
    mult = jnp.sqrt(-2.0 * th / (1.0 - th))
    bx = mult * (ig * xc2)

    av = _rows3(a)
    bv = _rows3(bx)
    for s in (1, 2, 4):
        a_sh = jnp.where(row >= s, pltpu.roll(av, s, 1), 1.0)
        b_sh = jnp.where(row >= s, pltpu.roll(bv, s, 1), 0.0)
        bv = av * b_sh + bv
        av = av * a_sh

    @pl.when(is_prompt)
    def _():
        a_scr[...] = av.reshape(tl, c)
        b_scr[...] = bv.reshape(tl, c)

        def body(g, h):
            hstart_scr[g] = h
            last = g * SUBLANES + (SUBLANES - 1)
            return a_scr[pl.ds(last, 1), :] * h + b_scr[pl.ds(last, 1), :]

        hprev_scr[...] = lax.fori_loop(0, groups, body, hprev_scr[...])

    @pl.when(jnp.logical_not(is_prompt))
    def _():
        hstart_scr[...] = h0_ref[0]

    h3 = av * hstart_scr[...] + bv
    hr = h3.reshape(tl, c)
    hr_ref[...] = hr
    y = yr_ref[...]
    gelu = 0.5 * y * (1.0 + jnp.tanh(np.sqrt(2.0 / np.pi).astype(np.float32) * (y + 0.044715 * (y * y * y))))
    hn = hr * lax.rsqrt(jnp.mean(hr * hr, axis=-1, keepdims=True) + EPS)
    mix_ref[...] = (hn * gain_ref[0] * gelu).astype(BF16)


def _rg_call(u, prev_s, h0_s, conv_w, conv_b, w_ra, b_ra, w_ri, b_ri, lam, g_rg, layer, tok):
    t = u.shape[0]
    _, conv_width, c = conv_w.shape
    _, nblk, bw, _ = w_ra.shape
    tl, groups = tok.tile, tok.groups
    vec = pl.BlockSpec((1, 1, c), lambda i: (layer, 0, 0))
    return pl.pallas_call(
        functools.partial(_rg_kernel, n_prompt_tiles=tok.n_prompt_tiles, tiles_per_seq=tok.tiles_per_seq),
        grid=(t // tl,),
        in_specs=[
            pl.BlockSpec((tl, c), lambda i: (i, 0)),
            pl.BlockSpec((tl, c), lambda i: (i, 1)),
            pl.BlockSpec((1, tl, c), lambda i: (layer, tok.sample_tile(i), 0)),
            pl.BlockSpec((1, groups, 1, c), lambda i: (layer, tok.sample_tile(i), 0, 0)),
            pl.BlockSpec((1, conv_width, c), lambda i: (layer, 0, 0)),
            vec,
            pl.BlockSpec((1, nblk, bw, bw), lambda i: (layer, 0, 0, 0)),
            vec,
            pl.BlockSpec((1, nblk, bw, bw), lambda i: (layer, 0, 0, 0)),
            vec,
            vec,
            vec,
        ],
        out_specs=[
            pl.BlockSpec((tl, c), lambda i: (i, 0)),
            pl.BlockSpec((tl, c), lambda i: (i, 0)),
        ],
        out_shape=[
            jax.ShapeDtypeStruct((t, c), F32),
            jax.ShapeDtypeStruct((t, c), BF16),
        ],
        scratch_shapes=[
            pltpu.VMEM((SUBLANES, c), F32),
            pltpu.VMEM((1, c), F32),
            pltpu.VMEM((groups, 1, c), F32),
            pltpu.VMEM((tl, c), F32),
            pltpu.VMEM((tl, c), F32),
        ],
        compiler_params=_cparams("arbitrary"),
        name="rg_lru",
    )(u, u, prev_s, h0_s, conv_w, conv_b, w_ra, b_ra, w_ri, b_ri, lam, g_rg)


def _ml_chunk(nseq, heads, dk, dv, q_ref, k_ref, v_ref, o_ref, gc, gr, mprev, gain_ref, mix_ref,
              get_c, get_n, put_state):
    tc = q_ref.shape[0]
    ls = tc // nseq
    shift = int(np.log2(ls))
    t_idx = lax.broadcasted_iota(jnp.int32, (tc, tc), 0)
    s_idx = lax.broadcasted_iota(jnp.int32, (tc, tc), 1)
    mask = (lax.shift_right_logical(t_idx, shift) == lax.shift_right_logical(s_idx, shift)) & (s_idx <= t_idx)
    m_cum = mask.astype(F32)
    bcol = jnp.dot(m_cum, gc, precision=HI, preferred_element_type=F32)
    brow = lax.dot_general(gr, m_cum, (((1,), (1,)), ((), ())), precision=HI,
                           preferred_element_type=F32)
    lane = lax.broadcasted_iota(jnp.int32, (tc, LANES), 1)
    scale = np.float32(dk ** -0.5)

    def seg_last(col):
        if nseq == 1:
            return col[tc - 1:tc, :].reshape(1, 1, 1)
        return col.reshape(nseq, ls, 1)[:, ls - 1:ls, :]

    def seg_bcast(val):
        return jnp.broadcast_to(val, (nseq, ls, 1)).reshape(tc, 1)

    m_out = jnp.zeros((tc, LANES), F32)
    for h in range(heads):
        bc = bcol[:, heads + h:heads + h + 1]
        br = brow[heads + h:heads + h + 1, :]
        igr = gr[h:h + 1, :]
        igc = gc[:, h:h + 1]
        mp = mprev[:, h:h + 1]
        dmat = jnp.where(mask, bc - br + igr, -jnp.inf)
        m_inter = bc + mp
        m_t = jnp.maximum(m_inter, jnp.max(dmat, axis=-1, keepdims=True))
        p = jnp.exp(dmat - m_t)
        qh = q_ref[:, h * dk:(h + 1) * dk] * scale
        kh = k_ref[:, h * dk:(h + 1) * dk]
        vh = v_ref[:, h * dv:(h + 1) * dv]
        qb = qh.astype(BF16)
        kb = kh.astype(BF16)
        s = lax.dot_general(qb, kb, (((1,), (1,)), ((), ())), preferred_element_type=F32) * p
        w_inter = jnp.exp(m_inter - m_t)
        c_old = get_c(h)
        n_old = get_n(h)
        q3 = qh.reshape(nseq, ls, dk)
        inter = lax.dot_general(q3.astype(BF16), c_old.astype(BF16), (((2,), (1,)), ((0,), (0,))),
                                preferred_element_type=F32).reshape(tc, dv)
        qn = jnp.sum(q3 * n_old, axis=-1, keepdims=True).reshape(tc, 1)
        num = jnp.dot(s.astype(BF16), vh.astype(BF16), preferred_element_type=F32) + w_inter * inter
        den = jnp.sum(s, axis=-1, keepdims=True) + w_inter * qn
        hh = num / jnp.maximum(jnp.abs(den), jnp.exp(-m_t))
        hn = hh * lax.rsqrt(jnp.mean(hh * hh, axis=-1, keepdims=True) + EPS)
        og = jax.nn.sigmoid(o_ref[:, h * dv:(h + 1) * dv])
        mix_ref[:, h * dv:(h + 1) * dv] = (hn * gain_ref[0, :, h * dv:(h + 1) * dv] * og).astype(BF16)

        m_new = seg_last(m_t)
        b_last = seg_last(bc)
        wl = jnp.exp(seg_bcast(b_last) - bc + igc - seg_bcast(m_new))
        decay = jnp.exp(b_last + seg_last(mp) - m_new)
        wv3 = (wl * vh).reshape(nseq, ls, dv).astype(BF16)
        k3 = kh.reshape(nseq, ls, dk).astype(BF16)
        upd = lax.dot_general(k3, wv3, (((1,), (1,)), ((0,), (0,))), preferred_element_type=F32)
        c_new = decay * c_old + upd
        n_new = decay * n_old + jnp.sum((wl * kh).reshape(nseq, ls, dk), axis=1, keepdims=True)
        put_state(h, c_new, n_new)
        m_out = jnp.where(lane == h, seg_bcast(m_new), m_out)
    return m_out


def _ml_kernel(q_ref, k_ref, v_ref, o_ref, gc_ref, gr_ref, cs_ref, ns_ref, ms_ref, gain_ref, cs_alias,
               mix_ref, cp_out, np_out, mp_out, cs_out, ns_out, ms_out, c_scr, n_scr, m_scr,
               *, n_prompt_chunks, chunks_per_seq, heads, dk, dv, nseq_sample):
    del cs_alias
    i = pl.program_id(0)
    tc = q_ref.shape[0]
    is_prompt = i < n_prompt_chunks

    @pl.when(is_prompt & (i % chunks_per_seq == 0))
    def _():
        c_scr[...] = jnp.zeros_like(c_scr)
        n_scr[...] = jnp.zeros_like(n_scr)
        m_scr[...] = jnp.zeros_like(m_scr)

    @pl.when(is_prompt)
    def _():
        def put(h, c_new, n_new):
            c_scr[h] = c_new[0]
            n_scr[h] = n_new[0]

        mprev = jnp.broadcast_to(m_scr[...], (tc, LANES))
        m_out = _ml_chunk(1, heads, dk, dv, q_ref, k_ref, v_ref, o_ref, gc_ref[...], gr_ref[...], mprev,
                          gain_ref, mix_ref, lambda h: c_scr[h][None], lambda h: n_scr[h][None], put)
        m_scr[...] = m_out[:1, :]
        cp_out[0] = c_scr[...]
        np_out[0] = n_scr[...]
        mp_out[0] = m_out[:1, :]

    @pl.when(jnp.logical_not(is_prompt))
    def _():
        def put(h, c_new, n_new):
            cs_out[0, :, h] = c_new
            ns_out[:, h] = n_new

        m_out = _ml_chunk(nseq_sample, heads, dk, dv, q_ref, k_ref, v_ref, o_ref, gc_ref[...], gr_ref[...],
                          ms_ref[0], gain_ref, mix_ref, lambda h: cs_ref[0, :, h], lambda h: ns_ref[0, :, h], put)
        ms_out[...] = m_out


def _ml_call(u, gcol, grow, c_s, n_s, m_s_col, g_ml, cs_stack, layer, t_prompt, seq, dec_seq,
             q_off, k_off, v_off, o_off, tc):
    t = u.shape[0]
    depth, bs, heads, dk, dv = c_s.shape
    bp = t_prompt // seq
    hdk, hdv = heads * dk, heads * dv
    n_p = t_prompt // tc
    cps = seq // tc
    nseq = tc // dec_seq
    samp = lambda i: jnp.maximum(i - n_p, 0)
    pseq = lambda i: jnp.minimum(i // cps, bp - 1)
    t_s = t - t_prompt
    in_specs = [
        pl.BlockSpec((tc, hdk), lambda i: (i, q_off // hdk)),
        pl.BlockSpec((tc, hdk), lambda i: (i, k_off // hdk)),
        pl.BlockSpec((tc, hdv), lambda i: (i, v_off // hdv)),
        pl.BlockSpec((tc, hdv), lambda i: (i, o_off // hdv)),
        pl.BlockSpec((tc, LANES), lambda i: (i, 0)),
        pl.BlockSpec((2 * heads, tc), lambda i: (0, i)),
        pl.BlockSpec((1, nseq, heads, dk, dv), lambda i: (layer, samp(i), 0, 0, 0)),
        pl.BlockSpec((1, nseq, heads, 1, dk), lambda i: (layer, samp(i), 0, 0, 0)),
        pl.BlockSpec((1, tc, LANES), lambda i: (layer, samp(i), 0)),
        pl.BlockSpec((1, 1, hdv), lambda i: (layer, 0, 0)),
        pl.BlockSpec(memory_space=pl.ANY),
    ]
    args = [u, u, u, u, gcol, grow, c_s, n_s, m_s_col, g_ml, cs_stack]
    return pl.pallas_call(
        functools.partial(_ml_kernel, n_prompt_chunks=n_p, chunks_per_seq=cps, heads=heads, dk=dk, dv=dv,
                          nseq_sample=nseq),
        grid=(t // tc,),
        in_specs=in_specs,
        out_specs=[
            pl.BlockSpec((tc, hdv), lambda i: (i, 0)),
            pl.BlockSpec((1, heads, dk, dv), lambda i: (pseq(i), 0, 0, 0)),
            pl.BlockSpec((1, heads, 1, dk), lambda i: (pseq(i), 0, 0, 0)),
            pl.BlockSpec((1, 1, LANES), lambda i: (pseq(i), 0, 0)),
            pl.BlockSpec((1, nseq, heads, dk, dv), lambda i: (layer, samp(i), 0, 0, 0)),
            pl.BlockSpec((nseq, heads, 1, dk), lambda i: (samp(i), 0, 0, 0)),
            pl.BlockSpec((tc, LANES), lambda i: (samp(i), 0)),
        ],
        out_shape=[
            jax.ShapeDtypeStruct((t, hdv), BF16),
            jax.ShapeDtypeStruct((bp, heads, dk, dv), F32),
            jax.ShapeDtypeStruct((bp, heads, 1, dk), F32),
            jax.ShapeDtypeStruct((bp, 1, LANES), F32),
            jax.ShapeDtypeStruct((depth, bs, heads, dk, dv), F32),
            jax.ShapeDtypeStruct((bs, heads, 1, dk), F32),
            jax.ShapeDtypeStruct((t_s, LANES), F32),
        ],
        scratch_shapes=[
            pltpu.VMEM((heads, dk, dv), F32),
            pltpu.VMEM((heads, 1, dk), F32),
            pltpu.VMEM((1, LANES), F32),
        ],
        input_output_aliases={len(args) - 1: 4},
        compiler_params=_cparams("arbitrary"),
        name="mlstm",
    )(*args)


def _out_kernel(x_ref, mr_ref, mm_ref, gtp_ref, gts_ref, shp_ref, shs_ref, scp_ref, scs_ref, wo_hbm, wr_ref, br_ref,
                xn_ref, h2_ref, eid_ref, wt_ref, wo_scr, stage, sem, *, layer, n_groups, per_group, n_prompt_tiles):
    i = pl.program_id(0)
    tm, d = x_ref.shape
    c = mr_ref.shape[1]
    rows = stage.shape[0]

    @pl.when(i == 0)
    def _():
        for part in range(wo_scr.shape[0] // rows):
            cp = pltpu.make_async_copy(wo_hbm.at[layer, pl.ds(part * rows, rows)], stage, sem)
            cp.start()
            cp.wait()
            wo_scr[pl.ds(part * rows, rows), :] = stage[...].astype(BF16)

    is_prompt = i < n_prompt_tiles
    acc = jnp.dot(mr_ref[...], wo_scr[:c, :], preferred_element_type=F32)
    acc = acc + jnp.dot(mm_ref[...], wo_scr[c:, :], preferred_element_type=F32)
    xn3 = _rows3(x_ref[...]) + _pick_mod(is_prompt, gtp_ref, gts_ref) * _rows3(acc)
    rs = lax.rsqrt(jnp.mean(xn3 * xn3, axis=-1, keepdims=True) + EPS)
    h2 = (xn3 * rs * (1.0 + _pick_mod(is_prompt, scp_ref, scs_ref))
          + _pick_mod(is_prompt, shp_ref, shs_ref)).reshape(tm, d)
    xn_ref[...] = xn3.reshape(tm, d)
    h2_ref[...] = h2

    logits = jnp.dot(h2.astype(BF16), wr_ref[0], preferred_element_type=F32) + br_ref[0]
    lane = lax.broadcasted_iota(jnp.int32, logits.shape, 1)
    lane_f = lane.astype(F32)
    big = np.float32(LANES)

    def first_max(vals):
        vmax = jnp.max(vals, axis=-1, keepdims=True)
        idx = jnp.min(jnp.where(vals == vmax, lane_f, big), axis=-1, keepdims=True)
        return vmax, idx.astype(jnp.int32)

    is_grp = lane < n_groups
    gl = jnp.where(is_grp, logits, -jnp.inf)
    gmax, g_sel = first_max(gl)
    p_g = 1.0 / jnp.sum(jnp.exp(gl - gmax), axis=-1, keepdims=True)
    lo = n_groups + g_sel * per_group
    el = jnp.where((lane >= lo) & (lane < lo + per_group), logits, -jnp.inf)
    v1, i1 = first_max(el)
    v2, i2 = first_max(jnp.where(lane == i1, -jnp.inf, el))
    e21 = jnp.exp(v2 - v1)
    w1 = 1.0 / (1.0 + e21)
    w2 = e21 * w1
    eid_ref[...] = jnp.where(lane == 0, i1 - n_groups, jnp.where(lane == 1, i2 - n_groups, 0))
    wt_ref[...] = jnp.where(lane == 0, w1 * p_g, jnp.where(lane == 1, w2 * p_g, 0.0))


def _out_call(x, mix_rg, mix_ml, mod_p, mod_s, w_out, w_router, b_router, layer, n_groups, per_group, tok):
    t, d = x.shape
    c = mix_rg.shape[1]
    cm = mix_ml.shape[1]
    tm = tok.tile
    stage_rows = min(256, c + cm)
    return pl.pallas_call(
        functools.partial(_out_kernel, layer=layer, n_groups=n_groups, per_group=per_group,
                          n_prompt_tiles=tok.n_prompt_tiles),
        grid=(t // tm,),
        in_specs=[
            pl.BlockSpec((tm, d), lambda i: (i, 0)),
            pl.BlockSpec((tm, c), lambda i: (i, 0)),
            pl.BlockSpec((tm, cm), lambda i: (i, 0)),
            *tok.mod_specs(layer, GT_A, d),
            *tok.mod_specs(layer, SH_F, d),
            *tok.mod_specs(layer, SC_F, d),
            pl.BlockSpec(memory_space=pl.ANY),
            pl.BlockSpec((1, d, LANES), lambda i: (layer, 0, 0)),
            pl.BlockSpec((1, 1, LANES), lambda i: (layer, 0, 0)),
        ],
        out_specs=[
            pl.BlockSpec((tm, d), lambda i: (i, 0)),
            pl.BlockSpec((tm, d), lambda i: (i, 0)),
            pl.BlockSpec((tm, LANES), lambda i: (i, 0)),
            pl.BlockSpec((tm, LANES), lambda i: (i, 0)),
        ],
        out_shape=[
            jax.ShapeDtypeStruct((t, d), F32),
            jax.ShapeDtypeStruct((t, d), F32),
            jax.ShapeDtypeStruct((t, LANES), jnp.int32),
            jax.ShapeDtypeStruct((t, LANES), F32),
        ],
        scratch_shapes=[
            pltpu.VMEM((c + cm, d), BF16),
            pltpu.VMEM((stage_rows, d), F32),
            pltpu.SemaphoreType.DMA(()),
        ],
        compiler_params=_cparams("arbitrary"),
        name="out_proj_router",
    )(x, mix_rg, mix_ml, mod_p, mod_s, mod_p, mod_s, mod_p, mod_s, w_out, w_router, b_router)


def _gather_rows(idx_ref, base, n_rows, src_hbm, dst, sem, stride=1, offset=0):
    def body(g, carry):
        for lane in range(GATHER_UNROLL):
            r = g * GATHER_UNROLL + lane
            tok = idx_ref[base + r * stride + offset]
            pltpu.make_async_copy(src_hbm.at[pl.ds(tok, 1)], dst.at[pl.ds(r, 1)], sem).start(
                priority=lane % DMA_THREADS)
        return carry

    lax.fori_loop(0, n_rows // GATHER_UNROLL, body, 0)


def _moe_kernel(te_ref, nx_ref, nu_ref, src_ref, h2_hbm, w1_hbm, w3_hbm, w2_hbm, y_ref,
                xbuf, st1, st3, st2, w1_scr, w3_scr, w2_scr, slot_ref, gsem, wsem, *, layer):
    j = pl.program_id(0)
    nbuf, tm, _ = xbuf.shape
    n_used = nu_ref[0]

    def weight_copies(expert, slot):
        return [pltpu.make_async_copy(w_hbm.at[layer, expert], stage.at[slot], wsem.at[slot])
                for w_hbm, stage in ((w1_hbm, st1), (w3_hbm, st3), (w2_hbm, st2))]

    def issue(tile):
        _gather_rows(src_ref, tile * tm, tm, h2_hbm, xbuf.at[tile % nbuf], gsem.at[tile % nbuf])

    @pl.when(j == 0)
    def _():
        slot_ref[0] = 1
        for cp in weight_copies(te_ref[0], 0):
            cp.start()
        for ahead in range(nbuf - 1):
            @pl.when(ahead < n_used)
            def _():
                issue(ahead)

    @pl.when(j + (nbuf - 1) < n_used)
    def _():
        issue(j + (nbuf - 1))

    @pl.when((j < n_used) & ((j == 0) | (te_ref[j] != te_ref[jnp.maximum(j - 1, 0)])))
    def _():
        slot = 1 - slot_ref[0]
        slot_ref[0] = slot
        for cp in weight_copies(te_ref[j], slot):
            cp.wait()
        w1_scr[...] = st1[slot].astype(BF16)
        w3_scr[...] = st3[slot].astype(BF16)
        w2_scr[...] = st2[slot].astype(BF16)

        @pl.when(nx_ref[j] >= 0)
        def _():
            for cp in weight_copies(nx_ref[j], 1 - slot):
                cp.start()

    @pl.when(j < n_used)
    def _():
        slot = j % nbuf
        pltpu.make_async_copy(h2_hbm.at[pl.ds(0, tm)], xbuf.at[slot], gsem.at[slot]).wait()
        xb = xbuf[slot].astype(BF16)
        a = jnp.dot(xb, w1_scr[...], preferred_element_type=F32)
        b = jnp.dot(xb, w3_scr[...], preferred_element_type=F32)
        hid = (a * jax.nn.sigmoid(a)) * b
        y_ref[...] = jnp.dot(hid.astype(BF16), w2_scr[...], preferred_element_type=F32)

    @pl.when(j >= n_used)
    def _():
        y_ref[...] = jnp.zeros_like(y_ref)


def _moe_call(tile_exp, next_exp, n_used, src_tok, h2, w1, w3, w2, layer, tm):
    t, d = h2.shape
    _, ne, _, f = w1.shape
    n_tiles = tile_exp.shape[0]
    gather_bufs = 3
    grid_spec = pltpu.PrefetchScalarGridSpec(
        num_scalar_prefetch=4,
        grid=(n_tiles,),
        in_specs=[pl.BlockSpec(memory_space=pl.ANY)] * 4,
        out_specs=pl.BlockSpec((tm, d), lambda j, *_: (j, 0)),
        scratch_shapes=[
            pltpu.VMEM((gather_bufs, tm, d), F32),
            pltpu.VMEM((2, d, f), F32),
            pltpu.VMEM((2, d, f), F32),
            pltpu.VMEM((2, f, d), F32),
            pltpu.VMEM((d, f), BF16),
            pltpu.VMEM((d, f), BF16),
            pltpu.VMEM((f, d), BF16),
            pltpu.SMEM((1,), jnp.int32),
            pltpu.SemaphoreType.DMA((gather_bufs,)),
            pltpu.SemaphoreType.DMA((2,)),
        ],
    )
    return pl.pallas_call(
        functools.partial(_moe_kernel, layer=layer),
        grid_spec=grid_spec,
        out_shape=jax.ShapeDtypeStruct((n_tiles * tm, d), F32),
        compiler_params=_cparams("arbitrary"),
        name="experts",
    )(tile_exp, next_exp, n_used, src_tok, h2, w1, w3, w2)


def _comb_kernel(dest_ref, xn_ref, wt_ref, gtp_ref, gts_ref, *rest, final, heads, n_prompt_tiles):
    if final:
        gf_ref, y_hbm, yp_ref, ys_ref, ybuf, sem = rest
    else:
        shp_ref, shs_ref, scp_ref, scs_ref, wg_ref, bg_ref, y_hbm, xo_ref, h_ref, g_ref, ybuf, sem = rest
    i = pl.program_id(0)
    n = pl.num_programs(0)
    tm, d = xn_ref.shape
    is_prompt = i < n_prompt_tiles

    def issue(tile, slot):
        for k in range(2):
            _gather_rows(dest_ref, tile * tm * 2, tm, y_hbm, ybuf.at[slot, k], sem.at[slot], stride=2, offset=k)

    @pl.when(i == 0)
    def _():
        issue(0, 0)

    @pl.when(i + 1 < n)
    def _():
        issue(i + 1, (i + 1) % 2)

    slot = i % 2
    for k in range(2):
        pltpu.make_async_copy(y_hbm.at[pl.ds(0, tm)], ybuf.at[slot, k], sem.at[slot]).wait()
    wt = wt_ref[...]
    y = wt[:, 0:1] * ybuf[slot, 0] + wt[:, 1:2] * ybuf[slot, 1]
    xo3 = _rows3(xn_ref[...]) + _pick_mod(is_prompt, gtp_ref, gts_ref) * _rows3(y)
    if final:
        xo = xo3.reshape(tm, d)
        yo = xo * lax.rsqrt(jnp.mean(xo * xo, axis=-1, keepdims=True) + EPS) * gf_ref[...]

        @pl.when(is_prompt)
        def _():
            yp_ref[...] = yo

        @pl.when(jnp.logical_not(is_prompt))
        def _():
            ys_ref[...] = yo
    else:
        xo_ref[...] = xo3.reshape(tm, d)
        _norm_gates(xo3, _pick_mod(is_prompt, scp_ref, scs_ref), _pick_mod(is_prompt, shp_ref, shs_ref),
                    wg_ref, bg_ref, h_ref, g_ref, heads)


def _comb_call(dest, xn, wts, mod_p, mod_s, w_gate, b_gate, g_final, y, layer, heads, tok, final):
    t, d = xn.shape
    tm = tok.tile
    full = lambda i, dst: (0, 0)
    in_specs = [
        pl.BlockSpec((tm, d), lambda i, dst: (i, 0)),
        pl.BlockSpec((tm, LANES), lambda i, dst: (i, 0)),
        *tok.mod_specs(layer, GT_F, d),
    ]
    args = [xn, wts, mod_p, mod_s]
    if final:
        n_p = tok.n_prompt_tiles
        in_specs.append(pl.BlockSpec((1, d), full))
        args.append(g_final)
        out_specs = [pl.BlockSpec((tm, d), lambda i, dst: (jnp.minimum(i, n_p - 1), 0)),
                     pl.BlockSpec((tm, d), lambda i, dst: (tok.sample_tile(i), 0))]
        out_shape = [jax.ShapeDtypeStruct((n_p * tm, d), F32), jax.ShapeDtypeStruct((t - n_p * tm, d), F32)]
    else:
        out_specs = [pl.BlockSpec((tm, d), lambda i, dst: (i, 0))]
        out_shape = [jax.ShapeDtypeStruct((t, d), F32)]
        nxt = layer + 1
        in_specs += [
            *tok.mod_specs(nxt, SH_A, d),
            *tok.mod_specs(nxt, SC_A, d),
            pl.BlockSpec((1, d, LANES), lambda i, dst: (nxt, 0, 0)),
            pl.BlockSpec((1, 1, LANES), lambda i, dst: (nxt, 0, 0)),
        ]
        args += [mod_p, mod_s, mod_p, mod_s, w_gate, b_gate]
        out_specs += [pl.BlockSpec((tm, d), lambda i, dst: (i, 0)), pl.BlockSpec((tm, LANES), lambda i, dst: (i, 0))]
        out_shape += [jax.ShapeDtypeStruct((t, d), BF16), jax.ShapeDtypeStruct((t, LANES), F32)]
    in_specs.append(pl.BlockSpec(memory_space=pl.ANY))
    args.append(y)
    grid_spec = pltpu.PrefetchScalarGridSpec(
        num_scalar_prefetch=1,
        grid=(t // tm,),
        in_specs=in_specs,
        out_specs=out_specs,
        scratch_shapes=[pltpu.VMEM((2, 2, tm, d), F32), pltpu.SemaphoreType.DMA((2,))],
    )
    return pl.pallas_call(
        functools.partial(_comb_kernel, final=final, heads=heads, n_prompt_tiles=tok.n_prompt_tiles),
        grid_spec=grid_spec,
        out_shape=out_shape,
        compiler_params=_cparams("arbitrary"),
        name="combine",
    )(dest, *args)


def _route_plan(eid, n_experts, tm):
    t = eid.shape[0]
    e_flat = eid[:, :2].reshape(-1)
    n_pairs = 2 * t
    n_tiles = -(-n_pairs // tm) + n_experts
    onehot = (e_flat[:, None] == jnp.arange(n_experts, dtype=jnp.int32)[None, :]).astype(jnp.int32)
    csum = jnp.cumsum(onehot, axis=0)
    rank = jnp.sum(csum * onehot, axis=1) - 1
    counts = csum[-1]
    padded = ((counts + tm - 1) // tm) * tm
    ends = jnp.cumsum(padded)
    starts = ends - padded
    dest = (jnp.sum(starts[None, :] * onehot, axis=1) + rank).astype(jnp.int32)
    src_tok = jnp.zeros((n_tiles * tm,), jnp.int32).at[dest].set(jnp.arange(n_pairs, dtype=jnp.int32) // 2)
    n_used = (ends[-1] // tm).astype(jnp.int32)
    tile_id = jnp.minimum(jnp.arange(n_tiles, dtype=jnp.int32), n_used - 1)
    tile_exp = jnp.sum((ends[None, :] <= (tile_id * tm)[:, None]).astype(jnp.int32), axis=1)
    tile_exp = jnp.minimum(tile_exp, n_experts - 1).astype(jnp.int32)
    ids = jnp.arange(n_experts, dtype=jnp.int32)
    later = (ids[None, :] > ids[:, None]) & (padded[None, :] > 0)
    following = jnp.min(jnp.where(later, ids[None, :], n_experts), axis=1)
    following = jnp.where(following == n_experts, -1, following).astype(jnp.int32)
    return tile_exp, following[tile_exp], n_used.reshape(1), src_tok, dest


def _pick_tile(total, pref):
    tile = min(pref, total)
    while total % tile:
        tile //= 2
    return tile


def kernel(x_prompt, x_sample, c_prompt, c_sample, state_rg_conv, state_rg_h, state_mlstm_C, state_mlstm_n,
           state_mlstm_m, w_mod, b_mod, w_in, conv_w, conv_b, w_ra, b_ra, w_ri, b_ri, lam, g_rg, b_ig, b_fg,
           g_ml, w_out, w_grp, b_grp, w_er, b_er, w1, w3, w2, g_final):
    bp, seq, d = x_prompt.shape
    bs, dec_seq, _ = x_sample.shape
    depth = w_mod.shape[0]
    d_rg = conv_w.shape[2]
    conv_width = conv_w.shape[1]
    _, _, heads, dk, dv = state_mlstm_C.shape
    d_ml = heads * dv
    n_groups, per_group = w_er.shape[1], w_er.shape[3]
    n_experts = w1.shape[1]
    t_p, t_s = bp * seq, bs * dec_seq
    t = t_p + t_s
    n_main = 2 * d_rg + 2 * heads * dk + 2 * d_ml
    q_off, k_off, v_off, o_off = 2 * d_rg, 2 * d_rg + heads * dk, 2 * d_rg + 2 * heads * dk, n_main - d_ml
    assert dec_seq == SUBLANES and seq % LANES == 0 and conv_width - 1 <= SUBLANES
    assert 2 * heads <= LANES and n_groups * (per_group + 1) <= LANES
    assert q_off % (heads * dk) == 0 and k_off % (heads * dk) == 0 and v_off % d_ml == 0 and o_off % d_ml == 0

    tok = _Tokens(t_p, t_s, seq, _pick_tile(np.gcd(seq, t_s), 256))
    tm_in = _pick_tile(np.gcd(t_p, t_s), 1024)
    tc = LANES
    assert t_s % tc == 0 and seq % tc == 0

    mc = -(-(bp + bs) // SUBLANES) * SUBLANES
    c_all = jnp.zeros((mc, d), F32).at[:bp].set(c_prompt).at[bp:bp + bs].set(c_sample)
    mod = _mod_call(c_all, w_mod, b_mod, _pick_tile(6 * d, 1024))
    mod_p = mod[:, :bp].reshape(depth, bp, 1, 6 * d)
    mod_s = mod[:, bp:bp + bs].reshape(depth, bs, 1, 6 * d)

    w_in_t = jnp.swapaxes(w_in, 1, 2)
    w_gate = jnp.zeros((depth, d, LANES), F32).at[:, :, :2 * heads].set(w_in[:, :, n_main:]).astype(BF16)
    b_gate = jnp.zeros((depth, 1, LANES), F32).at[:, 0, :heads].set(b_ig).at[:, 0, heads:2 * heads].set(b_fg)
    w_ra_b, w_ri_b = w_ra.astype(BF16), w_ri.astype(BF16)
    n_rt = n_groups * (per_group + 1)
    w_router = jnp.concatenate([w_grp, jnp.moveaxis(w_er, 1, 2).reshape(depth, d, n_groups * per_group)], axis=-1)
    w_router = jnp.zeros((depth, d, LANES), F32).at[:, :, :n_rt].set(w_router).astype(BF16)
    b_router = jnp.concatenate([b_grp, b_er.reshape(depth, -1)], axis=-1)
    b_router = jnp.zeros((depth, 1, LANES), F32).at[:, 0, :n_rt].set(b_router)
    vec = lambda p: p.reshape(depth, 1, -1)

    pad_rows = SUBLANES - (conv_width - 1)
    prev_s = jnp.pad(state_rg_conv, ((0, 0), (0, 0), (pad_rows, 0), (0, 0))).reshape(depth, bs * SUBLANES, d_rg)
    h0_s = state_rg_h.reshape(depth, bs, 1, d_rg)
    n_s = state_mlstm_n.reshape(depth, bs, heads, 1, dk)
    m_s_col = jnp.pad(jnp.repeat(state_mlstm_m, dec_seq, axis=1), ((0, 0), (0, 0), (0, LANES - heads)))

    x = jnp.concatenate([x_prompt.reshape(t_p, d), x_sample.reshape(t_s, d)], axis=0)
    h, gcol = _prenorm_call(x, mod_p, mod_s, w_gate, b_gate, heads, tok)
    outs_p = [[] for _ in range(5)]
    outs_s = [[] for _ in range(5)]
    cs_stack = jnp.zeros_like(state_mlstm_C)
    for l in range(depth):
        u = _in_call(h, w_in_t, l, n_main, tm_in, _pick_tile(n_main, 1024))
        hr, mix_rg = _rg_call(u, prev_s, h0_s, conv_w, vec(conv_b), w_ra_b, vec(b_ra), w_ri_b, vec(b_ri),
                              vec(lam), vec(g_rg), l, tok)
        grow = gcol[:, :2 * heads].T
        mix_ml, c_p, n_p, m_p, cs_stack, n_so, m_so = _ml_call(
            u, gcol, grow, state_mlstm_C, n_s, m_s_col, vec(g_ml), cs_stack, l, t_p, seq, dec_seq,
            q_off, k_off, v_off, o_off, tc)
        xn, h2, eid, wts = _out_call(x, mix_rg, mix_ml, mod_p, mod_s, w_out, w_router, b_router, l,
                                     n_groups, per_group, tok)
        tile_exp, next_exp, n_used, src_tok, dest = _route_plan(eid, n_experts, tok.tile)
        y = _moe_call(tile_exp, next_exp, n_used, src_tok, h2, w1, w3, w2, l, tok.tile)
        final = l == depth - 1
        res = _comb_call(dest, xn, wts, mod_p, mod_s, w_gate, b_gate, g_final[None], y, l, heads, tok, final)
        if final:
            y_p, y_s = res
        else:
            x, h, gcol = res

        tail = conv_width - 1
        outs_p[0].append(jnp.stack([u[(b + 1) * seq - tail:(b + 1) * seq, :d_rg] for b in range(bp)]))
        outs_p[1].append(jnp.stack([hr[(b + 1) * seq - 1] for b in range(bp)]))
        outs_p[2].append(c_p)
        outs_p[3].append(n_p.reshape(bp, heads, dk))
        outs_p[4].append(m_p[:, 0, :heads])
        outs_s[0].append(u[t_p:, :d_rg].reshape(bs, dec_seq, d_rg)[:, dec_seq - tail:])
        outs_s[1].append(hr[t_p:].reshape(bs, dec_seq, d_rg)[:, -1])
        outs_s[3].append(n_so.reshape(bs, heads, dk))
        outs_s[4].append(m_so.reshape(bs, dec_seq, LANES)[:, -1, :heads])

    stack = lambda parts: jnp.stack(parts)
    return (y_p.reshape(bp, seq, d), y_s.reshape(bs, dec_seq, d), *[stack(o) for o in outs_p],
            stack(outs_s[0]), stack(outs_s[1]), cs_stack, stack(outs_s[3]), stack(outs_s[4]))
```

```python
import functools

import numpy as np
import jax
import jax.numpy as jnp
from jax import lax
from jax.experimental import pallas as pl
from jax.experimental.pallas import tpu as pltpu

F32 = jnp.float32
BF16 = jnp.bfloat16
EPS = 1e-6
RG_C = 8.0
SUBLANES = 8
LANES = 128
VMEM_LIMIT = 56 * 1024 * 1024
HI = lax.Precision.HIGHEST
DMA_THREADS = 2
EXPERT_DMA_PLAN = ((3, 2, 0), (3, 1, 1), (3, 2, 1), (4, 1, 1))
SH_A, SC_A, GT_A, SH_F, SC_F, GT_F = range(6)


def _cparams(*sem):
    return pltpu.CompilerParams(dimension_semantics=sem, vmem_limit_bytes=VMEM_LIMIT)


def _softplus(z):
    return jnp.maximum(z, 0.0) + jnp.log1p(jnp.exp(-jnp.abs(z)))


def _rows3(x):
    r, c = x.shape
    return x.reshape(r // SUBLANES, SUBLANES, c)


class _Tokens:
    def __init__(self, t_prompt, t_sample, seq, tile):
        self.tile = tile
        self.groups = tile // SUBLANES
        self.n_prompt_tiles = t_prompt // tile
        self.tiles_per_seq = seq // tile
        self.n_prompt_seqs = t_prompt // seq

    def prompt_seq(self, i):
        return jnp.minimum(i // self.tiles_per_seq, self.n_prompt_seqs - 1)

    def sample_tile(self, i):
        return jnp.maximum(i - self.n_prompt_tiles, 0)

    def mod_specs(self, layer, chunk, d, tile_axis=0):
        def prompt_map(*idx):
            return (layer, self.prompt_seq(idx[tile_axis]), 0, chunk)

        def sample_map(*idx):
            return (layer, self.sample_tile(idx[tile_axis]), 0, chunk)

        return [pl.BlockSpec((1, 1, 1, d), prompt_map), pl.BlockSpec((1, self.groups, 1, d), sample_map)]


def _pick_mod(is_prompt, mp_ref, ms_ref):
    return jnp.where(is_prompt, mp_ref[0], ms_ref[0])


def _mod_kernel(c_ref, w_ref, b_ref, o_ref):
    c = c_ref[...]
    s = (c * jax.nn.sigmoid(c)).astype(BF16)
    o_ref[0] = jnp.dot(s, w_ref[0].astype(BF16), preferred_element_type=F32) + b_ref[0]


def _mod_call(c_all, w_mod, b_mod, tn):
    depth, d, n = w_mod.shape
    mc = c_all.shape[0]
    return pl.pallas_call(
        _mod_kernel,
        grid=(depth, n // tn),
        in_specs=[
            pl.BlockSpec((mc, d), lambda l, j: (0, 0)),
            pl.BlockSpec((1, d, tn), lambda l, j: (l, 0, j)),
            pl.BlockSpec((1, 1, tn), lambda l, j: (l, 0, j)),
        ],
        out_specs=pl.BlockSpec((1, mc, tn), lambda l, j: (l, 0, j)),
        out_shape=jax.ShapeDtypeStruct((depth, mc, n), F32),
        compiler_params=_cparams("arbitrary", "arbitrary"),
        name="mod",
    )(c_all, w_mod, b_mod.reshape(depth, 1, n))


def _norm_gates(x3, sc, sh, wg_ref, bg_ref, h_ref, g_ref, heads):
    groups, _, d = x3.shape
    xn = x3 * lax.rsqrt(jnp.mean(x3 * x3, axis=-1, keepdims=True) + EPS)
    hb = (xn * (1.0 + sc) + sh).reshape(groups * SUBLANES, d).astype(BF16)
    h_ref[...] = hb
    g = jnp.dot(hb, wg_ref[0], preferred_element_type=F32) + bg_ref[0]
    lane = lax.broadcasted_iota(jnp.int32, g.shape, 1)
    is_forget = (lane >= heads) & (lane < 2 * heads)
    g_ref[...] = jnp.where(is_forget, -_softplus(-g), g)


def _prenorm_kernel(x_ref, shp_ref, shs_ref, scp_ref, scs_ref, wg_ref, bg_ref, h_ref, g_ref, *, heads, n_prompt_tiles):
    is_prompt = pl.program_id(0) < n_prompt_tiles
    _norm_gates(_rows3(x_ref[...]), _pick_mod(is_prompt, scp_ref, scs_ref), _pick_mod(is_prompt, shp_ref, shs_ref),
                wg_ref, bg_ref, h_ref, g_ref, heads)


def _prenorm_call(x, mod_p, mod_s, w_gate, b_gate, heads, tok):
    t, d = x.shape
    tm = tok.tile
    return pl.pallas_call(
        functools.partial(_prenorm_kernel, heads=heads, n_prompt_tiles=tok.n_prompt_tiles),
        grid=(t // tm,),
        in_specs=[
            pl.BlockSpec((tm, d), lambda i: (i, 0)),
            *tok.mod_specs(0, SH_A, d),
            *tok.mod_specs(0, SC_A, d),
            pl.BlockSpec((1, d, LANES), lambda i: (0, 0, 0)),
            pl.BlockSpec((1, 1, LANES), lambda i: (0, 0, 0)),
        ],
        out_specs=[
            pl.BlockSpec((tm, d), lambda i: (i, 0)),
            pl.BlockSpec((tm, LANES), lambda i: (i, 0)),
        ],
        out_shape=[
            jax.ShapeDtypeStruct((t, d), BF16),
            jax.ShapeDtypeStruct((t, LANES), F32),
        ],
        compiler_params=_cparams("arbitrary"),
        name="prenorm",
    )(x, mod_p, mod_s, mod_p, mod_s, w_gate, b_gate)


def _in_kernel(h_ref, w_ref, u_ref, wb_scr):
    @pl.when(pl.program_id(1) == 0)
    def _():
        wb_scr[...] = w_ref[0].astype(BF16)

    u_ref[...] = lax.dot_general(h_ref[...], wb_scr[...], (((1,), (1,)), ((), ())), preferred_element_type=F32)


def _in_call(h, w_in_t, layer, n_main, tm, tn):
    t, d = h.shape
    return pl.pallas_call(
        _in_kernel,
        grid=(n_main // tn, t // tm),
        in_specs=[
            pl.BlockSpec((tm, d), lambda j, i: (i, 0)),
            pl.BlockSpec((1, tn, d), lambda j, i: (layer, j, 0)),
        ],
        out_specs=pl.BlockSpec((tm, tn), lambda j, i: (i, j)),
        out_shape=jax.ShapeDtypeStruct((t, n_main), F32),
        scratch_shapes=[pltpu.VMEM((tn, d), BF16)],
        compiler_params=_cparams("arbitrary", "arbitrary"),
        name="in_proj",
    )(h, w_in_t)


def _rg_kernel(xr_ref, yr_ref, prev_ref, h0_ref, cw_ref, cb_ref, wra_ref, bra_ref, wri_ref, bri_ref,
               lam_ref, gain_ref, hr_ref, mix_ref, xprev_scr, hprev_scr, hstart_scr, a_scr, b_scr,
               *, n_prompt_tiles, tiles_per_seq):
    i = pl.program_id(0)
    tl, c = xr_ref.shape
    groups = tl // SUBLANES
    _, nblk, bw, _ = wra_ref.shape
    is_prompt = i < n_prompt_tiles

    @pl.when(is_prompt & (i % tiles_per_seq == 0))
    def _():
        xprev_scr[...] = jnp.zeros_like(xprev_scr)
        hprev_scr[...] = jnp.zeros_like(hprev_scr)

    x = xr_ref[...]
    prev_prompt = jnp.concatenate([xprev_scr[...], x[:tl - SUBLANES]], axis=0)
    prev = jnp.where(is_prompt, prev_prompt, prev_ref[0])
    xprev_scr[...] = x[tl - SUBLANES:]

    x3 = _rows3(x)
    p3 = _rows3(prev)
    row = lax.broadcasted_iota(jnp.int32, x3.shape, 1)
    cw = cw_ref[0]
    conv_w = cw.shape[0]
    xc = cb_ref[0] + x3 * cw[conv_w - 1:conv_w]
    for dlt in range(1, conv_w):
        shifted = jnp.where(row >= dlt, pltpu.roll(x3, dlt, 1), pltpu.roll(p3, dlt, 1))
        xc = xc + shifted * cw[conv_w - 1 - dlt:conv_w - dlt]

    xc2 = xc.reshape(tl, c)
    xcb = xc2.astype(BF16)
    r_parts, i_parts = [], []
    for nb in range(nblk):
        blk = xcb[:, nb * bw:(nb + 1) * bw]
        r_parts.append(jnp.dot(blk, wra_ref[0, nb], preferred_element_type=F32))
        i_parts.append(jnp.dot(blk, wri_ref[0, nb], preferred_element_type=F32))
    r = jax.nn.sigmoid(jnp.concatenate(r_parts, axis=-1) + bra_ref[0])
    ig = jax.nn.sigmoid(jnp.concatenate(i_parts, axis=-1) + bri_ref[0])
    log_a = (-RG_C) * r * _softplus(-lam_ref[0])
    a = jnp.exp(log_a)
    th = jnp.tanh(log_a)
    mult = jnp.sqrt(-2.0 * th / (1.0 - th))
    bx = mult * (ig * xc2)

    av = _rows3(a)
    bv = _rows3(bx)
    for s in (1, 2, 4):
        a_sh = jnp.where(row >= s, pltpu.roll(av, s, 1), 1.0)
        b_sh = jnp.where(row >= s, pltpu.roll(bv, s, 1), 0.0)
        bv = av * b_sh + bv
        av = av * a_sh

    @pl.when(is_prompt)
    def _():
        a_scr[...] = av.reshape(tl, c)
        b_scr[...] = bv.reshape(tl, c)

        def body(g, h):
            hstart_scr[g] = h
            last = g * SUBLANES + (SUBLANES - 1)
            return a_scr[pl.ds(last, 1), :] * h + b_scr[pl.ds(last, 1), :]

        hprev_scr[...] = lax.fori_loop(0, groups, body, hprev_scr[...])

    @pl.when(jnp.logical_not(is_prompt))
    def _():
        hstart_scr[...] = h0_ref[0]

    h3 = av * hstart_scr[...] + bv
    hr = h3.reshape(tl, c)
    hr_ref[...] = hr
    y = yr_ref[...]
    gelu = 0.5 * y * (1.0 + jnp.tanh(np.sqrt(2.0 / np.pi).astype(np.float32) * (y + 0.044715 * (y * y * y))))
    hn = hr * lax.rsqrt(jnp.mean(hr * hr, axis=-1, keepdims=True) + EPS)
    mix_ref[...] = (hn * gain_ref[0] * gelu).astype(BF16)


def _rg_call(u, prev_s, h0_s, conv_w, conv_b, w_ra, b_ra, w_ri, b_ri, lam, g_rg, layer, tok):
    t = u.shape[0]
    _, conv_width, c = conv_w.shape
    _, nblk, bw, _ = w_ra.shape
    tl, groups = tok.tile, tok.groups
    vec = pl.BlockSpec((1, 1, c), lambda i: (layer, 0, 0))
    return pl.pallas_call(
        functools.partial(_rg_kernel, n_prompt_tiles=tok.n_prompt_tiles, tiles_per_seq=tok.tiles_per_seq),
        grid=(t // tl,),
        in_specs=[
            pl.BlockSpec((tl, c), lambda i: (i, 0)),
            pl.BlockSpec((tl, c), lambda i: (i, 1)),
            pl.BlockSpec((1, tl, c), lambda i: (layer, tok.sample_tile(i), 0)),
            pl.BlockSpec((1, groups, 1, c), lambda i: (layer, tok.sample_tile(i), 0, 0)),
            pl.BlockSpec((1, conv_width, c), lambda i: (layer, 0, 0)),
            vec,
            pl.BlockSpec((1, nblk, bw, bw), lambda i: (layer, 0, 0, 0)),
            vec,
            pl.BlockSpec((1, nblk, bw, bw), lambda i: (layer, 0, 0, 0)),
            vec,
            vec,
            vec,
        ],
        out_specs=[
            pl.BlockSpec((tl, c), lambda i: (i, 0)),
            pl.BlockSpec((tl, c), lambda i: (i, 0)),
        ],
        out_shape=[
            jax.ShapeDtypeStruct((t, c), F32),
            jax.ShapeDtypeStruct((t, c), BF16),
        ],
        scratch_shapes=[
            pltpu.VMEM((SUBLANES, c), F32),
            pltpu.VMEM((1, c), F32),
            pltpu.VMEM((groups, 1, c), F32),
            pltpu.VMEM((tl, c), F32),
            pltpu.VMEM((tl, c), F32),
        ],
        compiler_params=_cparams("arbitrary"),
        name="rg_lru",
    )(u, u, prev_s, h0_s, conv_w, conv_b, w_ra, b_ra, w_ri, b_ri, lam, g_rg)


def _ml_chunk(nseq, heads, dk, dv, q_ref, k_ref, v_ref, o_ref, gc, gr, mprev, gain_ref, mix_ref,
              get_c, get_n, put_state):
    tc = q_ref.shape[0]
    ls = tc // nseq
    shift = int(np.log2(ls))
    t_idx = lax.broadcasted_iota(jnp.int32, (tc, tc), 0)
    s_idx = lax.broadcasted_iota(jnp.int32, (tc, tc), 1)
    mask = (lax.shift_right_logical(t_idx, shift) == lax.shift_right_logical(s_idx, shift)) & (s_idx <= t_idx)
    m_cum = mask.astype(F32)
    bcol = jnp.dot(m_cum, gc, precision=HI, preferred_element_type=F32)
    brow = lax.dot_general(gr, m_cum, (((1,), (1,)), ((), ())), precision=HI,
                           preferred_element_type=F32)
    lane = lax.broadcasted_iota(jnp.int32, (tc, LANES), 1)
    scale = np.float32(dk ** -0.5)

    def seg_last(col):
        if nseq == 1:
            return col[tc - 1:tc, :].reshape(1, 1, 1)
        return col.reshape(nseq, ls, 1)[:, ls - 1:ls, :]

    def seg_bcast(val):
        return jnp.broadcast_to(val, (nseq, ls, 1)).reshape(tc, 1)

    m_out = jnp.zeros((tc, LANES), F32)
    for h in range(heads):
        bc = bcol[:, heads + h:heads + h + 1]
        br = brow[heads + h:heads + h + 1, :]
        igr = gr[h:h + 1, :]
        igc = gc[:, h:h + 1]
        mp = mprev[:, h:h + 1]
        dmat = jnp.where(mask, bc - br + igr, -jnp.inf)
        m_inter = bc + mp
        m_t = jnp.maximum(m_inter, jnp.max(dmat, axis=-1, keepdims=True))
        p = jnp.exp(dmat - m_t)
        qh = q_ref[:, h * dk:(h + 1) * dk] * scale
        kh = k_ref[:, h * dk:(h + 1) * dk]
        vh = v_ref[:, h * dv:(h + 1) * dv]
        qb = qh.astype(BF16)
        kb = kh.astype(BF16)
        s = lax.dot_general(qb, kb, (((1,), (1,)), ((), ())), preferred_element_type=F32) * p
        w_inter = jnp.exp(m_inter - m_t)
        c_old = get_c(h)
        n_old = get_n(h)
        q3 = qh.reshape(nseq, ls, dk)
        inter = lax.dot_general(q3.astype(BF16), c_old.astype(BF16), (((2,), (1,)), ((0,), (0,))),
                                preferred_element_type=F32).reshape(tc, dv)
        qn = jnp.sum(q3 * n_old, axis=-1, keepdims=True).reshape(tc, 1)
        num = jnp.dot(s.astype(BF16), vh.astype(BF16), preferred_element_type=F32) + w_inter * inter
        den = jnp.sum(s, axis=-1, keepdims=True) + w_inter * qn
        hh = num / jnp.maximum(jnp.abs(den), jnp.exp(-m_t))
        hn = hh * lax.rsqrt(jnp.mean(hh * hh, axis=-1, keepdims=True) + EPS)
        og = jax.nn.sigmoid(o_ref[:, h * dv:(h + 1) * dv])
        mix_ref[:, h * dv:(h + 1) * dv] = (hn * gain_ref[0, :, h * dv:(h + 1) * dv] * og).astype(BF16)

        m_new = seg_last(m_t)
        b_last = seg_last(bc)
        wl = jnp.exp(seg_bcast(b_last) - bc + igc - seg_bcast(m_new))
        decay = jnp.exp(b_last + seg_last(mp) - m_new)
        wv3 = (wl * vh).reshape(nseq, ls, dv).astype(BF16)
        k3 = kh.reshape(nseq, ls, dk).astype(BF16)
        upd = lax.dot_general(k3, wv3, (((1,), (1,)), ((0,), (0,))), preferred_element_type=F32)
        c_new = decay * c_old + upd
        n_new = decay * n_old + jnp.sum((wl * kh).reshape(nseq, ls, dk), axis=1, keepdims=True)
        put_state(h, c_new, n_new)
        m_out = jnp.where(lane == h, seg_bcast(m_new), m_out)
    return m_out


def _ml_kernel(q_ref, k_ref, v_ref, o_ref, gc_ref, gr_ref, cs_ref, ns_ref, ms_ref, gain_ref, cs_alias,
               mix_ref, cp_out, np_out, mp_out, cs_out, ns_out, ms_out, c_scr, n_scr, m_scr,
               *, n_prompt_chunks, chunks_per_seq, heads, dk, dv, nseq_sample):
    del cs_alias
    i = pl.program_id(0)
    tc = q_ref.shape[0]
    is_prompt = i < n_prompt_chunks

    @pl.when(is_prompt & (i % chunks_per_seq == 0))
    def _():
        c_scr[...] = jnp.zeros_like(c_scr)
        n_scr[...] = jnp.zeros_like(n_scr)
        m_scr[...] = jnp.zeros_like(m_scr)

    @pl.when(is_prompt)
    def _():
        def put(h, c_new, n_new):
            c_scr[h] = c_new[0]
            n_scr[h] = n_new[0]

        mprev = jnp.broadcast_to(m_scr[...], (tc, LANES))
        m_out = _ml_chunk(1, heads, dk, dv, q_ref, k_ref, v_ref, o_ref, gc_ref[...], gr_ref[...], mprev,
                          gain_ref, mix_ref, lambda h: c_scr[h][None], lambda h: n_scr[h][None], put)
        m_scr[...] = m_out[:1, :]
        cp_out[0] = c_scr[...]
        np_out[0] = n_scr[...]
        mp_out[0] = m_out[:1, :]

    @pl.when(jnp.logical_not(is_prompt))
    def _():
        def put(h, c_new, n_new):
            cs_out[0, :, h] = c_new
            ns_out[:, h] = n_new

        m_out = _ml_chunk(nseq_sample, heads, dk, dv, q_ref, k_ref, v_ref, o_ref, gc_ref[...], gr_ref[...],
                          ms_ref[0], gain_ref, mix_ref, lambda h: cs_ref[0, :, h], lambda h: ns_ref[0, :, h], put)
        ms_out[...] = m_out


def _ml_call(u, gcol, grow, c_s, n_s, m_s_col, g_ml, cs_stack, layer, t_prompt, seq, dec_seq,
             q_off, k_off, v_off, o_off, tc):
    t = u.shape[0]
    depth, bs, heads, dk, dv = c_s.shape
    bp = t_prompt // seq
    hdk, hdv = heads * dk, heads * dv
    n_p = t_prompt // tc
    cps = seq // tc
    nseq = tc // dec_seq
    samp = lambda i: jnp.maximum(i - n_p, 0)
    pseq = lambda i: jnp.minimum(i // cps, bp - 1)
    t_s = t - t_prompt
    in_specs = [
        pl.BlockSpec((tc, hdk), lambda i: (i, q_off // hdk)),
        pl.BlockSpec((tc, hdk), lambda i: (i, k_off // hdk)),
        pl.BlockSpec((tc, hdv), lambda i: (i, v_off // hdv)),
        pl.BlockSpec((tc, hdv), lambda i: (i, o_off // hdv)),
        pl.BlockSpec((tc, LANES), lambda i: (i, 0)),
        pl.BlockSpec((2 * heads, tc), lambda i: (0, i)),
        pl.BlockSpec((1, nseq, heads, dk, dv), lambda i: (layer, samp(i), 0, 0, 0)),
        pl.BlockSpec((1, nseq, heads, 1, dk), lambda i: (layer, samp(i), 0, 0, 0)),
        pl.BlockSpec((1, tc, LANES), lambda i: (layer, samp(i), 0)),
        pl.BlockSpec((1, 1, hdv), lambda i: (layer, 0, 0)),
        pl.BlockSpec(memory_space=pl.ANY),
    ]
    args = [u, u, u, u, gcol, grow, c_s, n_s, m_s_col, g_ml, cs_stack]
    return pl.pallas_call(
        functools.partial(_ml_kernel, n_prompt_chunks=n_p, chunks_per_seq=cps, heads=heads, dk=dk, dv=dv,
                          nseq_sample=nseq),
        grid=(t // tc,),
        in_specs=in_specs,
        out_specs=[
            pl.BlockSpec((tc, hdv), lambda i: (i, 0)),
            pl.BlockSpec((1, heads, dk, dv), lambda i: (pseq(i), 0, 0, 0)),
            pl.BlockSpec((1, heads, 1, dk), lambda i: (pseq(i), 0, 0, 0)),
            pl.BlockSpec((1, 1, LANES), lambda i: (pseq(i), 0, 0)),
            pl.BlockSpec((1, nseq, heads, dk, dv), lambda i: (layer, samp(i), 0, 0, 0)),
            pl.BlockSpec((nseq, heads, 1, dk), lambda i: (samp(i), 0, 0, 0)),
            pl.BlockSpec((tc, LANES), lambda i: (samp(i), 0)),
        ],
        out_shape=[
            jax.ShapeDtypeStruct((t, hdv), BF16),
            jax.ShapeDtypeStruct((bp, heads, dk, dv), F32),
            jax.ShapeDtypeStruct((bp, heads, 1, dk), F32),
            jax.ShapeDtypeStruct((bp, 1, LANES), F32),
            jax.ShapeDtypeStruct((depth, bs, heads, dk, dv), F32),
            jax.ShapeDtypeStruct((bs, heads, 1, dk), F32),
            jax.ShapeDtypeStruct((t_s, LANES), F32),
        ],
        scratch_shapes=[
            pltpu.VMEM((heads, dk, dv), F32),
            pltpu.VMEM((heads, 1, dk), F32),
            pltpu.VMEM((1, LANES), F32),
        ],
        input_output_aliases={len(args) - 1: 4},
        compiler_params=_cparams("arbitrary"),
        name="mlstm",
    )(*args)


def _out_kernel(x_ref, mr_ref, mm_ref, gtp_ref, gts_ref, shp_ref, shs_ref, scp_ref, scs_ref, wo_hbm, wr_ref, br_ref,
                xn_ref, h2_ref, eid_ref, wt_ref, wo_scr, stage, sem, *, layer, n_groups, per_group, n_prompt_tiles):
    i = pl.program_id(0)
    tm, d = x_ref.shape
    c = mr_ref.shape[1]
    rows = stage.shape[0]

    @pl.when(i == 0)
    def _():
        for part in range(wo_scr.shape[0] // rows):
            cp = pltpu.make_async_copy(wo_hbm.at[layer, pl.ds(part * rows, rows)], stage, sem)
            cp.start()
            cp.wait()
            wo_scr[pl.ds(part * rows, rows), :] = stage[...].astype(BF16)

    is_prompt = i < n_prompt_tiles
    acc = jnp.dot(mr_ref[...], wo_scr[:c, :], preferred_element_type=F32)
    acc = acc + jnp.dot(mm_ref[...], wo_scr[c:, :], preferred_element_type=F32)
    xn3 = _rows3(x_ref[...]) + _pick_mod(is_prompt, gtp_ref, gts_ref) * _rows3(acc)
    rs = lax.rsqrt(jnp.mean(xn3 * xn3, axis=-1, keepdims=True) + EPS)
    h2 = (xn3 * rs * (1.0 + _pick_mod(is_prompt, scp_ref, scs_ref))
          + _pick_mod(is_prompt, shp_ref, shs_ref)).reshape(tm, d)
    xn_ref[...] = xn3.reshape(tm, d)
    h2_ref[...] = h2

    logits = jnp.dot(h2.astype(BF16), wr_ref[0], preferred_element_type=F32) + br_ref[0]
    lane = lax.broadcasted_iota(jnp.int32, logits.shape, 1)
    lane_f = lane.astype(F32)
    big = np.float32(LANES)

    def first_max(vals):
        vmax = jnp.max(vals, axis=-1, keepdims=True)
        idx = jnp.min(jnp.where(vals == vmax, lane_f, big), axis=-1, keepdims=True)
        return vmax, idx.astype(jnp.int32)

    gl = jnp.where(lane < n_groups, logits, -jnp.inf)
    gmax, g_sel = first_max(gl)
    p_g = 1.0 / jnp.sum(jnp.exp(gl - gmax), axis=-1, keepdims=True)
    lo = n_groups + g_sel * per_group
    el = jnp.where((lane >= lo) & (lane < lo + per_group), logits, -jnp.inf)
    v1, i1 = first_max(el)
    v2, i2 = first_max(jnp.where(lane == i1, -jnp.inf, el))
    e21 = jnp.exp(v2 - v1)
    w1 = 1.0 / (1.0 + e21)
    w2 = e21 * w1
    eid_ref[...] = jnp.where(lane == 0, i1 - n_groups, jnp.where(lane == 1, i2 - n_groups, 0))
    wt_ref[...] = jnp.where(lane == 0, w1 * p_g, jnp.where(lane == 1, w2 * p_g, 0.0))


def _out_call(x, mix_rg, mix_ml, mod_p, mod_s, w_out, w_router, b_router, layer, n_groups, per_group, tok):
    t, d = x.shape
    c = mix_rg.shape[1]
    cm = mix_ml.shape[1]
    tm = tok.tile
    stage_rows = min(256, c + cm)
    return pl.pallas_call(
        functools.partial(_out_kernel, layer=layer, n_groups=n_groups, per_group=per_group,
                          n_prompt_tiles=tok.n_prompt_tiles),
        grid=(t // tm,),
        in_specs=[
            pl.BlockSpec((tm, d), lambda i: (i, 0)),
            pl.BlockSpec((tm, c), lambda i: (i, 0)),
            pl.BlockSpec((tm, cm), lambda i: (i, 0)),
            *tok.mod_specs(layer, GT_A, d),
            *tok.mod_specs(layer, SH_F, d),
            *tok.mod_specs(layer, SC_F, d),
            pl.BlockSpec(memory_space=pl.ANY),
            pl.BlockSpec((1, d, LANES), lambda i: (layer, 0, 0)),
            pl.BlockSpec((1, 1, LANES), lambda i: (layer, 0, 0)),
        ],
        out_specs=[
            pl.BlockSpec((tm, d), lambda i: (i, 0)),
            pl.BlockSpec((tm, d), lambda i: (i, 0)),
            pl.BlockSpec((tm, LANES), lambda i: (i, 0)),
            pl.BlockSpec((tm, LANES), lambda i: (i, 0)),
        ],
        out_shape=[
            jax.ShapeDtypeStruct((t, d), F32),
            jax.ShapeDtypeStruct((t, d), F32),
            jax.ShapeDtypeStruct((t, LANES), jnp.int32),
            jax.ShapeDtypeStruct((t, LANES), F32),
        ],
        scratch_shapes=[
            pltpu.VMEM((c + cm, d), BF16),
            pltpu.VMEM((stage_rows, d), F32),
            pltpu.SemaphoreType.DMA(()),
        ],
        compiler_params=_cparams("arbitrary"),
        name="out_proj_router",
    )(x, mix_rg, mix_ml, mod_p, mod_s, mod_p, mod_s, mod_p, mod_s, w_out, w_router, b_router)


def _gather_rows(idx_ref, base, src_hbm, dst, sem, stride=1, offset=0, threads=DMA_THREADS):
    def body(g, carry):
        for s in range(SUBLANES):
            tok = idx_ref[base + (g * SUBLANES + s) * stride + offset]
            pltpu.make_async_copy(src_hbm.at[pl.ds(tok, 1)], dst.at[g, pl.ds(s, 1)], sem).start(
                priority=s % threads)
        return carry

    lax.fori_loop(0, dst.shape[0], body, 0)


def _wait_rows(buf, sem):
    pltpu.make_async_copy(buf, buf, sem).wait()


def _moe_kernel(te_ref, nx_ref, nu_ref, src_ref, h2_hbm, w1_hbm, w3_hbm, w2_hbm, y_ref,
                xbuf, st1, st3, st2, w1_scr, w3_scr, w2_scr, slot_ref, gsem, wsem,
                *, layer, gather_threads, weight_thread):
    j = pl.program_id(0)
    nbuf, groups, _, d = xbuf.shape
    tm = groups * SUBLANES
    n_used = nu_ref[0]

    def weight_copies(expert, slot):
        return [pltpu.make_async_copy(w_hbm.at[layer, expert], stage.at[slot], wsem.at[slot])
                for w_hbm, stage in ((w1_hbm, st1), (w3_hbm, st3), (w2_hbm, st2))]

    def issue(tile):
        _gather_rows(src_ref, tile * tm, h2_hbm, xbuf.at[tile % nbuf], gsem.at[tile % nbuf],
                     threads=gather_threads)

    @pl.when(j == 0)
    def _():
        slot_ref[0] = 1
        for cp in weight_copies(te_ref[0], 0):
            cp.start(priority=weight_thread)
        for ahead in range(nbuf - 1):
            @pl.when(ahead < n_used)
            def _():
                issue(ahead)

    @pl.when(j + (nbuf - 1) < n_used)
    def _():
        issue(j + (nbuf - 1))

    @pl.when((j < n_used) & ((j == 0) | (te_ref[j] != te_ref[jnp.maximum(j - 1, 0)])))
    def _():
        slot = 1 - slot_ref[0]
        slot_ref[0] = slot
        for cp in weight_copies(te_ref[j], slot):
            cp.wait()
        w1_scr[...] = st1[slot].astype(BF16)
        w3_scr[...] = st3[slot].astype(BF16)
        w2_scr[...] = st2[slot].astype(BF16)

        @pl.when(nx_ref[j] >= 0)
        def _():
            for cp in weight_copies(nx_ref[j], 1 - slot):
                cp.start(priority=weight_thread)

    @pl.when(j < n_used)
    def _():
        slot = j % nbuf
        _wait_rows(xbuf.at[slot], gsem.at[slot])
        xb = xbuf[slot].reshape(tm, d).astype(BF16)
        a = jnp.dot(xb, w1_scr[...], preferred_element_type=F32)
        b = jnp.dot(xb, w3_scr[...], preferred_element_type=F32)
        hid = (a * jax.nn.sigmoid(a)) * b
        y_ref[...] = jnp.dot(hid.astype(BF16), w2_scr[...], preferred_element_type=F32)

    @pl.when(j >= n_used)
    def _():
        y_ref[...] = jnp.zeros_like(y_ref)


def _moe_call(tile_exp, next_exp, n_used, src_tok, h2, w1, w3, w2, layer, tm, gather_bufs, gather_threads,
              weight_thread):
    t, d = h2.shape
    _, ne, _, f = w1.shape
    n_tiles = tile_exp.shape[0]
    grid_spec = pltpu.PrefetchScalarGridSpec(
        num_scalar_prefetch=4,
        grid=(n_tiles,),
        in_specs=[pl.BlockSpec(memory_space=pl.ANY)] * 4,
        out_specs=pl.BlockSpec((tm, d), lambda j, *_: (j, 0)),
        scratch_shapes=[
            pltpu.VMEM((gather_bufs, tm // SUBLANES, SUBLANES, d), F32),
            pltpu.VMEM((2, d, f), F32),
            pltpu.VMEM((2, d, f), F32),
            pltpu.VMEM((2, f, d), F32),
            pltpu.VMEM((d, f), BF16),
            pltpu.VMEM((d, f), BF16),
            pltpu.VMEM((f, d), BF16),
            pltpu.SMEM((1,), jnp.int32),
            pltpu.SemaphoreType.DMA((gather_bufs,)),
            pltpu.SemaphoreType.DMA((2,)),
        ],
    )
    return pl.pallas_call(
        functools.partial(_moe_kernel, layer=layer, gather_threads=gather_threads, weight_thread=weight_thread),
        grid_spec=grid_spec,
        out_shape=jax.ShapeDtypeStruct((n_tiles * tm, d), F32),
        compiler_params=_cparams("arbitrary"),
        name="experts",
    )(tile_exp, next_exp, n_used, src_tok, h2, w1, w3, w2)


def _comb_kernel(dest_ref, xn_ref, wt_ref, gtp_ref, gts_ref, *rest, final, heads, n_prompt_tiles):
    if final:
        gf_ref, y_hbm, yp_ref, ys_ref, ybuf, sem = rest
    else:
        shp_ref, shs_ref, scp_ref, scs_ref, wg_ref, bg_ref, y_hbm, xo_ref, h_ref, g_ref, ybuf, sem = rest
    i = pl.program_id(0)
    n = pl.num_programs(0)
    tm, d = xn_ref.shape
    is_prompt = i < n_prompt_tiles

    def issue(tile, slot):
        for k in range(2):
            _gather_rows(dest_ref, tile * tm * 2, y_hbm, ybuf.at[slot, k], sem.at[slot], stride=2, offset=k)

    @pl.when(i == 0)
    def _():
        issue(0, 0)

    @pl.when(i + 1 < n)
    def _():
        issue(i + 1, (i + 1) % 2)

    slot = i % 2
    for k in range(2):
        _wait_rows(ybuf.at[slot, k], sem.at[slot])
    wt = wt_ref[...]
    y3 = _rows3(wt[:, 0:1]) * ybuf[slot, 0] + _rows3(wt[:, 1:2]) * ybuf[slot, 1]
    xo3 = _rows3(xn_ref[...]) + _pick_mod(is_prompt, gtp_ref, gts_ref) * y3
    if final:
        xo = xo3.reshape(tm, d)
        yo = xo * lax.rsqrt(jnp.mean(xo * xo, axis=-1, keepdims=True) + EPS) * gf_ref[...]

        @pl.when(is_prompt)
        def _():
            yp_ref[...] = yo

        @pl.when(jnp.logical_not(is_prompt))
        def _():
            ys_ref[...] = yo
    else:
        xo_ref[...] = xo3.reshape(tm, d)
        _norm_gates(xo3, _pick_mod(is_prompt, scp_ref, scs_ref), _pick_mod(is_prompt, shp_ref, shs_ref),
                    wg_ref, bg_ref, h_ref, g_ref, heads)


def _comb_call(dest, xn, wts, mod_p, mod_s, w_gate, b_gate, g_final, y, layer, heads, tok, final):
    t, d = xn.shape
    tm = tok.tile
    full = lambda i, dst: (0, 0)
    in_specs = [
        pl.BlockSpec((tm, d), lambda i, dst: (i, 0)),
        pl.BlockSpec((tm, LANES), lambda i, dst: (i, 0)),
        *tok.mod_specs(layer, GT_F, d),
    ]
    args = [xn, wts, mod_p, mod_s]
    if final:
        n_p = tok.n_prompt_tiles
        in_specs.append(pl.BlockSpec((1, d), full))
        args.append(g_final)
        out_specs = [pl.BlockSpec((tm, d), lambda i, dst: (jnp.minimum(i, n_p - 1), 0)),
                     pl.BlockSpec((tm, d), lambda i, dst: (tok.sample_tile(i), 0))]
        out_shape = [jax.ShapeDtypeStruct((n_p * tm, d), F32), jax.ShapeDtypeStruct((t - n_p * tm, d), F32)]
    else:
        out_specs = [pl.BlockSpec((tm, d), lambda i, dst: (i, 0))]
        out_shape = [jax.ShapeDtypeStruct((t, d), F32)]
        nxt = layer + 1
        in_specs += [
            *tok.mod_specs(nxt, SH_A, d),
            *tok.mod_specs(nxt, SC_A, d),
            pl.BlockSpec((1, d, LANES), lambda i, dst: (nxt, 0, 0)),
            pl.BlockSpec((1, 1, LANES), lambda i, dst: (nxt, 0, 0)),
        ]
        args += [mod_p, mod_s, mod_p, mod_s, w_gate, b_gate]
        out_specs += [pl.BlockSpec((tm, d), lambda i, dst: (i, 0)), pl.BlockSpec((tm, LANES), lambda i, dst: (i, 0))]
        out_shape += [jax.ShapeDtypeStruct((t, d), BF16), jax.ShapeDtypeStruct((t, LANES), F32)]
    in_specs.append(pl.BlockSpec(memory_space=pl.ANY))
    args.append(y)
    grid_spec = pltpu.PrefetchScalarGridSpec(
        num_scalar_prefetch=1,
        grid=(t // tm,),
        in_specs=in_specs,
        out_specs=out_specs,
        scratch_shapes=[pltpu.VMEM((2, 2, tm // SUBLANES, SUBLANES, d), F32), pltpu.SemaphoreType.DMA((2,))],
    )
    return pl.pallas_call(
        functools.partial(_comb_kernel, final=final, heads=heads, n_prompt_tiles=tok.n_prompt_tiles),
        grid_spec=grid_spec,
        out_shape=out_shape,
        compiler_params=_cparams("arbitrary"),
        name="combine",
    )(dest, *args)


def _route_plan(eid, n_experts, tm):
    t = eid.shape[0]
    e_flat = eid[:, :2].reshape(-1)
    n_pairs = 2 * t
    n_tiles = -(-n_pairs // tm) + n_experts
    onehot = (e_flat[:, None] == jnp.arange(n_experts, dtype=jnp.int32)[None, :]).astype(jnp.int32)
    csum = jnp.cumsum(onehot, axis=0)
    rank = jnp.sum(csum * onehot, axis=1) - 1
    counts = csum[-1]
    padded = ((counts + tm - 1) // tm) * tm
    ends = jnp.cumsum(padded)
    starts = ends - padded
    dest = (jnp.sum(starts[None, :] * onehot, axis=1) + rank).astype(jnp.int32)
    src_tok = jnp.zeros((n_tiles * tm,), jnp.int32).at[dest].set(jnp.arange(n_pairs, dtype=jnp.int32) // 2)
    n_used = (ends[-1] // tm).astype(jnp.int32)
    tile_id = jnp.minimum(jnp.arange(n_tiles, dtype=jnp.int32), n_used - 1)
    tile_exp = jnp.sum((ends[None, :] <= (tile_id * tm)[:, None]).astype(jnp.int32), axis=1)
    tile_exp = jnp.minimum(tile_exp, n_experts - 1).astype(jnp.int32)
    ids = jnp.arange(n_experts, dtype=jnp.int32)
    later = (ids[None, :] > ids[:, None]) & (padded[None, :] > 0)
    following = jnp.min(jnp.where(later, ids[None, :], n_experts), axis=1)
    following = jnp.where(following == n_experts, -1, following).astype(jnp.int32)
    return tile_exp, following[tile_exp], n_used.reshape(1), src_tok, dest


def _pick_tile(total, pref):
    tile = min(pref, total)
    while total % tile:
        tile //= 2
    return tile


def kernel(x_prompt, x_sample, c_prompt, c_sample, state_rg_conv, state_rg_h, state_mlstm_C, state_mlstm_n,
           state_mlstm_m, w_mod, b_mod, w_in, conv_w, conv_b, w_ra, b_ra, w_ri, b_ri, lam, g_rg, b_ig, b_fg,
           g_ml, w_out, w_grp, b_grp, w_er, b_er, w1, w3, w2, g_final):
    bp, seq, d = x_prompt.shape
    bs, dec_seq, _ = x_sample.shape
    depth = w_mod.shape[0]
    d_rg = conv_w.shape[2]
    conv_width = conv_w.shape[1]
    _, _, heads, dk, dv = state_mlstm_C.shape
    d_ml = heads * dv
    n_groups, per_group = w_er.shape[1], w_er.shape[3]
    n_experts = w1.shape[1]
    t_p, t_s = bp * seq, bs * dec_seq
    t = t_p + t_s
    n_main = 2 * d_rg + 2 * heads * dk + 2 * d_ml
    q_off, k_off, v_off, o_off = 2 * d_rg, 2 * d_rg + heads * dk, 2 * d_rg + 2 * heads * dk, n_main - d_ml
    assert dec_seq == SUBLANES and seq % LANES == 0 and conv_width - 1 <= SUBLANES
    assert 2 * heads <= LANES and n_groups * (per_group + 1) <= LANES
    assert q_off % (heads * dk) == 0 and k_off % (heads * dk) == 0 and v_off % d_ml == 0 and o_off % d_ml == 0

    tok = _Tokens(t_p, t_s, seq, _pick_tile(np.gcd(seq, t_s), 256))
    tok_out = _Tokens(t_p, t_s, seq, _pick_tile(np.gcd(seq, t_s), 512))
    tm_in = _pick_tile(np.gcd(t_p, t_s), 1024)
    tc = LANES
    assert t_s % tc == 0 and seq % tc == 0

    mc = -(-(bp + bs) // SUBLANES) * SUBLANES
    c_all = jnp.zeros((mc, d), F32).at[:bp].set(c_prompt).at[bp:bp + bs].set(c_sample)
    mod = _mod_call(c_all, w_mod, b_mod, _pick_tile(6 * d, 1024))
    mod_p = mod[:, :bp].reshape(depth, bp, 1, 6 * d)
    mod_s = mod[:, bp:bp + bs].reshape(depth, bs, 1, 6 * d)

    w_in_t = jnp.swapaxes(w_in, 1, 2)
    w_gate = jnp.zeros((depth, d, LANES), F32).at[:, :, :2 * heads].set(w_in[:, :, n_main:]).astype(BF16)
    b_gate = jnp.zeros((depth, 1, LANES), F32).at[:, 0, :heads].set(b_ig).at[:, 0, heads:2 * heads].set(b_fg)
    w_ra_b, w_ri_b = w_ra.astype(BF16), w_ri.astype(BF16)
    n_rt = n_groups * (per_group + 1)
    w_router = jnp.concatenate([w_grp, jnp.moveaxis(w_er, 1, 2).reshape(depth, d, n_groups * per_group)], axis=-1)
    w_router = jnp.zeros((depth, d, LANES), F32).at[:, :, :n_rt].set(w_router).astype(BF16)
    b_router = jnp.concatenate([b_grp, b_er.reshape(depth, -1)], axis=-1)
    b_router = jnp.zeros((depth, 1, LANES), F32).at[:, 0, :n_rt].set(b_router)
    vec = lambda p: p.reshape(depth, 1, -1)

    pad_rows = SUBLANES - (conv_width - 1)
    prev_s = jnp.pad(state_rg_conv, ((0, 0), (0, 0), (pad_rows, 0), (0, 0))).reshape(depth, bs * SUBLANES, d_rg)
    h0_s = state_rg_h.reshape(depth, bs, 1, d_rg)
    n_s = state_mlstm_n.reshape(depth, bs, heads, 1, dk)
    m_s_col = jnp.pad(jnp.repeat(state_mlstm_m, dec_seq, axis=1), ((0, 0), (0, 0), (0, LANES - heads)))

    x = jnp.concatenate([x_prompt.reshape(t_p, d), x_sample.reshape(t_s, d)], axis=0)
    h, gcol = _prenorm_call(x, mod_p, mod_s, w_gate, b_gate, heads, tok)
    outs_p = [[] for _ in range(5)]
    outs_s = [[] for _ in range(5)]
    cs_stack = jnp.zeros_like(state_mlstm_C)
    for l in range(depth):
        u = _in_call(h, w_in_t, l, n_main, tm_in, _pick_tile(n_main, 1024))
        hr, mix_rg = _rg_call(u, prev_s, h0_s, conv_w, vec(conv_b), w_ra_b, vec(b_ra), w_ri_b, vec(b_ri),
                              vec(lam), vec(g_rg), l, tok)
        grow = gcol[:, :2 * heads].T
        mix_ml, c_p, n_p, m_p, cs_stack, n_so, m_so = _ml_call(
            u, gcol, grow, state_mlstm_C, n_s, m_s_col, vec(g_ml), cs_stack, l, t_p, seq, dec_seq,
            q_off, k_off, v_off, o_off, tc)
        xn, h2, eid, wts = _out_call(x, mix_rg, mix_ml, mod_p, mod_s, w_out, w_router, b_router, l,
                                     n_groups, per_group, tok_out)
        tile_exp, next_exp, n_used, src_tok, dest = _route_plan(eid, n_experts, tok.tile)
        y = _moe_call(tile_exp, next_exp, n_used, src_tok, h2, w1, w3, w2, l, tok.tile, *EXPERT_DMA_PLAN[l % 4])
        final = l == depth - 1
        res = _comb_call(dest, xn, wts, mod_p, mod_s, w_gate, b_gate, g_final[None], y, l, heads, tok, final)
        if final:
            y_p, y_s = res
        else:
            x, h, gcol = res

        tail = conv_width - 1
        outs_p[0].append(jnp.stack([u[(b + 1) * seq - tail:(b + 1) * seq, :d_rg] for b in range(bp)]))
        outs_p[1].append(jnp.stack([hr[(b + 1) * seq - 1] for b in range(bp)]))
        outs_p[2].append(c_p)
        outs_p[3].append(n_p.reshape(bp, heads, dk))
        outs_p[4].append(m_p[:, 0, :heads])
        outs_s[0].append(u[t_p:, :d_rg].reshape(bs, dec_seq, d_rg)[:, dec_seq - tail:])
        outs_s[1].append(hr[t_p:].reshape(bs, dec_seq, d_rg)[:, -1])
        outs_s[3].append(n_so.reshape(bs, heads, dk))
        outs_s[4].append(m_so.reshape(bs, dec_seq, LANES)[:, -1, :heads])

    stack = lambda parts: jnp.stack(parts)
    return (y_p.reshape(bp, seq, d), y_s.reshape(bs, dec_seq, d), *[stack(o) for o in outs_p],
            stack(outs_s[0]), stack(outs_s[1]), cs_stack, stack(outs_s[3]), stack(outs_s[4]))
```

```python
import functools

import numpy as np
import jax
import jax.numpy as jnp
from jax import lax
from jax.experimental import pallas as pl
from jax.experimental.pallas import tpu as pltpu

F32 = jnp.float32
BF16 = jnp.bfloat16
EPS = 1e-6
RG_C = 8.0
SUBLANES = 8
LANES = 128
VMEM_LIMIT = 56 * 1024 * 1024
HI = lax.Precision.HIGHEST
INVERT_CLEAR_UNROLL = 32
SH_A, SC_A, GT_A, SH_F, SC_F, GT_F = range(6)


def _cparams(*sem):
    return pltpu.CompilerParams(dimension_semantics=sem, vmem_limit_bytes=VMEM_LIMIT)


def _softplus(z):
    return jnp.maximum(z, 0.0) + jnp.log1p(jnp.exp(-jnp.abs(z)))


def _rows3(x):
    r, c = x.shape
    return x.reshape(r // SUBLANES, SUBLANES, c)


class _Tokens:
    def __init__(self, t_prompt, t_sample, seq, tile):
        self.tile = tile
        self.groups = tile // SUBLANES
        self.n_prompt_tiles = t_prompt // tile
        self.tiles_per_seq = seq // tile
        self.n_prompt_seqs = t_prompt // seq

    def prompt_seq(self, i):
        return jnp.minimum(i // self.tiles_per_seq, self.n_prompt_seqs - 1)

    def sample_tile(self, i):
        return jnp.maximum(i - self.n_prompt_tiles, 0)

    def mod_specs(self, layer, chunk, d, tile_axis=0):
        def prompt_map(*idx):
            return (layer, self.prompt_seq(idx[tile_axis]), 0, chunk)

        def sample_map(*idx):
            return (layer, self.sample_tile(idx[tile_axis]), 0, chunk)

        return [pl.BlockSpec((1, 1, 1, d), prompt_map), pl.BlockSpec((1, self.groups, 1, d), sample_map)]


def _pick_mod(is_prompt, mp_ref, ms_ref):
    return jnp.where(is_prompt, mp_ref[0], ms_ref[0])


def _mod_kernel(c_ref, w_ref, b_ref, o_ref):
    c = c_ref[...]
    s = (c * jax.nn.sigmoid(c)).astype(BF16)
    o_ref[0] = jnp.dot(s, w_ref[0].astype(BF16), preferred_element_type=F32) + b_ref[0]


def _mod_call(c_all, w_mod, b_mod, tn):
    depth, d, n = w_mod.shape
    mc = c_all.shape[0]
    return pl.pallas_call(
        _mod_kernel,
        grid=(depth, n // tn),
        in_specs=[
            pl.BlockSpec((mc, d), lambda l, j: (0, 0)),
            pl.BlockSpec((1, d, tn), lambda l, j: (l, 0, j)),
            pl.BlockSpec((1, 1, tn), lambda l, j: (l, 0, j)),
        ],
        out_specs=pl.BlockSpec((1, mc, tn), lambda l, j: (l, 0, j)),
        out_shape=jax.ShapeDtypeStruct((depth, mc, n), F32),
        compiler_params=_cparams("arbitrary", "arbitrary"),
        name="mod",
    )(c_all, w_mod, b_mod.reshape(depth, 1, n))


def _norm_gates(x3, sc, sh, wg_ref, bg_ref, h_ref, g_ref, heads):
    groups, _, d = x3.shape
    xn = x3 * lax.rsqrt(jnp.mean(x3 * x3, axis=-1, keepdims=True) + EPS)
    hb = (xn * (1.0 + sc) + sh).reshape(groups * SUBLANES, d).astype(BF16)
    h_ref[...] = hb
    g = jnp.dot(hb, wg_ref[0], preferred_element_type=F32) + bg_ref[0]
    lane = lax.broadcasted_iota(jnp.int32, g.shape, 1)
    is_forget = (lane >= heads) & (lane < 2 * heads)
    g_ref[...] = jnp.where(is_forget, -_softplus(-g), g)


def _prenorm_kernel(x_ref, shp_ref, shs_ref, scp_ref, scs_ref, wg_ref, bg_ref, h_ref, g_ref, *, heads, n_prompt_tiles):
    is_prompt = pl.program_id(0) < n_prompt_tiles
    _norm_gates(_rows3(x_ref[...]), _pick_mod(is_prompt, scp_ref, scs_ref), _pick_mod(is_prompt, shp_ref, shs_ref),
                wg_ref, bg_ref, h_ref, g_ref, heads)


def _prenorm_call(x, mod_p, mod_s, w_gate, b_gate, heads, tok):
    t, d = x.shape
    tm = tok.tile
    return pl.pallas_call(
        functools.partial(_prenorm_kernel, heads=heads, n_prompt_tiles=tok.n_prompt_tiles),
        grid=(t // tm,),
        in_specs=[
            pl.BlockSpec((tm, d), lambda i: (i, 0)),
            *tok.mod_specs(0, SH_A, d),
            *tok.mod_specs(0, SC_A, d),
            pl.BlockSpec((1, d, LANES), lambda i: (0, 0, 0)),
            pl.BlockSpec((1, 1, LANES), lambda i: (0, 0, 0)),
        ],
        out_specs=[
            pl.BlockSpec((tm, d), lambda i: (i, 0)),
            pl.BlockSpec((tm, LANES), lambda i: (i, 0)),
        ],
        out_shape=[
            jax.ShapeDtypeStruct((t, d), BF16),
            jax.ShapeDtypeStruct((t, LANES), F32),
        ],
        compiler_params=_cparams("arbitrary"),
        name="prenorm",
    )(x, mod_p, mod_s, mod_p, mod_s, w_gate, b_gate)


def _in_kernel(h_ref, w_ref, u_ref, wb_scr):
    @pl.when(pl.program_id(1) == 0)
    def _():
        wb_scr[...] = w_ref[0].astype(BF16)

    u_ref[...] = lax.dot_general(h_ref[...], wb_scr[...], (((1,), (1,)), ((), ())), preferred_element_type=F32)


def _in_call(h, w_in_t, layer, n_main, tm, tn):
    t, d = h.shape
    return pl.pallas_call(
        _in_kernel,
        grid=(n_main // tn, t // tm),
        in_specs=[
            pl.BlockSpec((tm, d), lambda j, i: (i, 0)),
            pl.BlockSpec((1, tn, d), lambda j, i: (layer, j, 0)),
        ],
        out_specs=pl.BlockSpec((tm, tn), lambda j, i: (i, j)),
        out_shape=jax.ShapeDtypeStruct((t, n_main), F32),
        scratch_shapes=[pltpu.VMEM((tn, d), BF16)],
        compiler_params=_cparams("arbitrary", "arbitrary"),
        name="in_proj",
    )(h, w_in_t)


def _rg_kernel(xr_ref, yr_ref, prev_ref, h0_ref, cw_ref, cb_ref, wra_ref, bra_ref, wri_ref, bri_ref,
               lam_ref, gain_ref, hr_ref, mix_ref, xprev_scr, hprev_scr, hstart_scr, a_scr, b_scr,
               *, n_prompt_tiles, tiles_per_seq):
    i = pl.program_id(0)
    tl, c = xr_ref.shape
    groups = tl // SUBLANES
    _, nblk, bw, _ = wra_ref.shape
    is_prompt = i < n_prompt_tiles

    @pl.when(is_prompt & (i % tiles_per_seq == 0))
    def _():
        xprev_scr[...] = jnp.zeros_like(xprev_scr)
        hprev_scr[...] = jnp.zeros_like(hprev_scr)

    x = xr_ref[...]
    prev_prompt = jnp.concatenate([xprev_scr[...], x[:tl - SUBLANES]], axis=0)
    prev = jnp.where(is_prompt, prev_prompt, prev_ref[0])
    xprev_scr[...] = x[tl - SUBLANES:]

    x3 = _rows3(x)
    p3 = _rows3(prev)
    row = lax.broadcasted_iota(jnp.int32, x3.shape, 1)
    cw = cw_ref[0]
    conv_w = cw.shape[0]
    xc = cb_ref[0] + x3 * cw[conv_w - 1:conv_w]
    for dlt in range(1, conv_w):
        shifted = jnp.where(row >= dlt, pltpu.roll(x3, dlt, 1), pltpu.roll(p3, dlt, 1))
        xc = xc + shifted * cw[conv_w - 1 - dlt:conv_w - dlt]

    xc2 = xc.reshape(tl, c)
    xcb = xc2.astype(BF16)
    r_parts, i_parts = [], []
    for nb in range(nblk):
        blk = xcb[:, nb * bw:(nb + 1) * bw]
        r_parts.append(jnp.dot(blk, wra_ref[0, nb], preferred_element_type=F32))
        i_parts.append(jnp.dot(blk, wri_ref[0, nb], preferred_element_type=F32))
    r = jax.nn.sigmoid(jnp.concatenate(r_parts, axis=-1) + bra_ref[0])
    ig = jax.nn.sigmoid(jnp.concatenate(i_parts, axis=-1) + bri_ref[0])
    log_a = (-RG_C) * r * _softplus(-lam_ref[0])
    a = jnp.exp(log_a)
    th = jnp.tanh(log_a)
    mult = jnp.sqrt(-2.0 * th / (1.0 - th))
    bx = mult * (ig * xc2)

    av = _rows3(a)
    bv = _rows3(bx)
    for s in (1, 2, 4):
        a_sh = jnp.where(row >= s, pltpu.roll(av, s, 1), 1.0)
        b_sh = jnp.where(row >= s, pltpu.roll(bv, s, 1), 0.0)
        bv = av * b_sh + bv
        av = av * a_sh

    @pl.when(is_prompt)
    def _():
        a_scr[...] = av.reshape(tl, c)
        b_scr[...] = bv.reshape(tl, c)

        def body(g, h):
            hstart_scr[g] = h
            last = g * SUBLANES + (SUBLANES - 1)
            return a_scr[pl.ds(last, 1), :] * h + b_scr[pl.ds(last, 1), :]

        hprev_scr[...] = lax.fori_loop(0, groups, body, hprev_scr[...])

    @pl.when(jnp.logical_not(is_prompt))
    def _():
        hstart_scr[...] = h0_ref[0]

    h3 = av * hstart_scr[...] + bv
    hr = h3.reshape(tl, c)
    hr_ref[...] = hr
    y = yr_ref[...]
    gelu = 0.5 * y * (1.0 + jnp.tanh(np.sqrt(2.0 / np.pi).astype(np.float32) * (y + 0.044715 * (y * y * y))))
    hn = hr * lax.rsqrt(jnp.mean(hr * hr, axis=-1, keepdims=True) + EPS)
    mix_ref[...] = (hn * gain_ref[0] * gelu).astype(BF16)


def _rg_call(u, prev_s, h0_s, conv_w, conv_b, w_ra, b_ra, w_ri, b_ri, lam, g_rg, layer, tok):
    t = u.shape[0]
    _, conv_width, c = conv_w.shape
    _, nblk, bw, _ = w_ra.shape
    tl, groups = tok.tile, tok.groups
    vec = pl.BlockSpec((1, 1, c), lambda i: (layer, 0, 0))
    return pl.pallas_call(
        functools.partial(_rg_kernel, n_prompt_tiles=tok.n_prompt_tiles, tiles_per_seq=tok.tiles_per_seq),
        grid=(t // tl,),
        in_specs=[
            pl.BlockSpec((tl, c), lambda i: (i, 0)),
            pl.BlockSpec((tl, c), lambda i: (i, 1)),
            pl.BlockSpec((1, tl, c), lambda i: (layer, tok.sample_tile(i), 0)),
            pl.BlockSpec((1, groups, 1, c), lambda i: (layer, tok.sample_tile(i), 0, 0)),
            pl.BlockSpec((1, conv_width, c), lambda i: (layer, 0, 0)),
            vec,
            pl.BlockSpec((1, nblk, bw, bw), lambda i: (layer, 0, 0, 0)),
            vec,
            pl.BlockSpec((1, nblk, bw, bw), lambda i: (layer, 0, 0, 0)),
            vec,
            vec,
            vec,
        ],
        out_specs=[
            pl.BlockSpec((tl, c), lambda i: (i, 0)),
            pl.BlockSpec((tl, c), lambda i: (i, 0)),
        ],
        out_shape=[
            jax.ShapeDtypeStruct((t, c), F32),
            jax.ShapeDtypeStruct((t, c), BF16),
        ],
        scratch_shapes=[
            pltpu.VMEM((SUBLANES, c), F32),
            pltpu.VMEM((1, c), F32),
            pltpu.VMEM((groups, 1, c), F32),
            pltpu.VMEM((tl, c), F32),
            pltpu.VMEM((tl, c), F32),
        ],
        compiler_params=_cparams("arbitrary"),
        name="rg_lru",
    )(u, u, prev_s, h0_s, conv_w, conv_b, w_ra, b_ra, w_ri, b_ri, lam, g_rg)


def _ml_chunk(nseq, heads, dk, dv, q_ref, k_ref, v_ref, o_ref, gc, gr, mprev, gain_ref, mix_ref,
              get_c, get_n, put_state):
    tc = q_ref.shape[0]
    ls = tc // nseq
    shift = int(np.log2(ls))
    t_idx = lax.broadcasted_iota(jnp.int32, (tc, tc), 0)
    s_idx = lax.broadcasted_iota(jnp.int32, (tc, tc), 1)
    mask = (lax.shift_right_logical(t_idx, shift) == lax.shift_right_logical(s_idx, shift)) & (s_idx <= t_idx)
    m_cum = mask.astype(F32)
    bcol = jnp.dot(m_cum, gc, precision=HI, preferred_element_type=F32)
    brow = lax.dot_general(gr, m_cum, (((1,), (1,)), ((), ())), precision=HI,
                           preferred_element_type=F32)
    lane = lax.broadcasted_iota(jnp.int32, (tc, LANES), 1)
    scale = np.float32(dk ** -0.5)

    def seg_last(col):
        if nseq == 1:
            return col[tc - 1:tc, :].reshape(1, 1, 1)
        return col.reshape(nseq, ls, 1)[:, ls - 1:ls, :]

    def seg_bcast(val):
        return jnp.broadcast_to(val, (nseq, ls, 1)).reshape(tc, 1)

    m_out = jnp.zeros((tc, LANES), F32)
    for h in range(heads):
        bc = bcol[:, heads + h:heads + h + 1]
        br = brow[heads + h:heads + h + 1, :]
        igr = gr[h:h + 1, :]
        igc = gc[:, h:h + 1]
        mp = mprev[:, h:h + 1]
        dmat = jnp.where(mask, bc - br + igr, -jnp.inf)
        m_inter = bc + mp
        m_t = jnp.maximum(m_inter, jnp.max(dmat, axis=-1, keepdims=True))
        p = jnp.exp(dmat - m_t)
        qh = q_ref[:, h * dk:(h + 1) * dk] * scale
        kh = k_ref[:, h * dk:(h + 1) * dk]
        vh = v_ref[:, h * dv:(h + 1) * dv]
        qb = qh.astype(BF16)
        kb = kh.astype(BF16)
        s = lax.dot_general(qb, kb, (((1,), (1,)), ((), ())), preferred_element_type=F32) * p
        w_inter = jnp.exp(m_inter - m_t)
        c_old = get_c(h)
        n_old = get_n(h)
        q3 = qh.reshape(nseq, ls, dk)
        inter = lax.dot_general(q3.astype(BF16), c_old.astype(BF16), (((2,), (1,)), ((0,), (0,))),
                                preferred_element_type=F32).reshape(tc, dv)
        qn = jnp.sum(q3 * n_old, axis=-1, keepdims=True).reshape(tc, 1)
        num = jnp.dot(s.astype(BF16), vh.astype(BF16), preferred_element_type=F32) + w_inter * inter
        den = jnp.sum(s, axis=-1, keepdims=True) + w_inter * qn
        hh = num / jnp.maximum(jnp.abs(den), jnp.exp(-m_t))
        hn = hh * lax.rsqrt(jnp.mean(hh * hh, axis=-1, keepdims=True) + EPS)
        og = jax.nn.sigmoid(o_ref[:, h * dv:(h + 1) * dv])
        mix_ref[:, h * dv:(h + 1) * dv] = (hn * gain_ref[0, :, h * dv:(h + 1) * dv] * og).astype(BF16)

        m_new = seg_last(m_t)
        b_last = seg_last(bc)
        wl = jnp.exp(seg_bcast(b_last) - bc + igc - seg_bcast(m_new))
        decay = jnp.exp(b_last + seg_last(mp) - m_new)
        wv3 = (wl * vh).reshape(nseq, ls, dv).astype(BF16)
        k3 = kh.reshape(nseq, ls, dk).astype(BF16)
        upd = lax.dot_general(k3, wv3, (((1,), (1,)), ((0,), (0,))), preferred_element_type=F32)
        c_new = decay * c_old + upd
        n_new = decay * n_old + jnp.sum((wl * kh).reshape(nseq, ls, dk), axis=1, keepdims=True)
        put_state(h, c_new, n_new)
        m_out = jnp.where(lane == h, seg_bcast(m_new), m_out)
    return m_out


def _ml_kernel(q_ref, k_ref, v_ref, o_ref, gc_ref, gr_ref, cs_ref, ns_ref, ms_ref, gain_ref, cs_alias,
               mix_ref, cp_out, np_out, mp_out, cs_out, ns_out, ms_out, c_scr, n_scr, m_scr,
               *, n_prompt_chunks, chunks_per_seq, heads, dk, dv, nseq_sample):
    del cs_alias
    i = pl.program_id(0)
    tc = q_ref.shape[0]
    is_prompt = i < n_prompt_chunks

    @pl.when(is_prompt & (i % chunks_per_seq == 0))
    def _():
        c_scr[...] = jnp.zeros_like(c_scr)
        n_scr[...] = jnp.zeros_like(n_scr)
        m_scr[...] = jnp.zeros_like(m_scr)

    @pl.when(is_prompt)
    def _():
        def put(h, c_new, n_new):
            c_scr[h] = c_new[0]
            n_scr[h] = n_new[0]

        mprev = jnp.broadcast_to(m_scr[...], (tc, LANES))
        m_out = _ml_chunk(1, heads, dk, dv, q_ref, k_ref, v_ref, o_ref, gc_ref[...], gr_ref[...], mprev,
                          gain_ref, mix_ref, lambda h: c_scr[h][None], lambda h: n_scr[h][None], put)
        m_scr[...] = m_out[:1, :]
        cp_out[0] = c_scr[...]
        np_out[0] = n_scr[...]
        mp_out[0] = m_out[:1, :]

    @pl.when(jnp.logical_not(is_prompt))
    def _():
        def put(h, c_new, n_new):
            cs_out[0, :, h] = c_new
            ns_out[:, h] = n_new

        m_out = _ml_chunk(nseq_sample, heads, dk, dv, q_ref, k_ref, v_ref, o_ref, gc_ref[...], gr_ref[...],
                          ms_ref[0], gain_ref, mix_ref, lambda h: cs_ref[0, :, h], lambda h: ns_ref[0, :, h], put)
        ms_out[...] = m_out


def _ml_call(u, gcol, grow, c_s, n_s, m_s_col, g_ml, cs_stack, layer, t_prompt, seq, dec_seq,
             q_off, k_off, v_off, o_off, tc):
    t = u.shape[0]
    depth, bs, heads, dk, dv = c_s.shape
    bp = t_prompt // seq
    hdk, hdv = heads * dk, heads * dv
    n_p = t_prompt // tc
    cps = seq // tc
    nseq = tc // dec_seq
    samp = lambda i: jnp.maximum(i - n_p, 0)
    pseq = lambda i: jnp.minimum(i // cps, bp - 1)
    t_s = t - t_prompt
    in_specs = [
        pl.BlockSpec((tc, hdk), lambda i: (i, q_off // hdk)),
        pl.BlockSpec((tc, hdk), lambda i: (i, k_off // hdk)),
        pl.BlockSpec((tc, hdv), lambda i: (i, v_off // hdv)),
        pl.BlockSpec((tc, hdv), lambda i: (i, o_off // hdv)),
        pl.BlockSpec((tc, LANES), lambda i: (i, 0)),
        pl.BlockSpec((2 * heads, tc), lambda i: (0, i)),
        pl.BlockSpec((1, nseq, heads, dk, dv), lambda i: (layer, samp(i), 0, 0, 0)),
        pl.BlockSpec((1, nseq, heads, 1, dk), lambda i: (layer, samp(i), 0, 0, 0)),
        pl.BlockSpec((1, tc, LANES), lambda i: (layer, samp(i), 0)),
        pl.BlockSpec((1, 1, hdv), lambda i: (layer, 0, 0)),
        pl.BlockSpec(memory_space=pl.ANY),
    ]
    args = [u, u, u, u, gcol, grow, c_s, n_s, m_s_col, g_ml, cs_stack]
    return pl.pallas_call(
        functools.partial(_ml_kernel, n_prompt_chunks=n_p, chunks_per_seq=cps, heads=heads, dk=dk, dv=dv,
                          nseq_sample=nseq),
        grid=(t // tc,),
        in_specs=in_specs,
        out_specs=[
            pl.BlockSpec((tc, hdv), lambda i: (i, 0)),
            pl.BlockSpec((1, heads, dk, dv), lambda i: (pseq(i), 0, 0, 0)),
            pl.BlockSpec((1, heads, 1, dk), lambda i: (pseq(i), 0, 0, 0)),
            pl.BlockSpec((1, 1, LANES), lambda i: (pseq(i), 0, 0)),
            pl.BlockSpec((1, nseq, heads, dk, dv), lambda i: (layer, samp(i), 0, 0, 0)),
            pl.BlockSpec((nseq, heads, 1, dk), lambda i: (samp(i), 0, 0, 0)),
            pl.BlockSpec((tc, LANES), lambda i: (samp(i), 0)),
        ],
        out_shape=[
            jax.ShapeDtypeStruct((t, hdv), BF16),
            jax.ShapeDtypeStruct((bp, heads, dk, dv), F32),
            jax.ShapeDtypeStruct((bp, heads, 1, dk), F32),
            jax.ShapeDtypeStruct((bp, 1, LANES), F32),
            jax.ShapeDtypeStruct((depth, bs, heads, dk, dv), F32),
            jax.ShapeDtypeStruct((bs, heads, 1, dk), F32),
            jax.ShapeDtypeStruct((t_s, LANES), F32),
        ],
        scratch_shapes=[
            pltpu.VMEM((heads, dk, dv), F32),
            pltpu.VMEM((heads, 1, dk), F32),
            pltpu.VMEM((1, LANES), F32),
        ],
        input_output_aliases={len(args) - 1: 4},
        compiler_params=_cparams("arbitrary"),
        name="mlstm",
    )(*args)


def _out_kernel(x_ref, mr_ref, mm_ref, gtp_ref, gts_ref, shp_ref, shs_ref, scp_ref, scs_ref, wo_hbm, wr_ref, br_ref,
                xn_ref, h2_ref, eid_ref, wt_ref, wo_scr, stage, sem, *, layer, n_groups, per_group, n_prompt_tiles):
    i = pl.program_id(0)
    tm, d = x_ref.shape
    c = mr_ref.shape[1]
    rows = stage.shape[0]

    @pl.when(i == 0)
    def _():
        for part in range(wo_scr.shape[0] // rows):
            cp = pltpu.make_async_copy(wo_hbm.at[layer, pl.ds(part * rows, rows)], stage, sem)
            cp.start()
            cp.wait()
            wo_scr[pl.ds(part * rows, rows), :] = stage[...].astype(BF16)

    is_prompt = i < n_prompt_tiles
    acc = jnp.dot(mr_ref[...], wo_scr[:c, :], preferred_element_type=F32)
    acc = acc + jnp.dot(mm_ref[...], wo_scr[c:, :], preferred_element_type=F32)
    xn3 = _rows3(x_ref[...]) + _pick_mod(is_prompt, gtp_ref, gts_ref) * _rows3(acc)
    rs = lax.rsqrt(jnp.mean(xn3 * xn3, axis=-1, keepdims=True) + EPS)
    h2 = (xn3 * rs * (1.0 + _pick_mod(is_prompt, scp_ref, scs_ref))
          + _pick_mod(is_prompt, shp_ref, shs_ref)).reshape(tm, d)
    xn_ref[...] = xn3.reshape(tm, d)
    _store_slabs(h2_ref, h2)

    logits = jnp.dot(h2.astype(BF16), wr_ref[0], preferred_element_type=F32) + br_ref[0]
    lane = lax.broadcasted_iota(jnp.int32, logits.shape, 1)
    lane_f = lane.astype(F32)
    big = np.float32(LANES)

    def first_max(vals):
        vmax = jnp.max(vals, axis=-1, keepdims=True)
        idx = jnp.min(jnp.where(vals == vmax, lane_f, big), axis=-1, keepdims=True)
        return vmax, idx.astype(jnp.int32)

    gl = jnp.where(lane < n_groups, logits, -jnp.inf)
    gmax, g_sel = first_max(gl)
    p_g = 1.0 / jnp.sum(jnp.exp(gl - gmax), axis=-1, keepdims=True)
    lo = n_groups + g_sel * per_group
    el = jnp.where((lane >= lo) & (lane < lo + per_group), logits, -jnp.inf)
    v1, i1 = first_max(el)
    v2, i2 = first_max(jnp.where(lane == i1, -jnp.inf, el))
    e21 = jnp.exp(v2 - v1)
    w1 = 1.0 / (1.0 + e21)
    w2 = e21 * w1
    eid_ref[...] = jnp.where(lane == 0, i1 - n_groups, jnp.where(lane == 1, i2 - n_groups, 0))
    wt_ref[...] = jnp.where(lane == 0, w1 * p_g, jnp.where(lane == 1, w2 * p_g, 0.0))


def _out_call(x, mix_rg, mix_ml, mod_p, mod_s, w_out, w_router, b_router, layer, n_groups, per_group, tok):
    t, d = x.shape
    c = mix_rg.shape[1]
    cm = mix_ml.shape[1]
    tm = tok.tile
    stage_rows = min(256, c + cm)
    return pl.pallas_call(
        functools.partial(_out_kernel, layer=layer, n_groups=n_groups, per_group=per_group,
                          n_prompt_tiles=tok.n_prompt_tiles),
        grid=(t // tm,),
        in_specs=[
            pl.BlockSpec((tm, d), lambda i: (i, 0)),
            pl.BlockSpec((tm, c), lambda i: (i, 0)),
            pl.BlockSpec((tm, cm), lambda i: (i, 0)),
            *tok.mod_specs(layer, GT_A, d),
            *tok.mod_specs(layer, SH_F, d),
            *tok.mod_specs(layer, SC_F, d),
            pl.BlockSpec(memory_space=pl.ANY),
            pl.BlockSpec((1, d, LANES), lambda i: (layer, 0, 0)),
            pl.BlockSpec((1, 1, LANES), lambda i: (layer, 0, 0)),
        ],
        out_specs=[
            pl.BlockSpec((tm, d), lambda i: (i, 0)),
            pl.BlockSpec((tm * _slab_rows(d), LANES), lambda i: (i, 0)),
            pl.BlockSpec((tm, LANES), lambda i: (i, 0)),
            pl.BlockSpec((tm, LANES), lambda i: (i, 0)),
        ],
        out_shape=[
            jax.ShapeDtypeStruct((t, d), F32),
            jax.ShapeDtypeStruct((t * _slab_rows(d), LANES), F32),
            jax.ShapeDtypeStruct((t, LANES), jnp.int32),
            jax.ShapeDtypeStruct((t, LANES), F32),
        ],
        scratch_shapes=[
            pltpu.VMEM((c + cm, d), BF16),
            pltpu.VMEM((stage_rows, d), F32),
            pltpu.SemaphoreType.DMA(()),
        ],
        compiler_params=_cparams("arbitrary"),
        name="out_proj_router",
    )(x, mix_rg, mix_ml, mod_p, mod_s, mod_p, mod_s, mod_p, mod_s, w_out, w_router, b_router)


def _slab_rows(d):
    return d // LANES


def _store_slabs(ref, val):
    rows, d = val.shape
    slab = _slab_rows(d)
    for s in range(slab):
        ref[pl.ds(s, rows, stride=slab), :] = val[:, s * LANES:(s + 1) * LANES]


def _load_slabs(ref, rows, d):
    slab = _slab_rows(d)
    return jnp.concatenate([ref[pl.ds(s, rows, stride=slab), :] for s in range(slab)], axis=-1)


def _gather_slabs(idx_ref, base, n_rows, src_hbm, dst, sem, slab, stride=1, offset=0):
    def body(g, carry):
        for s in range(SUBLANES):
            r = g * SUBLANES + s
            tok = idx_ref[base + r * stride + offset]
            pltpu.make_async_copy(src_hbm.at[pl.ds(pl.multiple_of(tok * slab, slab), slab)],
                                  dst.at[pl.ds(pl.multiple_of(r * slab, slab), slab)], sem).start()
        return carry

    lax.fori_loop(0, n_rows // SUBLANES, body, 0)


def _wait_rows(buf, sem):
    pltpu.make_async_copy(buf, buf, sem).wait()


def _moe_kernel(te_ref, nx_ref, nu_ref, src_ref, h2_hbm, w1_hbm, w3_hbm, w2_hbm, y_ref,
                xbuf, st1, st3, st2, w1_scr, w3_scr, w2_scr, slot_ref, gsem, wsem,
                *, layer):
    j = pl.program_id(0)
    nbuf = xbuf.shape[0]
    d = w1_scr.shape[0]
    slab = _slab_rows(d)
    tm = xbuf.shape[1] // slab
    n_used = nu_ref[0]

    def weight_copies(expert, slot):
        return [pltpu.make_async_copy(w_hbm.at[layer, expert], stage.at[slot], wsem.at[slot])
                for w_hbm, stage in ((w1_hbm, st1), (w3_hbm, st3), (w2_hbm, st2))]

    def issue(tile):
        _gather_slabs(src_ref, tile * tm, tm, h2_hbm, xbuf.at[tile % nbuf], gsem.at[tile % nbuf], slab)

    @pl.when(j == 0)
    def _():
        slot_ref[0] = 1
        for cp in weight_copies(te_ref[0], 0):
            cp.start()
        for ahead in range(nbuf - 1):
            @pl.when(ahead < n_used)
            def _():
                issue(ahead)

    @pl.when(j + (nbuf - 1) < n_used)
    def _():
        issue(j + (nbuf - 1))

    @pl.when((j < n_used) & ((j == 0) | (te_ref[j] != te_ref[jnp.maximum(j - 1, 0)])))
    def _():
        slot = 1 - slot_ref[0]
        slot_ref[0] = slot
        for cp in weight_copies(te_ref[j], slot):
            cp.wait()
        w1_scr[...] = st1[slot].astype(BF16)
        w3_scr[...] = st3[slot].astype(BF16)
        w2_scr[...] = st2[slot].astype(BF16)

        @pl.when(nx_ref[j] >= 0)
        def _():
            for cp in weight_copies(nx_ref[j], 1 - slot):
                cp.start()

    @pl.when(j < n_used)
    def _():
        slot = j % nbuf
        _wait_rows(xbuf.at[slot], gsem.at[slot])
        xb = _load_slabs(xbuf.at[slot], tm, d).astype(BF16)
        a = jnp.dot(xb, w1_scr[...], preferred_element_type=F32)
        b = jnp.dot(xb, w3_scr[...], preferred_element_type=F32)
        hid = (a * jax.nn.sigmoid(a)) * b
        _store_slabs(y_ref, jnp.dot(hid.astype(BF16), w2_scr[...], preferred_element_type=F32))

    @pl.when(j >= n_used)
    def _():
        y_ref[...] = jnp.zeros_like(y_ref)


def _moe_call(tile_exp, next_exp, n_used, src_tok, h2_slabs, w1, w3, w2, layer, tm):
    _, ne, d, f = w1.shape
    slab = _slab_rows(d)
    n_tiles = tile_exp.shape[0]
    gather_bufs = 3
    grid_spec = pltpu.PrefetchScalarGridSpec(
        num_scalar_prefetch=4,
        grid=(n_tiles,),
        in_specs=[pl.BlockSpec(memory_space=pl.ANY)] * 4,
        out_specs=pl.BlockSpec((tm * slab, LANES), lambda j, *_: (j, 0)),
        scratch_shapes=[
            pltpu.VMEM((gather_bufs, tm * slab, LANES), F32),
            pltpu.VMEM((2, d, f), F32),
            pltpu.VMEM((2, d, f), F32),
            pltpu.VMEM((2, f, d), F32),
            pltpu.VMEM((d, f), BF16),
            pltpu.VMEM((d, f), BF16),
            pltpu.VMEM((f, d), BF16),
            pltpu.SMEM((1,), jnp.int32),
            pltpu.SemaphoreType.DMA((gather_bufs,)),
            pltpu.SemaphoreType.DMA((2,)),
        ],
    )
    return pl.pallas_call(
        functools.partial(_moe_kernel, layer=layer),
        grid_spec=grid_spec,
        out_shape=jax.ShapeDtypeStruct((n_tiles * tm * slab, LANES), F32),
        compiler_params=_cparams("arbitrary"),
        name="experts",
    )(tile_exp, next_exp, n_used, src_tok, h2_slabs, w1, w3, w2)


def _comb_kernel(dest_ref, xn_ref, wt_ref, gtp_ref, gts_ref, *rest, final, heads, n_prompt_tiles):
    if final:
        gf_ref, y_hbm, yp_ref, ys_ref, ybuf, sem = rest
    else:
        shp_ref, shs_ref, scp_ref, scs_ref, wg_ref, bg_ref, y_hbm, xo_ref, h_ref, g_ref, ybuf, sem = rest
    i = pl.program_id(0)
    n = pl.num_programs(0)
    tm, d = xn_ref.shape
    is_prompt = i < n_prompt_tiles

    def issue(tile, slot):
        for k in range(2):
            _gather_slabs(dest_ref, tile * tm * 2, tm, y_hbm, ybuf.at[slot, k], sem.at[slot], _slab_rows(d),
                          stride=2, offset=k)

    @pl.when(i == 0)
    def _():
        issue(0, 0)

    @pl.when(i + 1 < n)
    def _():
        issue(i + 1, (i + 1) % 2)

    slot = i % 2
    for k in range(2):
        _wait_rows(ybuf.at[slot, k], sem.at[slot])
    wt = wt_ref[...]
    y = wt[:, 0:1] * _load_slabs(ybuf.at[slot, 0], tm, d) + wt[:, 1:2] * _load_slabs(ybuf.at[slot, 1], tm, d)
    xo3 = _rows3(xn_ref[...]) + _pick_mod(is_prompt, gtp_ref, gts_ref) * _rows3(y)
    if final:
        xo = xo3.reshape(tm, d)
        yo = xo * lax.rsqrt(jnp.mean(xo * xo, axis=-1, keepdims=True) + EPS) * gf_ref[...]

        @pl.when(is_prompt)
        def _():
            yp_ref[...] = yo

        @pl.when(jnp.logical_not(is_prompt))
        def _():
            ys_ref[...] = yo
    else:
        xo_ref[...] = xo3.reshape(tm, d)
        _norm_gates(xo3, _pick_mod(is_prompt, scp_ref, scs_ref), _pick_mod(is_prompt, shp_ref, shs_ref),
                    wg_ref, bg_ref, h_ref, g_ref, heads)


def _comb_call(dest, xn, wts, mod_p, mod_s, w_gate, b_gate, g_final, y, layer, heads, tok, final):
    t, d = xn.shape
    tm = tok.tile
    full = lambda i, dst: (0, 0)
    in_specs = [
        pl.BlockSpec((tm, d), lambda i, dst: (i, 0)),
        pl.BlockSpec((tm, LANES), lambda i, dst: (i, 0)),
        *tok.mod_specs(layer, GT_F, d),
    ]
    args = [xn, wts, mod_p, mod_s]
    if final:
        n_p = tok.n_prompt_tiles
        in_specs.append(pl.BlockSpec((1, d), full))
        args.append(g_final)
        out_specs = [pl.BlockSpec((tm, d), lambda i, dst: (jnp.minimum(i, n_p - 1), 0)),
                     pl.BlockSpec((tm, d), lambda i, dst: (tok.sample_tile(i), 0))]
        out_shape = [jax.ShapeDtypeStruct((n_p * tm, d), F32), jax.ShapeDtypeStruct((t - n_p * tm, d), F32)]
    else:
        out_specs = [pl.BlockSpec((tm, d), lambda i, dst: (i, 0))]
        out_shape = [jax.ShapeDtypeStruct((t, d), F32)]
        nxt = layer + 1
        in_specs += [
            *tok.mod_specs(nxt, SH_A, d),
            *tok.mod_specs(nxt, SC_A, d),
            pl.BlockSpec((1, d, LANES), lambda i, dst: (nxt, 0, 0)),
            pl.BlockSpec((1, 1, LANES), lambda i, dst: (nxt, 0, 0)),
        ]
        args += [mod_p, mod_s, mod_p, mod_s, w_gate, b_gate]
        out_specs += [pl.BlockSpec((tm, d), lambda i, dst: (i, 0)), pl.BlockSpec((tm, LANES), lambda i, dst: (i, 0))]
        out_shape += [jax.ShapeDtypeStruct((t, d), BF16), jax.ShapeDtypeStruct((t, LANES), F32)]
    in_specs.append(pl.BlockSpec(memory_space=pl.ANY))
    args.append(y)
    grid_spec = pltpu.PrefetchScalarGridSpec(
        num_scalar_prefetch=1,
        grid=(t // tm,),
        in_specs=in_specs,
        out_specs=out_specs,
        scratch_shapes=[pltpu.VMEM((2, 2, tm * _slab_rows(d), LANES), F32), pltpu.SemaphoreType.DMA((2,))],
    )
    return pl.pallas_call(
        functools.partial(_comb_kernel, final=final, heads=heads, n_prompt_tiles=tok.n_prompt_tiles),
        grid_spec=grid_spec,
        out_shape=out_shape,
        compiler_params=_cparams("arbitrary"),
        name="combine",
    )(dest, *args)


def _invert_kernel(dest_ref, src_ref):
    n_pairs = dest_ref.shape[0]
    n_rows = src_ref.shape[0]

    def clear(i, carry):
        for u in range(INVERT_CLEAR_UNROLL):
            src_ref[i * INVERT_CLEAR_UNROLL + u] = 0
        return carry

    lax.fori_loop(0, n_rows // INVERT_CLEAR_UNROLL, clear, 0)

    def place(i, carry):
        for u in range(SUBLANES):
            pair = i * SUBLANES + u
            src_ref[dest_ref[pair]] = lax.shift_right_logical(pair, 1)
        return carry

    lax.fori_loop(0, n_pairs // SUBLANES, place, 0)


def _invert_call(dest, n_rows):
    assert dest.shape[0] % SUBLANES == 0 and n_rows % INVERT_CLEAR_UNROLL == 0
    return pl.pallas_call(
        _invert_kernel,
        in_specs=[pl.BlockSpec(memory_space=pltpu.SMEM)],
        out_specs=pl.BlockSpec(memory_space=pltpu.SMEM),
        out_shape=jax.ShapeDtypeStruct((n_rows,), jnp.int32),
        name="route_invert",
    )(dest)


def _route_plan(eid, n_experts, tm):
    t = eid.shape[0]
    e_flat = eid[:, :2].reshape(-1)
    n_pairs = 2 * t
    n_tiles = -(-n_pairs // tm) + n_experts
    onehot = (e_flat[:, None] == jnp.arange(n_experts, dtype=jnp.int32)[None, :]).astype(jnp.int32)
    csum = jnp.cumsum(onehot, axis=0)
    rank = jnp.sum(csum * onehot, axis=1) - 1
    counts = csum[-1]
    padded = ((counts + tm - 1) // tm) * tm
    ends = jnp.cumsum(padded)
    starts = ends - padded
    dest = (jnp.sum(starts[None, :] * onehot, axis=1) + rank).astype(jnp.int32)
    src_tok = _invert_call(dest, n_tiles * tm)
    n_used = (ends[-1] // tm).astype(jnp.int32)
    tile_id = jnp.minimum(jnp.arange(n_tiles, dtype=jnp.int32), n_used - 1)
    tile_exp = jnp.sum((ends[None, :] <= (tile_id * tm)[:, None]).astype(jnp.int32), axis=1)
    tile_exp = jnp.minimum(tile_exp, n_experts - 1).astype(jnp.int32)
    ids = jnp.arange(n_experts, dtype=jnp.int32)
    later = (ids[None, :] > ids[:, None]) & (padded[None, :] > 0)
    following = jnp.min(jnp.where(later, ids[None, :], n_experts), axis=1)
    following = jnp.where(following == n_experts, -1, following).astype(jnp.int32)
    return tile_exp, following[tile_exp], n_used.reshape(1), src_tok, dest


def _pick_tile(total, pref):
    tile = min(pref, total)
    while total % tile:
        tile //= 2
    return tile


def kernel(x_prompt, x_sample, c_prompt, c_sample, state_rg_conv, state_rg_h, state_mlstm_C, state_mlstm_n,
           state_mlstm_m, w_mod, b_mod, w_in, conv_w, conv_b, w_ra, b_ra, w_ri, b_ri, lam, g_rg, b_ig, b_fg,
           g_ml, w_out, w_grp, b_grp, w_er, b_er, w1, w3, w2, g_final):
    bp, seq, d = x_prompt.shape
    bs, dec_seq, _ = x_sample.shape
    depth = w_mod.shape[0]
    d_rg = conv_w.shape[2]
    conv_width = conv_w.shape[1]
    _, _, heads, dk, dv = state_mlstm_C.shape
    d_ml = heads * dv
    n_groups, per_group = w_er.shape[1], w_er.shape[3]
    n_experts = w1.shape[1]
    t_p, t_s = bp * seq, bs * dec_seq
    t = t_p + t_s
    n_main = 2 * d_rg + 2 * heads * dk + 2 * d_ml
    q_off, k_off, v_off, o_off = 2 * d_rg, 2 * d_rg + heads * dk, 2 * d_rg + 2 * heads * dk, n_main - d_ml
    assert dec_seq == SUBLANES and seq % LANES == 0 and conv_width - 1 <= SUBLANES
    assert 2 * heads <= LANES and n_groups * (per_group + 1) <= LANES
    assert q_off % (heads * dk) == 0 and k_off % (heads * dk) == 0 and v_off % d_ml == 0 and o_off % d_ml == 0

    tok = _Tokens(t_p, t_s, seq, _pick_tile(np.gcd(seq, t_s), 256))
    tok_out = _Tokens(t_p, t_s, seq, _pick_tile(np.gcd(seq, t_s), 512))
    tm_in = _pick_tile(np.gcd(t_p, t_s), 1024)
    tc = LANES
    assert t_s % tc == 0 and seq % tc == 0

    mc = -(-(bp + bs) // SUBLANES) * SUBLANES
    c_all = jnp.zeros((mc, d), F32).at[:bp].set(c_prompt).at[bp:bp + bs].set(c_sample)
    mod = _mod_call(c_all, w_mod, b_mod, _pick_tile(6 * d, 1024))
    mod_p = mod[:, :bp].reshape(depth, bp, 1, 6 * d)
    mod_s = mod[:, bp:bp + bs].reshape(depth, bs, 1, 6 * d)

    w_in_t = jnp.swapaxes(w_in, 1, 2)
    w_gate = jnp.zeros((depth, d, LANES), F32).at[:, :, :2 * heads].set(w_in[:, :, n_main:]).astype(BF16)
    b_gate = jnp.zeros((depth, 1, LANES), F32).at[:, 0, :heads].set(b_ig).at[:, 0, heads:2 * heads].set(b_fg)
    w_ra_b, w_ri_b = w_ra.astype(BF16), w_ri.astype(BF16)
    n_rt = n_groups * (per_group + 1)
    w_router = jnp.concatenate([w_grp, jnp.moveaxis(w_er, 1, 2).reshape(depth, d, n_groups * per_group)], axis=-1)
    w_router = jnp.zeros((depth, d, LANES), F32).at[:, :, :n_rt].set(w_router).astype(BF16)
    b_router = jnp.concatenate([b_grp, b_er.reshape(depth, -1)], axis=-1)
    b_router = jnp.zeros((depth, 1, LANES), F32).at[:, 0, :n_rt].set(b_router)
    vec = lambda p: p.reshape(depth, 1, -1)

    pad_rows = SUBLANES - (conv_width - 1)
    prev_s = jnp.pad(state_rg_conv, ((0, 0), (0, 0), (pad_rows, 0), (0, 0))).reshape(depth, bs * SUBLANES, d_rg)
    h0_s = state_rg_h.reshape(depth, bs, 1, d_rg)
    n_s = state_mlstm_n.reshape(depth, bs, heads, 1, dk)
    m_s_col = jnp.pad(jnp.repeat(state_mlstm_m, dec_seq, axis=1), ((0, 0), (0, 0), (0, LANES - heads)))

    x = jnp.concatenate([x_prompt.reshape(t_p, d), x_sample.reshape(t_s, d)], axis=0)
    h, gcol = _prenorm_call(x, mod_p, mod_s, w_gate, b_gate, heads, tok)
    outs_p = [[] for _ in range(5)]
    outs_s = [[] for _ in range(5)]
    cs_stack = jnp.zeros_like(state_mlstm_C)
    for l in range(depth):
        u = _in_call(h, w_in_t, l, n_main, tm_in, _pick_tile(n_main, 1024))
        hr, mix_rg = _rg_call(u, prev_s, h0_s, conv_w, vec(conv_b), w_ra_b, vec(b_ra), w_ri_b, vec(b_ri),
                              vec(lam), vec(g_rg), l, tok)
        grow = gcol[:, :2 * heads].T
        mix_ml, c_p, n_p, m_p, cs_stack, n_so, m_so = _ml_call(
            u, gcol, grow, state_mlstm_C, n_s, m_s_col, vec(g_ml), cs_stack, l, t_p, seq, dec_seq,
            q_off, k_off, v_off, o_off, tc)
        xn, h2, eid, wts = _out_call(x, mix_rg, mix_ml, mod_p, mod_s, w_out, w_router, b_router, l,
                                     n_groups, per_group, tok_out)
        tile_exp, next_exp, n_used, src_tok, dest = _route_plan(eid, n_experts, tok.tile)
        y = _moe_call(tile_exp, next_exp, n_used, src_tok, h2, w1, w3, w2, l, tok.tile)
        final = l == depth - 1
        res = _comb_call(dest, xn, wts, mod_p, mod_s, w_gate, b_gate, g_final[None], y, l, heads, tok, final)
        if final:
            y_p, y_s = res
        else:
            x, h, gcol = res

        tail = conv_width - 1
        outs_p[0].append(jnp.stack([u[(b + 1) * seq - tail:(b + 1) * seq, :d_rg] for b in range(bp)]))
        outs_p[1].append(jnp.stack([hr[(b + 1) * seq - 1] for b in range(bp)]))
        outs_p[2].append(c_p)
        outs_p[3].append(n_p.reshape(bp, heads, dk))
        outs_p[4].append(m_p[:, 0, :heads])
        outs_s[0].append(u[t_p:, :d_rg].reshape(bs, dec_seq, d_rg)[:, dec_seq - tail:])
        outs_s[1].append(hr[t_p:].reshape(bs, dec_seq, d_rg)[:, -1])
        outs_s[3].append(n_so.reshape(bs, heads, dk))
        outs_s[4].append(m_so.reshape(bs, dec_seq, LANES)[:, -1, :heads])

    stack = lambda parts: jnp.stack(parts)
    return (y_p.reshape(bp, seq, d), y_s.reshape(bs, dec_seq, d), *[stack(o) for o in outs_p],
            stack(outs_s[0]), stack(outs_s[1]), cs_stack, stack(outs_s[3]), stack(outs_s[4]))
```

```python
import functools

import numpy as np
import jax
import jax.numpy as jnp
from jax import lax
from jax.experimental import pallas as pl
from jax.experimental.pallas import tpu as pltpu

F32 = jnp.float32
BF16 = jnp.bfloat16
EPS = 1e-6
RG_C = 8.0
SUBLANES = 8
LANES = 128
VMEM_LIMIT = 56 * 1024 * 1024
HI = lax.Precision.HIGHEST
INVERT_CLEAR_UNROLL = 32
EXPERT_WEIGHT_DMA = ((1, False), (8, False), (8, True), (32, False))
SH_A, SC_A, GT_A, SH_F, SC_F, GT_F = range(6)


def _cparams(*sem):
    return pltpu.CompilerParams(dimension_semantics=sem, vmem_limit_bytes=VMEM_LIMIT)


def _sigmoid(z):
    return 0.5 * jnp.tanh(0.5 * z) + 0.5


def _softplus(z):
    return jnp.maximum(z, 0.0) + jnp.log1p(jnp.exp(-jnp.abs(z)))


def _rows3(x):
    r, c = x.shape
    return x.reshape(r // SUBLANES, SUBLANES, c)


class _Tokens:
    def __init__(self, t_prompt, t_sample, seq, tile):
        self.tile = tile
        self.groups = tile // SUBLANES
        self.n_prompt_tiles = t_prompt // tile
        self.tiles_per_seq = seq // tile
        self.n_prompt_seqs = t_prompt // seq

    def prompt_seq(self, i):
        return jnp.minimum(i // self.tiles_per_seq, self.n_prompt_seqs - 1)

    def sample_tile(self, i):
        return jnp.maximum(i - self.n_prompt_tiles, 0)

    def mod_specs(self, layer, chunk, d, tile_axis=0):
        def prompt_map(*idx):
            return (layer, self.prompt_seq(idx[tile_axis]), 0, chunk)

        def sample_map(*idx):
            return (layer, self.sample_tile(idx[tile_axis]), 0, chunk)

        return [pl.BlockSpec((1, 1, 1, d), prompt_map), pl.BlockSpec((1, self.groups, 1, d), sample_map)]


def _pick_mod(is_prompt, mp_ref, ms_ref):
    return jnp.where(is_prompt, mp_ref[0], ms_ref[0])


def _mod_kernel(c_ref, w_ref, b_ref, o_ref):
    c = c_ref[...]
    s = (c * _sigmoid(c)).astype(BF16)
    o_ref[0] = jnp.dot(s, w_ref[0].astype(BF16), preferred_element_type=F32) + b_ref[0]


def _mod_call(c_all, w_mod, b_mod, tn):
    depth, d, n = w_mod.shape
    mc = c_all.shape[0]
    return pl.pallas_call(
        _mod_kernel,
        grid=(depth, n // tn),
        in_specs=[
            pl.BlockSpec((mc, d), lambda l, j: (0, 0)),
            pl.BlockSpec((1, d, tn), lambda l, j: (l, 0, j)),
            pl.BlockSpec((1, 1, tn), lambda l, j: (l, 0, j)),
        ],
        out_specs=pl.BlockSpec((1, mc, tn), lambda l, j: (l, 0, j)),
        out_shape=jax.ShapeDtypeStruct((depth, mc, n), F32),
        compiler_params=_cparams("arbitrary", "arbitrary"),
        name="mod",
    )(c_all, w_mod, b_mod.reshape(depth, 1, n))


def _norm_gates(x3, sc, sh, wg_ref, bg_ref, h_ref, g_ref, heads):
    groups, _, d = x3.shape
    xn = x3 * lax.rsqrt(jnp.mean(x3 * x3, axis=-1, keepdims=True) + EPS)
    hb = (xn * (1.0 + sc) + sh).reshape(groups * SUBLANES, d).astype(BF16)
    h_ref[...] = hb
    g = jnp.dot(hb, wg_ref[0], preferred_element_type=F32) + bg_ref[0]
    lane = lax.broadcasted_iota(jnp.int32, g.shape, 1)
    is_forget = (lane >= heads) & (lane < 2 * heads)
    g_ref[...] = jnp.where(is_forget, -_softplus(-g), g)


def _prenorm_kernel(x_ref, shp_ref, shs_ref, scp_ref, scs_ref, wg_ref, bg_ref, h_ref, g_ref, *, heads, n_prompt_tiles):
    is_prompt = pl.program_id(0) < n_prompt_tiles
    _norm_gates(_rows3(x_ref[...]), _pick_mod(is_prompt, scp_ref, scs_ref), _pick_mod(is_prompt, shp_ref, shs_ref),
                wg_ref, bg_ref, h_ref, g_ref, heads)


def _prenorm_call(x, mod_p, mod_s, w_gate, b_gate, heads, tok):
    t, d = x.shape
    tm = tok.tile
    return pl.pallas_call(
        functools.partial(_prenorm_kernel, heads=heads, n_prompt_tiles=tok.n_prompt_tiles),
        grid=(t // tm,),
        in_specs=[
            pl.BlockSpec((tm, d), lambda i: (i, 0)),
            *tok.mod_specs(0, SH_A, d),
            *tok.mod_specs(0, SC_A, d),
            pl.BlockSpec((1, d, LANES), lambda i: (0, 0, 0)),
            pl.BlockSpec((1, 1, LANES), lambda i: (0, 0, 0)),
        ],
        out_specs=[
            pl.BlockSpec((tm, d), lambda i: (i, 0)),
            pl.BlockSpec((tm, LANES), lambda i: (i, 0)),
        ],
        out_shape=[
            jax.ShapeDtypeStruct((t, d), BF16),
            jax.ShapeDtypeStruct((t, LANES), F32),
        ],
        compiler_params=_cparams("arbitrary"),
        name="prenorm",
    )(x, mod_p, mod_s, mod_p, mod_s, w_gate, b_gate)


def _in_kernel(h_ref, w_ref, u_ref, wb_scr):
    @pl.when(pl.program_id(1) == 0)
    def _():
        wb_scr[...] = w_ref[0].astype(BF16)

    u_ref[...] = lax.dot_general(h_ref[...], wb_scr[...], (((1,), (1,)), ((), ())), preferred_element_type=F32)


def _in_call(h, w_in_t, layer, n_main, tm, tn):
    t, d = h.shape
    return pl.pallas_call(
        _in_kernel,
        grid=(n_main // tn, t // tm),
        in_specs=[
            pl.BlockSpec((tm, d), lambda j, i: (i, 0)),
            pl.BlockSpec((1, tn, d), lambda j, i: (layer, j, 0)),
        ],
        out_specs=pl.BlockSpec((tm, tn), lambda j, i: (i, j)),
        out_shape=jax.ShapeDtypeStruct((t, n_main), F32),
        scratch_shapes=[pltpu.VMEM((tn, d), BF16)],
        compiler_params=_cparams("arbitrary", "arbitrary"),
        name="in_proj",
    )(h, w_in_t)


def _rg_kernel(xr_ref, yr_ref, prev_ref, h0_ref, cw_ref, cb_ref, wra_ref, bra_ref, wri_ref, bri_ref,
               lam_ref, gain_ref, hr_ref, mix_ref, xprev_scr, hprev_scr, hstart_scr, a_scr, b_scr,
               *, n_prompt_tiles, tiles_per_seq):
    i = pl.program_id(0)
    tl, c = xr_ref.shape
    groups = tl // SUBLANES
    _, nblk, bw, _ = wra_ref.shape
    is_prompt = i < n_prompt_tiles

    @pl.when(is_prompt & (i % tiles_per_seq == 0))
    def _():
        xprev_scr[...] = jnp.zeros_like(xprev_scr)
        hprev_scr[...] = jnp.zeros_like(hprev_scr)

    x = xr_ref[...]
    prev_prompt = jnp.concatenate([xprev_scr[...], x[:tl - SUBLANES]], axis=0)
    prev = jnp.where(is_prompt, prev_prompt, prev_ref[0])
    xprev_scr[...] = x[tl - SUBLANES:]

    x3 = _rows3(x)
    p3 = _rows3(prev)
    row = lax.broadcasted_iota(jnp.int32, x3.shape, 1)
    cw = cw_ref[0]
    conv_w = cw.shape[0]
    xc = cb_ref[0] + x3 * cw[conv_w - 1:conv_w]
    for dlt in range(1, conv_w):
        shifted = jnp.where(row >= dlt, pltpu.roll(x3, dlt, 1), pltpu.roll(p3, dlt, 1))
        xc = xc + shifted * cw[conv_w - 1 - dlt:conv_w - dlt]

    xc2 = xc.reshape(tl, c)
    xcb = xc2.astype(BF16)
    r_parts, i_parts = [], []
    for nb in range(nblk):
        blk = xcb[:, nb * bw:(nb + 1) * bw]
        r_parts.append(jnp.dot(blk, wra_ref[0, nb], preferred_element_type=F32))
        i_parts.append(jnp.dot(blk, wri_ref[0, nb], preferred_element_type=F32))
    r = _sigmoid(jnp.concatenate(r_parts, axis=-1) + bra_ref[0])
    ig = _sigmoid(jnp.concatenate(i_parts, axis=-1) + bri_ref[0])
    log_a = (-RG_C) * r * _softplus(-lam_ref[0])
    a = jnp.exp(log_a)
    th = jnp.tanh(log_a)
    gap = -2.0 * th / (1.0 - th)
    mult = jnp.where(gap > 0.0, gap * lax.rsqrt(gap), 0.0)
    bx = mult * (ig * xc2)

    av = _rows3(a)
    bv = _rows3(bx)
    for s in (1, 2, 4):
        a_sh = jnp.where(row >= s, pltpu.roll(av, s, 1), 1.0)
        b_sh = jnp.where(row >= s, pltpu.roll(bv, s, 1), 0.0)
        bv = av * b_sh + bv
        av = av * a_sh

    @pl.when(is_prompt)
    def _():
        a_scr[...] = av.reshape(tl, c)
        b_scr[...] = bv.reshape(tl, c)

        def body(g, h):
            hstart_scr[g] = h
            last = g * SUBLANES + (SUBLANES - 1)
            return a_scr[pl.ds(last, 1), :] * h + b_scr[pl.ds(last, 1), :]

        hprev_scr[...] = lax.fori_loop(0, groups, body, hprev_scr[...])

    @pl.when(jnp.logical_not(is_prompt))
    def _():
        hstart_scr[...] = h0_ref[0]

    h3 = av * hstart_scr[...] + bv
    hr = h3.reshape(tl, c)
    hr_ref[...] = hr
    y = yr_ref[...]
    gelu = 0.5 * y * (1.0 + jnp.tanh(np.sqrt(2.0 / np.pi).astype(np.float32) * (y + 0.044715 * (y * y * y))))
    hn = hr * lax.rsqrt(jnp.mean(hr * hr, axis=-1, keepdims=True) + EPS)
    mix_ref[...] = (hn * gain_ref[0] * gelu).astype(BF16)


def _rg_call(u, prev_s, h0_s, conv_w, conv_b, w_ra, b_ra, w_ri, b_ri, lam, g_rg, layer, tok):
    t = u.shape[0]
    _, conv_width, c = conv_w.shape
    _, nblk, bw, _ = w_ra.shape
    tl, groups = tok.tile, tok.groups
    vec = pl.BlockSpec((1, 1, c), lambda i: (layer, 0, 0))
    return pl.pallas_call(
        functools.partial(_rg_kernel, n_prompt_tiles=tok.n_prompt_tiles, tiles_per_seq=tok.tiles_per_seq),
        grid=(t // tl,),
        in_specs=[
            pl.BlockSpec((tl, c), lambda i: (i, 0)),
            pl.BlockSpec((tl, c), lambda i: (i, 1)),
            pl.BlockSpec((1, tl, c), lambda i: (layer, tok.sample_tile(i), 0)),
            pl.BlockSpec((1, groups, 1, c), lambda i: (layer, tok.sample_tile(i), 0, 0)),
            pl.BlockSpec((1, conv_width, c), lambda i: (layer, 0, 0)),
            vec,
            pl.BlockSpec((1, nblk, bw, bw), lambda i: (layer, 0, 0, 0)),
            vec,
            pl.BlockSpec((1, nblk, bw, bw), lambda i: (layer, 0, 0, 0)),
            vec,
            vec,
            vec,
        ],
        out_specs=[
            pl.BlockSpec((tl, c), lambda i: (i, 0)),
            pl.BlockSpec((tl, c), lambda i: (i, 0)),
        ],
        out_shape=[
            jax.ShapeDtypeStruct((t, c), F32),
            jax.ShapeDtypeStruct((t, c), BF16),
        ],
        scratch_shapes=[
            pltpu.VMEM((SUBLANES, c), F32),
            pltpu.VMEM((1, c), F32),
            pltpu.VMEM((groups, 1, c), F32),
            pltpu.VMEM((tl, c), F32),
            pltpu.VMEM((tl, c), F32),
        ],
        compiler_params=_cparams("arbitrary"),
        name="rg_lru",
    )(u, u, prev_s, h0_s, conv_w, conv_b, w_ra, b_ra, w_ri, b_ri, lam, g_rg)


def _ml_chunk(nseq, heads, dk, dv, q_ref, k_ref, v_ref, o_ref, gc, gr, mprev, gain_ref, mix_ref,
              get_c, get_n, put_state):
    tc = q_ref.shape[0]
    ls = tc // nseq
    shift = int(np.log2(ls))
    t_idx = lax.broadcasted_iota(jnp.int32, (tc, tc), 0)
    s_idx = lax.broadcasted_iota(jnp.int32, (tc, tc), 1)
    mask = (lax.shift_right_logical(t_idx, shift) == lax.shift_right_logical(s_idx, shift)) & (s_idx <= t_idx)
    m_cum = mask.astype(F32)
    bcol = jnp.dot(m_cum, gc, precision=HI, preferred_element_type=F32)
    brow = lax.dot_general(gr, m_cum, (((1,), (1,)), ((), ())), precision=HI,
                           preferred_element_type=F32)
    lane = lax.broadcasted_iota(jnp.int32, (tc, LANES), 1)
    scale = np.float32(dk ** -0.5)

    def seg_last(col):
        if nseq == 1:
            return col[tc - 1:tc, :].reshape(1, 1, 1)
        return col.reshape(nseq, ls, 1)[:, ls - 1:ls, :]

    def seg_bcast(val):
        return jnp.broadcast_to(val, (nseq, ls, 1)).reshape(tc, 1)

    m_out = jnp.zeros((tc, LANES), F32)
    for h in range(heads):
        bc = bcol[:, heads + h:heads + h + 1]
        br = brow[heads + h:heads + h + 1, :]
        igr = gr[h:h + 1, :]
        igc = gc[:, h:h + 1]
        mp = mprev[:, h:h + 1]
        dmat = jnp.where(mask, bc - br + igr, -jnp.inf)
        m_inter = bc + mp
        m_t = jnp.maximum(m_inter, jnp.max(dmat, axis=-1, keepdims=True))
        p = jnp.exp(dmat - m_t)
        qh = q_ref[:, h * dk:(h + 1) * dk] * scale
        kh = k_ref[:, h * dk:(h + 1) * dk]
        vh = v_ref[:, h * dv:(h + 1) * dv]
        qb = qh.astype(BF16)
        kb = kh.astype(BF16)
        s = lax.dot_general(qb, kb, (((1,), (1,)), ((), ())), preferred_element_type=F32) * p
        w_inter = jnp.exp(m_inter - m_t)
        c_old = get_c(h)
        n_old = get_n(h)
        q3 = qh.reshape(nseq, ls, dk)
        inter = lax.dot_general(q3.astype(BF16), c_old.astype(BF16), (((2,), (1,)), ((0,), (0,))),
                                preferred_element_type=F32).reshape(tc, dv)
        qn = jnp.sum(q3 * n_old, axis=-1, keepdims=True).reshape(tc, 1)
        num = jnp.dot(s.astype(BF16), vh.astype(BF16), preferred_element_type=F32) + w_inter * inter
        den = jnp.sum(s, axis=-1, keepdims=True) + w_inter * qn
        hh = num / jnp.maximum(jnp.abs(den), jnp.exp(-m_t))
        hn = hh * lax.rsqrt(jnp.mean(hh * hh, axis=-1, keepdims=True) + EPS)
        og = _sigmoid(o_ref[:, h * dv:(h + 1) * dv])
        mix_ref[:, h * dv:(h + 1) * dv] = (hn * gain_ref[0, :, h * dv:(h + 1) * dv] * og).astype(BF16)

        m_new = seg_last(m_t)
        b_last = seg_last(bc)
        wl = jnp.exp(seg_bcast(b_last) - bc + igc - seg_bcast(m_new))
        decay = jnp.exp(b_last + seg_last(mp) - m_new)
        wv3 = (wl * vh).reshape(nseq, ls, dv).astype(BF16)
        k3 = kh.reshape(nseq, ls, dk).astype(BF16)
        upd = lax.dot_general(k3, wv3, (((1,), (1,)), ((0,), (0,))), preferred_element_type=F32)
        c_new = decay * c_old + upd
        n_new = decay * n_old + jnp.sum((wl * kh).reshape(nseq, ls, dk), axis=1, keepdims=True)
        put_state(h, c_new, n_new)
        m_out = jnp.where(lane == h, seg_bcast(m_new), m_out)
    return m_out


def _ml_kernel(q_ref, k_ref, v_ref, o_ref, gc_ref, gr_ref, cs_ref, ns_ref, ms_ref, gain_ref, cs_alias,
               mix_ref, cp_out, np_out, mp_out, cs_out, ns_out, ms_out, c_scr, n_scr, m_scr,
               *, n_prompt_chunks, chunks_per_seq, heads, dk, dv, nseq_sample):
    del cs_alias
    i = pl.program_id(0)
    tc = q_ref.shape[0]
    is_prompt = i < n_prompt_chunks

    @pl.when(is_prompt & (i % chunks_per_seq == 0))
    def _():
        c_scr[...] = jnp.zeros_like(c_scr)
        n_scr[...] = jnp.zeros_like(n_scr)
        m_scr[...] = jnp.zeros_like(m_scr)

    @pl.when(is_prompt)
    def _():
        def put(h, c_new, n_new):
            c_scr[h] = c_new[0]
            n_scr[h] = n_new[0]

        mprev = jnp.broadcast_to(m_scr[...], (tc, LANES))
        m_out = _ml_chunk(1, heads, dk, dv, q_ref, k_ref, v_ref, o_ref, gc_ref[...], gr_ref[...], mprev,
                          gain_ref, mix_ref, lambda h: c_scr[h][None], lambda h: n_scr[h][None], put)
        m_scr[...] = m_out[:1, :]
        cp_out[0] = c_scr[...]
        np_out[0] = n_scr[...]
        mp_out[0] = m_out[:1, :]

    @pl.when(jnp.logical_not(is_prompt))
    def _():
        def put(h, c_new, n_new):
            cs_out[0, :, h] = c_new
            ns_out[:, h] = n_new

        m_out = _ml_chunk(nseq_sample, heads, dk, dv, q_ref, k_ref, v_ref, o_ref, gc_ref[...], gr_ref[...],
                          ms_ref[0], gain_ref, mix_ref, lambda h: cs_ref[0, :, h], lambda h: ns_ref[0, :, h], put)
        ms_out[...] = m_out


def _ml_call(u, gcol, grow, c_s, n_s, m_s_col, g_ml, cs_stack, layer, t_prompt, seq, dec_seq,
             q_off, k_off, v_off, o_off, tc):
    t = u.shape[0]
    depth, bs, heads, dk, dv = c_s.shape
    bp = t_prompt // seq
    hdk, hdv = heads * dk, heads * dv
    n_p = t_prompt // tc
    cps = seq // tc
    nseq = tc // dec_seq
    samp = lambda i: jnp.maximum(i - n_p, 0)
    pseq = lambda i: jnp.minimum(i // cps, bp - 1)
    t_s = t - t_prompt
    in_specs = [
        pl.BlockSpec((tc, hdk), lambda i: (i, q_off // hdk)),
        pl.BlockSpec((tc, hdk), lambda i: (i, k_off // hdk)),
        pl.BlockSpec((tc, hdv), lambda i: (i, v_off // hdv)),
        pl.BlockSpec((tc, hdv), lambda i: (i, o_off // hdv)),
        pl.BlockSpec((tc, LANES), lambda i: (i, 0)),
        pl.BlockSpec((2 * heads, tc), lambda i: (0, i)),
        pl.BlockSpec((1, nseq, heads, dk, dv), lambda i: (layer, samp(i), 0, 0, 0)),
        pl.BlockSpec((1, nseq, heads, 1, dk), lambda i: (layer, samp(i), 0, 0, 0)),
        pl.BlockSpec((1, tc, LANES), lambda i: (layer, samp(i), 0)),
        pl.BlockSpec((1, 1, hdv), lambda i: (layer, 0, 0)),
        pl.BlockSpec(memory_space=pl.ANY),
    ]
    args = [u, u, u, u, gcol, grow, c_s, n_s, m_s_col, g_ml, cs_stack]
    return pl.pallas_call(
        functools.partial(_ml_kernel, n_prompt_chunks=n_p, chunks_per_seq=cps, heads=heads, dk=dk, dv=dv,
                          nseq_sample=nseq),
        grid=(t // tc,),
        in_specs=in_specs,
        out_specs=[
            pl.BlockSpec((tc, hdv), lambda i: (i, 0)),
            pl.BlockSpec((1, heads, dk, dv), lambda i: (pseq(i), 0, 0, 0)),
            pl.BlockSpec((1, heads, 1, dk), lambda i: (pseq(i), 0, 0, 0)),
            pl.BlockSpec((1, 1, LANES), lambda i: (pseq(i), 0, 0)),
            pl.BlockSpec((1, nseq, heads, dk, dv), lambda i: (layer, samp(i), 0, 0, 0)),
            pl.BlockSpec((nseq, heads, 1, dk), lambda i: (samp(i), 0, 0, 0)),
            pl.BlockSpec((tc, LANES), lambda i: (samp(i), 0)),
        ],
        out_shape=[
            jax.ShapeDtypeStruct((t, hdv), BF16),
            jax.ShapeDtypeStruct((bp, heads, dk, dv), F32),
            jax.ShapeDtypeStruct((bp, heads, 1, dk), F32),
            jax.ShapeDtypeStruct((bp, 1, LANES), F32),
            jax.ShapeDtypeStruct((depth, bs, heads, dk, dv), F32),
            jax.ShapeDtypeStruct((bs, heads, 1, dk), F32),
            jax.ShapeDtypeStruct((t_s, LANES), F32),
        ],
        scratch_shapes=[
            pltpu.VMEM((heads, dk, dv), F32),
            pltpu.VMEM((heads, 1, dk), F32),
            pltpu.VMEM((1, LANES), F32),
        ],
        input_output_aliases={len(args) - 1: 4},
        compiler_params=_cparams("arbitrary"),
        name="mlstm",
    )(*args)


def _out_kernel(x_ref, mr_ref, mm_ref, gtp_ref, gts_ref, shp_ref, shs_ref, scp_ref, scs_ref, wo_hbm, wr_ref, br_ref,
                xn_ref, h2_ref, eid_ref, wt_ref, wo_scr, stage, sem, *, layer, n_groups, per_group, n_prompt_tiles):
    i = pl.program_id(0)
    tm, d = x_ref.shape
    c = mr_ref.shape[1]
    rows = stage.shape[0]

    @pl.when(i == 0)
    def _():
        for part in range(wo_scr.shape[0] // rows):
            cp = pltpu.make_async_copy(wo_hbm.at[layer, pl.ds(part * rows, rows)], stage, sem)
            cp.start()
            cp.wait()
            wo_scr[pl.ds(part * rows, rows), :] = stage[...].astype(BF16)

    is_prompt = i < n_prompt_tiles
    acc = jnp.dot(mr_ref[...], wo_scr[:c, :], preferred_element_type=F32)
    acc = acc + jnp.dot(mm_ref[...], wo_scr[c:, :], preferred_element_type=F32)
    xn3 = _rows3(x_ref[...]) + _pick_mod(is_prompt, gtp_ref, gts_ref) * _rows3(acc)
    rs = lax.rsqrt(jnp.mean(xn3 * xn3, axis=-1, keepdims=True) + EPS)
    h2 = (xn3 * rs * (1.0 + _pick_mod(is_prompt, scp_ref, scs_ref))
          + _pick_mod(is_prompt, shp_ref, shs_ref)).reshape(tm, d)
    xn_ref[...] = xn3.reshape(tm, d)
    _store_slabs(h2_ref, h2)

    logits = jnp.dot(h2.astype(BF16), wr_ref[0], preferred_element_type=F32) + br_ref[0]
    lane = lax.broadcasted_iota(jnp.int32, logits.shape, 1)
    lane_f = lane.astype(F32)
    big = np.float32(LANES)

    def first_max(vals):
        vmax = jnp.max(vals, axis=-1, keepdims=True)
        idx = jnp.min(jnp.where(vals == vmax, lane_f, big), axis=-1, keepdims=True)
        return vmax, idx.astype(jnp.int32)

    gl = jnp.where(lane < n_groups, logits, -jnp.inf)
    gmax, g_sel = first_max(gl)
    p_g = 1.0 / jnp.sum(jnp.exp(gl - gmax), axis=-1, keepdims=True)
    lo = n_groups + g_sel * per_group
    el = jnp.where((lane >= lo) & (lane < lo + per_group), logits, -jnp.inf)
    v1, i1 = first_max(el)
    v2, i2 = first_max(jnp.where(lane == i1, -jnp.inf, el))
    e21 = jnp.exp(v2 - v1)
    w1 = 1.0 / (1.0 + e21)
    w2 = e21 * w1
    eid_ref[...] = jnp.where(lane == 0, i1 - n_groups, jnp.where(lane == 1, i2 - n_groups, 0))
    wt_ref[...] = jnp.where(lane == 0, w1 * p_g, jnp.where(lane == 1, w2 * p_g, 0.0))


def _out_call(x, mix_rg, mix_ml, mod_p, mod_s, w_out, w_router, b_router, layer, n_groups, per_group, tok):
    t, d = x.shape
    c = mix_rg.shape[1]
    cm = mix_ml.shape[1]
    tm = tok.tile
    stage_rows = min(256, c + cm)
    return pl.pallas_call(
        functools.partial(_out_kernel, layer=layer, n_groups=n_groups, per_group=per_group,
                          n_prompt_tiles=tok.n_prompt_tiles),
        grid=(t // tm,),
        in_specs=[
            pl.BlockSpec((tm, d), lambda i: (i, 0)),
            pl.BlockSpec((tm, c), lambda i: (i, 0)),
            pl.BlockSpec((tm, cm), lambda i: (i, 0)),
            *tok.mod_specs(layer, GT_A, d),
            *tok.mod_specs(layer, SH_F, d),
            *tok.mod_specs(layer, SC_F, d),
            pl.BlockSpec(memory_space=pl.ANY),
            pl.BlockSpec((1, d, LANES), lambda i: (layer, 0, 0)),
            pl.BlockSpec((1, 1, LANES), lambda i: (layer, 0, 0)),
        ],
        out_specs=[
            pl.BlockSpec((tm, d), lambda i: (i, 0)),
            pl.BlockSpec((tm * _slab_rows(d), LANES), lambda i: (i, 0)),
            pl.BlockSpec((tm, LANES), lambda i: (i, 0)),
            pl.BlockSpec((tm, LANES), lambda i: (i, 0)),
        ],
        out_shape=[
            jax.ShapeDtypeStruct((t, d), F32),
            jax.ShapeDtypeStruct((t * _slab_rows(d), LANES), F32),
            jax.ShapeDtypeStruct((t, LANES), jnp.int32),
            jax.ShapeDtypeStruct((t, LANES), F32),
        ],
        scratch_shapes=[
            pltpu.VMEM((c + cm, d), BF16),
            pltpu.VMEM((stage_rows, d), F32),
            pltpu.SemaphoreType.DMA(()),
        ],
        compiler_params=_cparams("arbitrary"),
        name="out_proj_router",
    )(x, mix_rg, mix_ml, mod_p, mod_s, mod_p, mod_s, mod_p, mod_s, w_out, w_router, b_router)


def _slab_rows(d):
    return d // LANES


def _store_slabs(ref, val):
    rows, d = val.shape
    slab = _slab_rows(d)
    for s in range(slab):
        ref[pl.ds(s, rows, stride=slab), :] = val[:, s * LANES:(s + 1) * LANES]


def _load_slabs(ref, rows, d):
    slab = _slab_rows(d)
    return jnp.concatenate([ref[pl.ds(s, rows, stride=slab), :] for s in range(slab)], axis=-1)


def _gather_slabs(idx_ref, base, n_rows, src_hbm, dst, sem, slab, stride=1, offset=0):
    def body(g, carry):
        for s in range(SUBLANES):
            r = g * SUBLANES + s
            tok = idx_ref[base + r * stride + offset]
            pltpu.make_async_copy(src_hbm.at[pl.ds(pl.multiple_of(tok * slab, slab), slab)],
                                  dst.at[pl.ds(pl.multiple_of(r * slab, slab), slab)], sem).start()
        return carry

    lax.fori_loop(0, n_rows // SUBLANES, body, 0)


def _wait_rows(buf, sem):
    pltpu.make_async_copy(buf, buf, sem).wait()


def _moe_kernel(te_ref, nx_ref, nu_ref, src_ref, h2_hbm, w1_hbm, w3_hbm, w2_hbm, y_ref,
                xbuf, st1, st3, st2, w1_scr, w3_scr, w2_scr, slot_ref, gsem, wsem,
                *, layer, weight_chunks, weights_first):
    j = pl.program_id(0)
    nbuf = xbuf.shape[0]
    d = w1_scr.shape[0]
    slab = _slab_rows(d)
    tm = xbuf.shape[1] // slab
    n_used = nu_ref[0]

    def weight_copies(expert, slot):
        copies = []
        for w_hbm, stage in ((w1_hbm, st1), (w3_hbm, st3), (w2_hbm, st2)):
            parts = min(weight_chunks, stage.shape[1] // SUBLANES)
            rows = stage.shape[1] // parts
            for part in range(parts):
                span = pl.ds(part * rows, rows)
                copies.append(pltpu.make_async_copy(w_hbm.at[layer, expert, span], stage.at[slot, span],
                                                    wsem.at[slot]))
        return copies

    def issue(tile):
        _gather_slabs(src_ref, tile * tm, tm, h2_hbm, xbuf.at[tile % nbuf], gsem.at[tile % nbuf], slab)

    @pl.when(j == 0)
    def _():
        slot_ref[0] = 1
        for cp in weight_copies(te_ref[0], 0):
            cp.start()
        for ahead in range(nbuf - 1):
            @pl.when(ahead < n_used)
            def _():
                issue(ahead)

    def gather_ahead():
        @pl.when(j + (nbuf - 1) < n_used)
        def _():
            issue(j + (nbuf - 1))

    def expert_switch():
        @pl.when((j < n_used) & ((j == 0) | (te_ref[j] != te_ref[jnp.maximum(j - 1, 0)])))
        def _():
            slot = 1 - slot_ref[0]
            slot_ref[0] = slot
            for cp in weight_copies(te_ref[j], slot):
                cp.wait()
            w1_scr[...] = st1[slot].astype(BF16)
            w3_scr[...] = st3[slot].astype(BF16)
            w2_scr[...] = st2[slot].astype(BF16)

            @pl.when(nx_ref[j] >= 0)
            def _():
                for cp in weight_copies(nx_ref[j], 1 - slot):
                    cp.start()

    if weights_first:
        expert_switch()
        gather_ahead()
    else:
        gather_ahead()
        expert_switch()

    @pl.when(j < n_used)
    def _():
        slot = j % nbuf
        _wait_rows(xbuf.at[slot], gsem.at[slot])
        xb = _load_slabs(xbuf.at[slot], tm, d).astype(BF16)
        a = jnp.dot(xb, w1_scr[...], preferred_element_type=F32)
        b = jnp.dot(xb, w3_scr[...], preferred_element_type=F32)
        hid = (a * _sigmoid(a)) * b
        _store_slabs(y_ref, jnp.dot(hid.astype(BF16), w2_scr[...], preferred_element_type=F32))

    @pl.when(j >= n_used)
    def _():
        y_ref[...] = jnp.zeros_like(y_ref)


def _moe_call(tile_exp, next_exp, n_used, src_tok, h2_slabs, w1, w3, w2, layer, tm, weight_chunks, weights_first):
    _, ne, d, f = w1.shape
    slab = _slab_rows(d)
    n_tiles = tile_exp.shape[0]
    gather_bufs = 3
    grid_spec = pltpu.PrefetchScalarGridSpec(
        num_scalar_prefetch=4,
        grid=(n_tiles,),
        in_specs=[pl.BlockSpec(memory_space=pl.ANY)] * 4,
        out_specs=pl.BlockSpec((tm * slab, LANES), lambda j, *_: (j, 0)),
        scratch_shapes=[
            pltpu.VMEM((gather_bufs, tm * slab, LANES), F32),
            pltpu.VMEM((2, d, f), F32),
            pltpu.VMEM((2, d, f), F32),
            pltpu.VMEM((2, f, d), F32),
            pltpu.VMEM((d, f), BF16),
            pltpu.VMEM((d, f), BF16),
            pltpu.VMEM((f, d), BF16),
            pltpu.SMEM((1,), jnp.int32),
            pltpu.SemaphoreType.DMA((gather_bufs,)),
            pltpu.SemaphoreType.DMA((2,)),
        ],
    )
    return pl.pallas_call(
        functools.partial(_moe_kernel, layer=layer, weight_chunks=weight_chunks, weights_first=weights_first),
        grid_spec=grid_spec,
        out_shape=jax.ShapeDtypeStruct((n_tiles * tm * slab, LANES), F32),
        compiler_params=_cparams("arbitrary"),
        name="experts",
    )(tile_exp, next_exp, n_used, src_tok, h2_slabs, w1, w3, w2)


def _comb_kernel(dest_ref, xn_ref, wt_ref, gtp_ref, gts_ref, *rest, final, heads, n_prompt_tiles):
    if final:
        gf_ref, y_hbm, yp_ref, ys_ref, ybuf, sem = rest
    else:
        shp_ref, shs_ref, scp_ref, scs_ref, wg_ref, bg_ref, y_hbm, xo_ref, h_ref, g_ref, ybuf, sem = rest
    i = pl.program_id(0)
    n = pl.num_programs(0)
    tm, d = xn_ref.shape
    is_prompt = i < n_prompt_tiles

    def issue(tile, slot):
        for k in range(2):
            _gather_slabs(dest_ref, tile * tm * 2, tm, y_hbm, ybuf.at[slot, k], sem.at[slot], _slab_rows(d),
                          stride=2, offset=k)

    @pl.when(i == 0)
    def _():
        issue(0, 0)

    @pl.when(i + 1 < n)
    def _():
        issue(i + 1, (i + 1) % 2)

    slot = i % 2
    for k in range(2):
        _wait_rows(ybuf.at[slot, k], sem.at[slot])
    wt = wt_ref[...]
    y = wt[:, 0:1] * _load_slabs(ybuf.at[slot, 0], tm, d) + wt[:, 1:2] * _load_slabs(ybuf.at[slot, 1], tm, d)
    xo3 = _rows3(xn_ref[...]) + _pick_mod(is_prompt, gtp_ref, gts_ref) * _rows3(y)
    if final:
        xo = xo3.reshape(tm, d)
        yo = xo * lax.rsqrt(jnp.mean(xo * xo, axis=-1, keepdims=True) + EPS) * gf_ref[...]

        @pl.when(is_prompt)
        def _():
            yp_ref[...] = yo

        @pl.when(jnp.logical_not(is_prompt))
        def _():
            ys_ref[...] = yo
    else:
        xo_ref[...] = xo3.reshape(tm, d)
        _norm_gates(xo3, _pick_mod(is_prompt, scp_ref, scs_ref), _pick_mod(is_prompt, shp_ref, shs_ref),
                    wg_ref, bg_ref, h_ref, g_ref, heads)


def _comb_call(dest, xn, wts, mod_p, mod_s, w_gate, b_gate, g_final, y, layer, heads, tok, final):
    t, d = xn.shape
    tm = tok.tile
    full = lambda i, dst: (0, 0)
    in_specs = [
        pl.BlockSpec((tm, d), lambda i, dst: (i, 0)),
        pl.BlockSpec((tm, LANES), lambda i, dst: (i, 0)),
        *tok.mod_specs(layer, GT_F, d),
    ]
    args = [xn, wts, mod_p, mod_s]
    if final:
        n_p = tok.n_prompt_tiles
        in_specs.append(pl.BlockSpec((1, d), full))
        args.append(g_final)
        out_specs = [pl.BlockSpec((tm, d), lambda i, dst: (jnp.minimum(i, n_p - 1), 0)),
                     pl.BlockSpec((tm, d), lambda i, dst: (tok.sample_tile(i), 0))]
        out_shape = [jax.ShapeDtypeStruct((n_p * tm, d), F32), jax.ShapeDtypeStruct((t - n_p * tm, d), F32)]
    else:
        out_specs = [pl.BlockSpec((tm, d), lambda i, dst: (i, 0))]
        out_shape = [jax.ShapeDtypeStruct((t, d), F32)]
        nxt = layer + 1
        in_specs += [
            *tok.mod_specs(nxt, SH_A, d),
            *tok.mod_specs(nxt, SC_A, d),
            pl.BlockSpec((1, d, LANES), lambda i, dst: (nxt, 0, 0)),
            pl.BlockSpec((1, 1, LANES), lambda i, dst: (nxt, 0, 0)),
        ]
        args += [mod_p, mod_s, mod_p, mod_s, w_gate, b_gate]
        out_specs += [pl.BlockSpec((tm, d), lambda i, dst: (i, 0)), pl.BlockSpec((tm, LANES), lambda i, dst: (i, 0))]
        out_shape += [jax.ShapeDtypeStruct((t, d), BF16), jax.ShapeDtypeStruct((t, LANES), F32)]
    in_specs.append(pl.BlockSpec(memory_space=pl.ANY))
    args.append(y)
    grid_spec = pltpu.PrefetchScalarGridSpec(
        num_scalar_prefetch=1,
        grid=(t // tm,),
        in_specs=in_specs,
        out_specs=out_specs,
        scratch_shapes=[pltpu.VMEM((2, 2, tm * _slab_rows(d), LANES), F32), pltpu.SemaphoreType.DMA((2,))],
    )
    return pl.pallas_call(
        functools.partial(_comb_kernel, final=final, heads=heads, n_prompt_tiles=tok.n_prompt_tiles),
        grid_spec=grid_spec,
        out_shape=out_shape,
        compiler_params=_cparams("arbitrary"),
        name="combine",
    )(dest, *args)


def _invert_kernel(dest_ref, src_ref):
    n_pairs = dest_ref.shape[0]
    n_rows = src_ref.shape[0]

    def clear(i, carry):
        for u in range(INVERT_CLEAR_UNROLL):
            src_ref[i * INVERT_CLEAR_UNROLL + u] = 0
        return carry

    lax.fori_loop(0, n_rows // INVERT_CLEAR_UNROLL, clear, 0)

    def place(i, carry):
        for u in range(SUBLANES):
            pair = i * SUBLANES + u
            src_ref[dest_ref[pair]] = lax.shift_right_logical(pair, 1)
        return carry

    lax.fori_loop(0, n_pairs // SUBLANES, place, 0)


def _invert_call(dest, n_rows):
    assert dest.shape[0] % SUBLANES == 0 and n_rows % INVERT_CLEAR_UNROLL == 0
    return pl.pallas_call(
        _invert_kernel,
        in_specs=[pl.BlockSpec(memory_space=pltpu.SMEM)],
        out_specs=pl.BlockSpec(memory_space=pltpu.SMEM),
        out_shape=jax.ShapeDtypeStruct((n_rows,), jnp.int32),
        name="route_invert",
    )(dest)


def _route_plan(eid, n_experts, tm):
    t = eid.shape[0]
    e_flat = eid[:, :2].reshape(-1)
    n_pairs = 2 * t
    n_tiles = -(-n_pairs // tm) + n_experts
    onehot = (e_flat[:, None] == jnp.arange(n_experts, dtype=jnp.int32)[None, :]).astype(jnp.int32)
    csum = jnp.cumsum(onehot, axis=0)
    rank = jnp.sum(csum * onehot, axis=1) - 1
    counts = csum[-1]
    padded = ((counts + tm - 1) // tm) * tm
    ends = jnp.cumsum(padded)
    starts = ends - padded
    dest = (jnp.sum(starts[None, :] * onehot, axis=1) + rank).astype(jnp.int32)
    src_tok = _invert_call(dest, n_tiles * tm)
    n_used = (ends[-1] // tm).astype(jnp.int32)
    tile_id = jnp.minimum(jnp.arange(n_tiles, dtype=jnp.int32), n_used - 1)
    tile_exp = jnp.sum((ends[None, :] <= (tile_id * tm)[:, None]).astype(jnp.int32), axis=1)
    tile_exp = jnp.minimum(tile_exp, n_experts - 1).astype(jnp.int32)
    ids = jnp.arange(n_experts, dtype=jnp.int32)
    later = (ids[None, :] > ids[:, None]) & (padded[None, :] > 0)
    following = jnp.min(jnp.where(later, ids[None, :], n_experts), axis=1)
    following = jnp.where(following == n_experts, -1, following).astype(jnp.int32)
    return tile_exp, following[tile_exp], n_used.reshape(1), src_tok, dest


def _pick_tile(total, pref):
    tile = min(pref, total)
    while total % tile:
        tile //= 2
    return tile


def kernel(x_prompt, x_sample, c_prompt, c_sample, state_rg_conv, state_rg_h, state_mlstm_C, state_mlstm_n,
           state_mlstm_m, w_mod, b_mod, w_in, conv_w, conv_b, w_ra, b_ra, w_ri, b_ri, lam, g_rg, b_ig, b_fg,
           g_ml, w_out, w_grp, b_grp, w_er, b_er, w1, w3, w2, g_final):
    bp, seq, d = x_prompt.shape
    bs, dec_seq, _ = x_sample.shape
    depth = w_mod.shape[0]
    d_rg = conv_w.shape[2]
    conv_width = conv_w.shape[1]
    _, _, heads, dk, dv = state_mlstm_C.shape
    d_ml = heads * dv
    n_groups, per_group = w_er.shape[1], w_er.shape[3]
    n_experts = w1.shape[1]
    t_p, t_s = bp * seq, bs * dec_seq
    t = t_p + t_s
    n_main = 2 * d_rg + 2 * heads * dk + 2 * d_ml
    q_off, k_off, v_off, o_off = 2 * d_rg, 2 * d_rg + heads * dk, 2 * d_rg + 2 * heads * dk, n_main - d_ml
    assert dec_seq == SUBLANES and seq % LANES == 0 and conv_width - 1 <= SUBLANES
    assert 2 * heads <= LANES and n_groups * (per_group + 1) <= LANES
    assert q_off % (heads * dk) == 0 and k_off % (heads * dk) == 0 and v_off % d_ml == 0 and o_off % d_ml == 0

    tok = _Tokens(t_p, t_s, seq, _pick_tile(np.gcd(seq, t_s), 256))
    tok_out = _Tokens(t_p, t_s, seq, _pick_tile(np.gcd(seq, t_s), 512))
    tm_in = _pick_tile(np.gcd(t_p, t_s), 1024)
    tc = LANES
    assert t_s % tc == 0 and seq % tc == 0

    mc = -(-(bp + bs) // SUBLANES) * SUBLANES
    c_all = jnp.zeros((mc, d), F32).at[:bp].set(c_prompt).at[bp:bp + bs].set(c_sample)
    mod = _mod_call(c_all, w_mod, b_mod, _pick_tile(6 * d, 1024))
    mod_p = mod[:, :bp].reshape(depth, bp, 1, 6 * d)
    mod_s = mod[:, bp:bp + bs].reshape(depth, bs, 1, 6 * d)

    w_in_t = jnp.swapaxes(w_in, 1, 2)
    w_gate = jnp.zeros((depth, d, LANES), F32).at[:, :, :2 * heads].set(w_in[:, :, n_main:]).astype(BF16)
    b_gate = jnp.zeros((depth, 1, LANES), F32).at[:, 0, :heads].set(b_ig).at[:, 0, heads:2 * heads].set(b_fg)
    w_ra_b, w_ri_b = w_ra.astype(BF16), w_ri.astype(BF16)
    n_rt = n_groups * (per_group + 1)
    w_router = jnp.concatenate([w_grp, jnp.moveaxis(w_er, 1, 2).reshape(depth, d, n_groups * per_group)], axis=-1)
    w_router = jnp.zeros((depth, d, LANES), F32).at[:, :, :n_rt].set(w_router).astype(BF16)
    b_router = jnp.concatenate([b_grp, b_er.reshape(depth, -1)], axis=-1)
    b_router = jnp.zeros((depth, 1, LANES), F32).at[:, 0, :n_rt].set(b_router)
    vec = lambda p: p.reshape(depth, 1, -1)

    pad_rows = SUBLANES - (conv_width - 1)
    prev_s = jnp.pad(state_rg_conv, ((0, 0), (0, 0), (pad_rows, 0), (0, 0))).reshape(depth, bs * SUBLANES, d_rg)
    h0_s = state_rg_h.reshape(depth, bs, 1, d_rg)
    n_s = state_mlstm_n.reshape(depth, bs, heads, 1, dk)
    m_s_col = jnp.pad(jnp.repeat(state_mlstm_m, dec_seq, axis=1), ((0, 0), (0, 0), (0, LANES - heads)))

    x = jnp.concatenate([x_prompt.reshape(t_p, d), x_sample.reshape(t_s, d)], axis=0)
    h, gcol = _prenorm_call(x, mod_p, mod_s, w_gate, b_gate, heads, tok)
    outs_p = [[] for _ in range(5)]
    outs_s = [[] for _ in range(5)]
    cs_stack = jnp.zeros_like(state_mlstm_C)
    for l in range(depth):
        u = _in_call(h, w_in_t, l, n_main, tm_in, _pick_tile(n_main, 1024))
        hr, mix_rg = _rg_call(u, prev_s, h0_s, conv_w, vec(conv_b), w_ra_b, vec(b_ra), w_ri_b, vec(b_ri),
                              vec(lam), vec(g_rg), l, tok)
        grow = gcol[:, :2 * heads].T
        mix_ml, c_p, n_p, m_p, cs_stack, n_so, m_so = _ml_call(
            u, gcol, grow, state_mlstm_C, n_s, m_s_col, vec(g_ml), cs_stack, l, t_p, seq, dec_seq,
            q_off, k_off, v_off, o_off, tc)
        xn, h2, eid, wts = _out_call(x, mix_rg, mix_ml, mod_p, mod_s, w_out, w_router, b_router, l,
                                     n_groups, per_group, tok_out)
        tile_exp, next_exp, n_used, src_tok, dest = _route_plan(eid, n_experts, tok.tile)
        y = _moe_call(tile_exp, next_exp, n_used, src_tok, h2, w1, w3, w2, l, tok.tile, *EXPERT_WEIGHT_DMA[l % 4])
        final = l == depth - 1
        res = _comb_call(dest, xn, wts, mod_p, mod_s, w_gate, b_gate, g_final[None], y, l, heads, tok, final)
        if final:
            y_p, y_s = res
        else:
            x, h, gcol = res

        tail = conv_width - 1
        outs_p[0].append(jnp.stack([u[(b + 1) * seq - tail:(b + 1) * seq, :d_rg] for b in range(bp)]))
        outs_p[1].append(jnp.stack([hr[(b + 1) * seq - 1] for b in range(bp)]))
        outs_p[2].append(c_p)
        outs_p[3].append(n_p.reshape(bp, heads, dk))
        outs_p[4].append(m_p[:, 0, :heads])
        outs_s[0].append(u[t_p:, :d_rg].reshape(bs, dec_seq, d_rg)[:, dec_seq - tail:])
        outs_s[1].append(hr[t_p:].reshape(bs, dec_seq, d_rg)[:, -1])
        outs_s[3].append(n_so.reshape(bs, heads, dk))
        outs_s[4].append(m_so.reshape(bs, dec_seq, LANES)[:, -1, :heads])

    stack = lambda parts: jnp.stack(parts)
    return (y_p.reshape(bp, seq, d), y_s.reshape(bs, dec_seq, d), *[stack(o) for o in outs_p],
            stack(outs_s[0]), stack(outs_s[1]), cs_stack, stack(outs_s[3]), stack(outs_s[4]))
```

```python
import functools

import numpy as np
import jax
import jax.numpy as jnp
from jax import lax
from jax.experimental import pallas as pl
from jax.experimental.pallas import tpu as pltpu

F32 = jnp.float32
BF16 = jnp.bfloat16
EPS = 1e-6
RG_C = 8.0
SUBLANES = 8
LANES = 128
VMEM_LIMIT = 56 * 1024 * 1024
HI = lax.Precision.HIGHEST
INVERT_CLEAR_UNROLL = 32
SH_A, SC_A, GT_A, SH_F, SC_F, GT_F = range(6)


def _cparams(*sem):
    return pltpu.CompilerParams(dimension_semantics=sem, vmem_limit_bytes=VMEM_LIMIT)


def _sigmoid(z):
    return 0.5 * jnp.tanh(0.5 * z) + 0.5


def _softplus(z):
    return jnp.maximum(z, 0.0) + jnp.log1p(jnp.exp(-jnp.abs(z)))


def _rows3(x):
    r, c = x.shape
    return x.reshape(r // SUBLANES, SUBLANES, c)


class _Tokens:
    def __init__(self, t_prompt, t_sample, seq, tile):
        self.tile = tile
        self.groups = tile // SUBLANES
        self.n_prompt_tiles = t_prompt // tile
        self.tiles_per_seq = seq // tile
        self.n_prompt_seqs = t_prompt // seq

    def prompt_seq(self, i):
        return jnp.minimum(i // self.tiles_per_seq, self.n_prompt_seqs - 1)

    def sample_tile(self, i):
        return jnp.maximum(i - self.n_prompt_tiles, 0)

    def mod_specs(self, layer, chunk, d, tile_axis=0):
        def prompt_map(*idx):
            return (layer, self.prompt_seq(idx[tile_axis]), 0, chunk)

        def sample_map(*idx):
            return (layer, self.sample_tile(idx[tile_axis]), 0, chunk)

        return [pl.BlockSpec((1, 1, 1, d), prompt_map), pl.BlockSpec((1, self.groups, 1, d), sample_map)]


def _pick_mod(is_prompt, mp_ref, ms_ref):
    return jnp.where(is_prompt, mp_ref[0], ms_ref[0])


def _mod_kernel(c_ref, w_ref, b_ref, o_ref):
    c = c_ref[...]
    s = (c * _sigmoid(c)).astype(BF16)
    o_ref[0] = jnp.dot(s, w_ref[0].astype(BF16), preferred_element_type=F32) + b_ref[0]


def _mod_call(c_all, w_mod, b_mod, tn):
    depth, d, n = w_mod.shape
    mc = c_all.shape[0]
    return pl.pallas_call(
        _mod_kernel,
        grid=(depth, n // tn),
        in_specs=[
            pl.BlockSpec((mc, d), lambda l, j: (0, 0)),
            pl.BlockSpec((1, d, tn), lambda l, j: (l, 0, j)),
            pl.BlockSpec((1, 1, tn), lambda l, j: (l, 0, j)),
        ],
        out_specs=pl.BlockSpec((1, mc, tn), lambda l, j: (l, 0, j)),
        out_shape=jax.ShapeDtypeStruct((depth, mc, n), F32),
        compiler_params=_cparams("arbitrary", "arbitrary"),
        name="mod",
    )(c_all, w_mod, b_mod.reshape(depth, 1, n))


def _norm_gates(x3, sc, sh, wg_ref, bg_ref, h_ref, g_ref, heads):
    groups, _, d = x3.shape
    xn = x3 * lax.rsqrt(jnp.mean(x3 * x3, axis=-1, keepdims=True) + EPS)
    hb = (xn * (1.0 + sc) + sh).reshape(groups * SUBLANES, d).astype(BF16)
    h_ref[...] = hb
    g = jnp.dot(hb, wg_ref[0], preferred_element_type=F32) + bg_ref[0]
    lane = lax.broadcasted_iota(jnp.int32, g.shape, 1)
    is_forget = (lane >= heads) & (lane < 2 * heads)
    g_ref[...] = jnp.where(is_forget, -_softplus(-g), g)


def _prenorm_kernel(xp_ref, xs_ref, shp_ref, shs_ref, scp_ref, scs_ref, wg_ref, bg_ref, x_ref, h_ref, g_ref,
                    *, heads, n_prompt_tiles):
    is_prompt = pl.program_id(0) < n_prompt_tiles
    x = jnp.where(is_prompt, xp_ref[...], xs_ref[...])
    x_ref[...] = x
    _norm_gates(_rows3(x), _pick_mod(is_prompt, scp_ref, scs_ref), _pick_mod(is_prompt, shp_ref, shs_ref),
                wg_ref, bg_ref, h_ref, g_ref, heads)


def _prenorm_call(x_p, x_s, mod_p, mod_s, w_gate, b_gate, heads, tok):
    d = x_p.shape[1]
    t = x_p.shape[0] + x_s.shape[0]
    tm = tok.tile
    n_p = tok.n_prompt_tiles
    return pl.pallas_call(
        functools.partial(_prenorm_kernel, heads=heads, n_prompt_tiles=n_p),
        grid=(t // tm,),
        in_specs=[
            pl.BlockSpec((tm, d), lambda i: (jnp.minimum(i, n_p - 1), 0)),
            pl.BlockSpec((tm, d), lambda i: (tok.sample_tile(i), 0)),
            *tok.mod_specs(0, SH_A, d),
            *tok.mod_specs(0, SC_A, d),
            pl.BlockSpec((1, d, LANES), lambda i: (0, 0, 0)),
            pl.BlockSpec((1, 1, LANES), lambda i: (0, 0, 0)),
        ],
        out_specs=[
            pl.BlockSpec((tm, d), lambda i: (i, 0)),
            pl.BlockSpec((tm, d), lambda i: (i, 0)),
            pl.BlockSpec((tm, LANES), lambda i: (i, 0)),
        ],
        out_shape=[
            jax.ShapeDtypeStruct((t, d), F32),
            jax.ShapeDtypeStruct((t, d), BF16),
            jax.ShapeDtypeStruct((t, LANES), F32),
        ],
        compiler_params=_cparams("arbitrary"),
        name="prenorm",
    )(x_p, x_s, mod_p, mod_s, mod_p, mod_s, w_gate, b_gate)


def _in_kernel(h_ref, w_ref, u_ref, wb_scr):
    @pl.when(pl.program_id(1) == 0)
    def _():
        wb_scr[...] = w_ref[0].astype(BF16)

    u_ref[...] = lax.dot_general(h_ref[...], wb_scr[...], (((1,), (1,)), ((), ())), preferred_element_type=F32)


def _in_call(h, w_in_t, layer, n_main, tm, tn):
    t, d = h.shape
    return pl.pallas_call(
        _in_kernel,
        grid=(n_main // tn, t // tm),
        in_specs=[
            pl.BlockSpec((tm, d), lambda j, i: (i, 0)),
            pl.BlockSpec((1, tn, d), lambda j, i: (layer, j, 0)),
        ],
        out_specs=pl.BlockSpec((tm, tn), lambda j, i: (i, j)),
        out_shape=jax.ShapeDtypeStruct((t, n_main), F32),
        scratch_shapes=[pltpu.VMEM((tn, d), BF16)],
        compiler_params=_cparams("arbitrary", "arbitrary"),
        name="in_proj",
    )(h, w_in_t)


def _rg_kernel(xr_ref, yr_ref, prev_ref, h0_ref, cw_ref, cb_ref, wra_ref, bra_ref, wri_ref, bri_ref,
               lam_ref, gain_ref, hr_ref, mix_ref, xprev_scr, hprev_scr, hstart_scr, a_scr, b_scr,
               *, n_prompt_tiles, tiles_per_seq):
    i = pl.program_id(0)
    tl, c = xr_ref.shape
    groups = tl // SUBLANES
    _, nblk, bw, _ = wra_ref.shape
    is_prompt = i < n_prompt_tiles

    @pl.when(is_prompt & (i % tiles_per_seq == 0))
    def _():
        xprev_scr[...] = jnp.zeros_like(xprev_scr)
        hprev_scr[...] = jnp.zeros_like(hprev_scr)

    x = xr_ref[...]
    prev_prompt = jnp.concatenate([xprev_scr[...], x[:tl - SUBLANES]], axis=0)
    prev = jnp.where(is_prompt, prev_prompt, prev_ref[0])
    xprev_scr[...] = x[tl - SUBLANES:]

    x3 = _rows3(x)
    p3 = _rows3(prev)
    row = lax.broadcasted_iota(jnp.int32, x3.shape, 1)
    cw = cw_ref[0]
    conv_w = cw.shape[0]
    xc = cb_ref[0] + x3 * cw[conv_w - 1:conv_w]
    for dlt in range(1, conv_w):
        shifted = jnp.where(row >= dlt, pltpu.roll(x3, dlt, 1), pltpu.roll(p3, dlt, 1))
        xc = xc + shifted * cw[conv_w - 1 - dlt:conv_w - dlt]

    xc2 = xc.reshape(tl, c)
    xcb = xc2.astype(BF16)
    r_parts, i_parts = [], []
    for nb in range(nblk):
        blk = xcb[:, nb * bw:(nb + 1) * bw]
        r_parts.append(jnp.dot(blk, wra_ref[0, nb], preferred_element_type=F32))
        i_parts.append(jnp.dot(blk, wri_ref[0, nb], preferred_element_type=F32))
    r = _sigmoid(jnp.concatenate(r_parts, axis=-1) + bra_ref[0])
    ig = _sigmoid(jnp.concatenate(i_parts, axis=-1) + bri_ref[0])
    log_a = (-RG_C) * r * _softplus(-lam_ref[0])
    a = jnp.exp(log_a)
    th = jnp.tanh(log_a)
    gap = -2.0 * th / (1.0 - th)
    mult = jnp.where(gap > 0.0, gap * lax.rsqrt(gap), 0.0)
    bx = mult * (ig * xc2)

    av = _rows3(a)
    bv = _rows3(bx)
    for s in (1, 2, 4):
        a_sh = jnp.where(row >= s, pltpu.roll(av, s, 1), 1.0)
        b_sh = jnp.where(row >= s, pltpu.roll(bv, s, 1), 0.0)
        bv = av * b_sh + bv
        av = av * a_sh

    @pl.when(is_prompt)
    def _():
        a_scr[...] = av.reshape(tl, c)
        b_scr[...] = bv.reshape(tl, c)

        def body(g, h):
            hstart_scr[g] = h
            last = g * SUBLANES + (SUBLANES - 1)
            return a_scr[pl.ds(last, 1), :] * h + b_scr[pl.ds(last, 1), :]

        hprev_scr[...] = lax.fori_loop(0, groups, body, hprev_scr[...])

    @pl.when(jnp.logical_not(is_prompt))
    def _():
        hstart_scr[...] = h0_ref[0]

    h3 = av * hstart_scr[...] + bv
    hr = h3.reshape(tl, c)
    hr_ref[...] = hr
    y = yr_ref[...]
    gelu = 0.5 * y * (1.0 + jnp.tanh(np.sqrt(2.0 / np.pi).astype(np.float32) * (y + 0.044715 * (y * y * y))))
    hn = hr * lax.rsqrt(jnp.mean(hr * hr, axis=-1, keepdims=True) + EPS)
    mix_ref[...] = (hn * gain_ref[0] * gelu).astype(BF16)


def _rg_call(u, prev_s, h0_s, conv_w, conv_b, w_ra, b_ra, w_ri, b_ri, lam, g_rg, layer, tok):
    t = u.shape[0]
    _, conv_width, c = conv_w.shape
    _, nblk, bw, _ = w_ra.shape
    tl, groups = tok.tile, tok.groups
    vec = pl.BlockSpec((1, 1, c), lambda i: (layer, 0, 0))
    return pl.pallas_call(
        functools.partial(_rg_kernel, n_prompt_tiles=tok.n_prompt_tiles, tiles_per_seq=tok.tiles_per_seq),
        grid=(t // tl,),
        in_specs=[
            pl.BlockSpec((tl, c), lambda i: (i, 0)),
            pl.BlockSpec((tl, c), lambda i: (i, 1)),
            pl.BlockSpec((1, tl, c), lambda i: (layer, tok.sample_tile(i), 0)),
            pl.BlockSpec((1, groups, 1, c), lambda i: (layer, tok.sample_tile(i), 0, 0)),
            pl.BlockSpec((1, conv_width, c), lambda i: (layer, 0, 0)),
            vec,
            pl.BlockSpec((1, nblk, bw, bw), lambda i: (layer, 0, 0, 0)),
            vec,
            pl.BlockSpec((1, nblk, bw, bw), lambda i: (layer, 0, 0, 0)),
            vec,
            vec,
            vec,
        ],
        out_specs=[
            pl.BlockSpec((tl, c), lambda i: (i, 0)),
            pl.BlockSpec((tl, c), lambda i: (i, 0)),
        ],
        out_shape=[
            jax.ShapeDtypeStruct((t, c), F32),
            jax.ShapeDtypeStruct((t, c), BF16),
        ],
        scratch_shapes=[
            pltpu.VMEM((SUBLANES, c), F32),
            pltpu.VMEM((1, c), F32),
            pltpu.VMEM((groups, 1, c), F32),
            pltpu.VMEM((tl, c), F32),
            pltpu.VMEM((tl, c), F32),
        ],
        compiler_params=_cparams("arbitrary"),
        name="rg_lru",
    )(u, u, prev_s, h0_s, conv_w, conv_b, w_ra, b_ra, w_ri, b_ri, lam, g_rg)


def _ml_chunk(nseq, heads, dk, dv, q_ref, k_ref, v_ref, o_ref, gc, gr, mprev, gain_ref, mix_ref,
              get_c, get_n, put_state):
    tc = q_ref.shape[0]
    ls = tc // nseq
    shift = int(np.log2(ls))
    t_idx = lax.broadcasted_iota(jnp.int32, (tc, tc), 0)
    s_idx = lax.broadcasted_iota(jnp.int32, (tc, tc), 1)
    mask = (lax.shift_right_logical(t_idx, shift) == lax.shift_right_logical(s_idx, shift)) & (s_idx <= t_idx)
    m_cum = mask.astype(F32)
    bcol = jnp.dot(m_cum, gc, precision=HI, preferred_element_type=F32)
    brow = lax.dot_general(gr, m_cum, (((1,), (1,)), ((), ())), precision=HI,
                           preferred_element_type=F32)
    lane = lax.broadcasted_iota(jnp.int32, (tc, LANES), 1)
    scale = np.float32(dk ** -0.5)

    def seg_last(col):
        if nseq == 1:
            return col[tc - 1:tc, :].reshape(1, 1, 1)
        return col.reshape(nseq, ls, 1)[:, ls - 1:ls, :]

    def seg_bcast(val):
        return jnp.broadcast_to(val, (nseq, ls, 1)).reshape(tc, 1)

    m_out = jnp.zeros((tc, LANES), F32)
    for h in range(heads):
        bc = bcol[:, heads + h:heads + h + 1]
        br = brow[heads + h:heads + h + 1, :]
        igr = gr[h:h + 1, :]
        igc = gc[:, h:h + 1]
        mp = mprev[:, h:h + 1]
        dmat = jnp.where(mask, bc - br + igr, -jnp.inf)
        m_inter = bc + mp
        m_t = jnp.maximum(m_inter, jnp.max(dmat, axis=-1, keepdims=True))
        p = jnp.exp(dmat - m_t)
        qh = q_ref[:, h * dk:(h + 1) * dk] * scale
        kh = k_ref[:, h * dk:(h + 1) * dk]
        vh = v_ref[:, h * dv:(h + 1) * dv]
        qb = qh.astype(BF16)
        kb = kh.astype(BF16)
        s = lax.dot_general(qb, kb, (((1,), (1,)), ((), ())), preferred_element_type=F32) * p
        w_inter = jnp.exp(m_inter - m_t)
        c_old = get_c(h)
        n_old = get_n(h)
        q3 = qh.reshape(nseq, ls, dk)
        inter = lax.dot_general(q3.astype(BF16), c_old.astype(BF16), (((2,), (1,)), ((0,), (0,))),
                                preferred_element_type=F32).reshape(tc, dv)
        qn = jnp.sum(q3 * n_old, axis=-1, keepdims=True).reshape(tc, 1)
        num = jnp.dot(s.astype(BF16), vh.astype(BF16), preferred_element_type=F32) + w_inter * inter
        den = jnp.sum(s, axis=-1, keepdims=True) + w_inter * qn
        hh = num / jnp.maximum(jnp.abs(den), jnp.exp(-m_t))
        hn = hh * lax.rsqrt(jnp.mean(hh * hh, axis=-1, keepdims=True) + EPS)
        og = _sigmoid(o_ref[:, h * dv:(h + 1) * dv])
        mix_ref[:, h * dv:(h + 1) * dv] = (hn * gain_ref[0, :, h * dv:(h + 1) * dv] * og).astype(BF16)

        m_new = seg_last(m_t)
        b_last = seg_last(bc)
        wl = jnp.exp(seg_bcast(b_last) - bc + igc - seg_bcast(m_new))
        decay = jnp.exp(b_last + seg_last(mp) - m_new)
        wv3 = (wl * vh).reshape(nseq, ls, dv).astype(BF16)
        k3 = kh.reshape(nseq, ls, dk).astype(BF16)
        upd = lax.dot_general(k3, wv3, (((1,), (1,)), ((0,), (0,))), preferred_element_type=F32)
        c_new = decay * c_old + upd
        n_new = decay * n_old + jnp.sum((wl * kh).reshape(nseq, ls, dk), axis=1, keepdims=True)
        put_state(h, c_new, n_new)
        m_out = jnp.where(lane == h, seg_bcast(m_new), m_out)
    return m_out


def _ml_kernel(q_ref, k_ref, v_ref, o_ref, gc_ref, gr_ref, cs_ref, ns_ref, ms_ref, gain_ref, cs_alias,
               mix_ref, cp_out, np_out, mp_out, cs_out, ns_out, ms_out, c_scr, n_scr, m_scr,
               *, order, chunks_per_seq, heads, dk, dv, nseq_sample):
    del cs_alias
    i = pl.program_id(0)
    tc = q_ref.shape[0]
    is_prompt = jnp.logical_not(order.is_sample(i))

    @pl.when(is_prompt & (order.prompt_chunk(i) % chunks_per_seq == 0))
    def _():
        c_scr[...] = jnp.zeros_like(c_scr)
        n_scr[...] = jnp.zeros_like(n_scr)
        m_scr[...] = jnp.zeros_like(m_scr)

    @pl.when(is_prompt)
    def _():
        def put(h, c_new, n_new):
            c_scr[h] = c_new[0]
            n_scr[h] = n_new[0]

        mprev = jnp.broadcast_to(m_scr[...], (tc, LANES))
        m_out = _ml_chunk(1, heads, dk, dv, q_ref, k_ref, v_ref, o_ref, gc_ref[...], gr_ref[...], mprev,
                          gain_ref, mix_ref, lambda h: c_scr[h][None], lambda h: n_scr[h][None], put)
        m_scr[...] = m_out[:1, :]
        cp_out[0] = c_scr[...]
        np_out[0] = n_scr[...]
        mp_out[0] = m_out[:1, :]

    @pl.when(jnp.logical_not(is_prompt))
    def _():
        def put(h, c_new, n_new):
            cs_out[0, :, h] = c_new
            ns_out[:, h] = n_new

        m_out = _ml_chunk(nseq_sample, heads, dk, dv, q_ref, k_ref, v_ref, o_ref, gc_ref[...], gr_ref[...],
                          ms_ref[0], gain_ref, mix_ref, lambda h: cs_ref[0, :, h], lambda h: ns_ref[0, :, h], put)
        ms_out[...] = m_out


class _ChunkOrder:
    def __init__(self, n_prompt, n_sample):
        assert n_prompt % n_sample == 0
        self.n_prompt = n_prompt
        self.run = n_prompt // n_sample
        self.period = self.run + 1

    def is_sample(self, i):
        return i % self.period == self.run

    def prompt_chunk(self, i):
        return (i // self.period) * self.run + jnp.minimum(i % self.period, self.run - 1)

    def sample_chunk(self, i):
        return i // self.period

    def written_sample_chunk(self, i):
        return jnp.where(self.is_sample(i), i // self.period, jnp.maximum(i // self.period - 1, 0))

    def row_block(self, i):
        return jnp.where(self.is_sample(i), self.n_prompt + i // self.period, self.prompt_chunk(i))


def _ml_call(u, gcol, grow, c_s, n_s, m_s_col, g_ml, cs_stack, layer, t_prompt, seq, dec_seq,
             q_off, k_off, v_off, o_off, tc):
    t = u.shape[0]
    depth, bs, heads, dk, dv = c_s.shape
    bp = t_prompt // seq
    hdk, hdv = heads * dk, heads * dv
    n_p = t_prompt // tc
    cps = seq // tc
    nseq = tc // dec_seq
    t_s = t - t_prompt
    order = _ChunkOrder(n_p, t_s // tc)
    row = order.row_block
    samp_in = order.sample_chunk
    samp_out = order.written_sample_chunk
    pseq = lambda i: order.prompt_chunk(i) // cps
    in_specs = [
        pl.BlockSpec((tc, hdk), lambda i: (row(i), q_off // hdk)),
        pl.BlockSpec((tc, hdk), lambda i: (row(i), k_off // hdk)),
        pl.BlockSpec((tc, hdv), lambda i: (row(i), v_off // hdv)),
        pl.BlockSpec((tc, hdv), lambda i: (row(i), o_off // hdv)),
        pl.BlockSpec((tc, LANES), lambda i: (row(i), 0)),
        pl.BlockSpec((2 * heads, tc), lambda i: (0, row(i))),
        pl.BlockSpec((1, nseq, heads, dk, dv), lambda i: (layer, samp_in(i), 0, 0, 0)),
        pl.BlockSpec((1, nseq, heads, 1, dk), lambda i: (layer, samp_in(i), 0, 0, 0)),
        pl.BlockSpec((1, tc, LANES), lambda i: (layer, samp_in(i), 0)),
        pl.BlockSpec((1, 1, hdv), lambda i: (layer, 0, 0)),
        pl.BlockSpec(memory_space=pl.ANY),
    ]
    args = [u, u, u, u, gcol, grow, c_s, n_s, m_s_col, g_ml, cs_stack]
    return pl.pallas_call(
        functools.partial(_ml_kernel, order=order, chunks_per_seq=cps, heads=heads, dk=dk, dv=dv,
                          nseq_sample=nseq),
        grid=(t // tc,),
        in_specs=in_specs,
        out_specs=[
            pl.BlockSpec((tc, hdv), lambda i: (row(i), 0)),
            pl.BlockSpec((1, heads, dk, dv), lambda i: (pseq(i), 0, 0, 0)),
            pl.BlockSpec((1, heads, 1, dk), lambda i: (pseq(i), 0, 0, 0)),
            pl.BlockSpec((1, 1, LANES), lambda i: (pseq(i), 0, 0)),
            pl.BlockSpec((1, nseq, heads, dk, dv), lambda i: (layer, samp_out(i), 0, 0, 0)),
            pl.BlockSpec((nseq, heads, 1, dk), lambda i: (samp_out(i), 0, 0, 0)),
            pl.BlockSpec((tc, LANES), lambda i: (samp_out(i), 0)),
        ],
        out_shape=[
            jax.ShapeDtypeStruct((t, hdv), BF16),
            jax.ShapeDtypeStruct((bp, heads, dk, dv), F32),
            jax.ShapeDtypeStruct((bp, heads, 1, dk), F32),
            jax.ShapeDtypeStruct((bp, 1, LANES), F32),
            jax.ShapeDtypeStruct((depth, bs, heads, dk, dv), F32),
            jax.ShapeDtypeStruct((bs, heads, 1, dk), F32),
            jax.ShapeDtypeStruct((t_s, LANES), F32),
        ],
        scratch_shapes=[
            pltpu.VMEM((heads, dk, dv), F32),
            pltpu.VMEM((heads, 1, dk), F32),
            pltpu.VMEM((1, LANES), F32),
        ],
        input_output_aliases={len(args) - 1: 4},
        compiler_params=_cparams("arbitrary"),
        name="mlstm",
    )(*args)


def _out_kernel(x_ref, mr_ref, mm_ref, gtp_ref, gts_ref, shp_ref, shs_ref, scp_ref, scs_ref, wo_hbm, wr_ref, br_ref,
                xn_ref, h2_ref, eid_ref, wt_ref, wo_scr, stage, sem, *, layer, n_groups, per_group, n_prompt_tiles):
    i = pl.program_id(0)
    tm, d = x_ref.shape
    c = mr_ref.shape[1]
    rows = stage.shape[0]

    @pl.when(i == 0)
    def _():
        for part in range(wo_scr.shape[0] // rows):
            cp = pltpu.make_async_copy(wo_hbm.at[layer, pl.ds(part * rows, rows)], stage, sem)
            cp.start()
            cp.wait()
            wo_scr[pl.ds(part * rows, rows), :] = stage[...].astype(BF16)

    is_prompt = i < n_prompt_tiles
    acc = jnp.dot(mr_ref[...], wo_scr[:c, :], preferred_element_type=F32)
    acc = acc + jnp.dot(mm_ref[...], wo_scr[c:, :], preferred_element_type=F32)
    xn3 = _rows3(x_ref[...]) + _pick_mod(is_prompt, gtp_ref, gts_ref) * _rows3(acc)
    rs = lax.rsqrt(jnp.mean(xn3 * xn3, axis=-1, keepdims=True) + EPS)
    h2 = (xn3 * rs * (1.0 + _pick_mod(is_prompt, scp_ref, scs_ref))
          + _pick_mod(is_prompt, shp_ref, shs_ref)).reshape(tm, d)
    xn_ref[...] = xn3.reshape(tm, d)
    _store_slabs(h2_ref, h2)

    logits = jnp.dot(h2.astype(BF16), wr_ref[0], preferred_element_type=F32) + br_ref[0]
    lane = lax.broadcasted_iota(jnp.int32, logits.shape, 1)
    lane_f = lane.astype(F32)
    big = np.float32(LANES)

    def first_max(vals):
        vmax = jnp.max(vals, axis=-1, keepdims=True)
        idx = jnp.min(jnp.where(vals == vmax, lane_f, big), axis=-1, keepdims=True)
        return vmax, idx.astype(jnp.int32)

    gl = jnp.where(lane < n_groups, logits, -jnp.inf)
    gmax, g_sel = first_max(gl)
    p_g = 1.0 / jnp.sum(jnp.exp(gl - gmax), axis=-1, keepdims=True)
    lo = n_groups + g_sel * per_group
    el = jnp.where((lane >= lo) & (lane < lo + per_group), logits, -jnp.inf)
    v1, i1 = first_max(el)
    v2, i2 = first_max(jnp.where(lane == i1, -jnp.inf, el))
    e21 = jnp.exp(v2 - v1)
    w1 = 1.0 / (1.0 + e21)
    w2 = e21 * w1
    eid_ref[...] = jnp.where(lane == 0, i1 - n_groups, jnp.where(lane == 1, i2 - n_groups, 0))
    wt_ref[...] = jnp.where(lane == 0, w1 * p_g, jnp.where(lane == 1, w2 * p_g, 0.0))


def _out_call(x, mix_rg, mix_ml, mod_p, mod_s, w_out, w_router, b_router, layer, n_groups, per_group, tok):
    t, d = x.shape
    c = mix_rg.shape[1]
    cm = mix_ml.shape[1]
    tm = tok.tile
    stage_rows = min(256, c + cm)
    return pl.pallas_call(
        functools.partial(_out_kernel, layer=layer, n_groups=n_groups, per_group=per_group,
                          n_prompt_tiles=tok.n_prompt_tiles),
        grid=(t // tm,),
        in_specs=[
            pl.BlockSpec((tm, d), lambda i: (i, 0)),
            pl.BlockSpec((tm, c), lambda i: (i, 0)),
            pl.BlockSpec((tm, cm), lambda i: (i, 0)),
            *tok.mod_specs(layer, GT_A, d),
            *tok.mod_specs(layer, SH_F, d),
            *tok.mod_specs(layer, SC_F, d),
            pl.BlockSpec(memory_space=pl.ANY),
            pl.BlockSpec((1, d, LANES), lambda i: (layer, 0, 0)),
            pl.BlockSpec((1, 1, LANES), lambda i: (layer, 0, 0)),
        ],
        out_specs=[
            pl.BlockSpec((tm, d), lambda i: (i, 0)),
            pl.BlockSpec((tm * _slab_rows(d), LANES), lambda i: (i, 0)),
            pl.BlockSpec((tm, LANES), lambda i: (i, 0)),
            pl.BlockSpec((tm, LANES), lambda i: (i, 0)),
        ],
        out_shape=[
            jax.ShapeDtypeStruct((t, d), F32),
            jax.ShapeDtypeStruct((t * _slab_rows(d), LANES), F32),
            jax.ShapeDtypeStruct((t, LANES), jnp.int32),
            jax.ShapeDtypeStruct((t, LANES), F32),
        ],
        scratch_shapes=[
            pltpu.VMEM((c + cm, d), BF16),
            pltpu.VMEM((stage_rows, d), F32),
            pltpu.SemaphoreType.DMA(()),
        ],
        compiler_params=_cparams("arbitrary"),
        name="out_proj_router",
    )(x, mix_rg, mix_ml, mod_p, mod_s, mod_p, mod_s, mod_p, mod_s, w_out, w_router, b_router)


def _slab_rows(d):
    return d // LANES


def _store_slabs(ref, val):
    rows, d = val.shape
    slab = _slab_rows(d)
    for s in range(slab):
        ref[pl.ds(s, rows, stride=slab), :] = val[:, s * LANES:(s + 1) * LANES]


def _load_slabs(ref, rows, d):
    slab = _slab_rows(d)
    return jnp.concatenate([ref[pl.ds(s, rows, stride=slab), :] for s in range(slab)], axis=-1)


def _gather_slabs(idx_ref, base, n_rows, src_hbm, dst, sem, slab, stride=1, offset=0):
    def body(g, carry):
        for s in range(SUBLANES):
            r = g * SUBLANES + s
            tok = idx_ref[base + r * stride + offset]
            pltpu.make_async_copy(src_hbm.at[pl.ds(pl.multiple_of(tok * slab, slab), slab)],
                                  dst.at[pl.ds(pl.multiple_of(r * slab, slab), slab)], sem).start()
        return carry

    lax.fori_loop(0, n_rows // SUBLANES, body, 0)


def _wait_rows(buf, sem):
    pltpu.make_async_copy(buf, buf, sem).wait()


def _moe_kernel(te_ref, nx_ref, nu_ref, src_ref, h2_hbm, w1_hbm, w3_hbm, w2_hbm, y_ref,
                xbuf, st1, st3, st2, w1_scr, w3_scr, w2_scr, slot_ref, gsem, wsem,
                *, layer):
    j = pl.program_id(0)
    nbuf = xbuf.shape[0]
    d = w1_scr.shape[0]
    slab = _slab_rows(d)
    tm = xbuf.shape[1] // slab
    n_used = nu_ref[0]

    def weight_copies(expert, slot):
        return [pltpu.make_async_copy(w_hbm.at[layer, expert], stage.at[slot], wsem.at[slot])
                for w_hbm, stage in ((w1_hbm, st1), (w3_hbm, st3), (w2_hbm, st2))]

    def issue(tile):
        _gather_slabs(src_ref, tile * tm, tm, h2_hbm, xbuf.at[tile % nbuf], gsem.at[tile % nbuf], slab)

    @pl.when(j == 0)
    def _():
        slot_ref[0] = 1
        for cp in weight_copies(te_ref[0], 0):
            cp.start()
        for ahead in range(nbuf - 1):
            @pl.when(ahead < n_used)
            def _():
                issue(ahead)

    @pl.when(j + (nbuf - 1) < n_used)
    def _():
        issue(j + (nbuf - 1))

    @pl.when((j < n_used) & ((j == 0) | (te_ref[j] != te_ref[jnp.maximum(j - 1, 0)])))
    def _():
        slot = 1 - slot_ref[0]
        slot_ref[0] = slot
        for cp in weight_copies(te_ref[j], slot):
            cp.wait()
        w1_scr[...] = st1[slot].astype(BF16)
        w3_scr[...] = st3[slot].astype(BF16)
        w2_scr[...] = st2[slot].astype(BF16)

        @pl.when(nx_ref[j] >= 0)
        def _():
            for cp in weight_copies(nx_ref[j], 1 - slot):
                cp.start()

    @pl.when(j < n_used)
    def _():
        slot = j % nbuf
        _wait_rows(xbuf.at[slot], gsem.at[slot])
        xb = _load_slabs(xbuf.at[slot], tm, d).astype(BF16)
        a = jnp.dot(xb, w1_scr[...], preferred_element_type=F32)
        b = jnp.dot(xb, w3_scr[...], preferred_element_type=F32)
        hid = (a * _sigmoid(a)) * b
        _store_slabs(y_ref, jnp.dot(hid.astype(BF16), w2_scr[...], preferred_element_type=F32))

    @pl.when(j >= n_used)
    def _():
        y_ref[...] = jnp.zeros_like(y_ref)


def _moe_call(tile_exp, next_exp, n_used, src_tok, h2_slabs, w1, w3, w2, layer, tm):
    _, ne, d, f = w1.shape
    slab = _slab_rows(d)
    n_tiles = tile_exp.shape[0]
    gather_bufs = 3
    grid_spec = pltpu.PrefetchScalarGridSpec(
        num_scalar_prefetch=4,
        grid=(n_tiles,),
        in_specs=[pl.BlockSpec(memory_space=pl.ANY)] * 4,
        out_specs=pl.BlockSpec((tm * slab, LANES), lambda j, *_: (j, 0)),
        scratch_shapes=[
            pltpu.VMEM((gather_bufs, tm * slab, LANES), F32),
            pltpu.VMEM((2, d, f), F32),
            pltpu.VMEM((2, d, f), F32),
            pltpu.VMEM((2, f, d), F32),
            pltpu.VMEM((d, f), BF16),
            pltpu.VMEM((d, f), BF16),
            pltpu.VMEM((f, d), BF16),
            pltpu.SMEM((1,), jnp.int32),
            pltpu.SemaphoreType.DMA((gather_bufs,)),
            pltpu.SemaphoreType.DMA((2,)),
        ],
    )
    return pl.pallas_call(
        functools.partial(_moe_kernel, layer=layer),
        grid_spec=grid_spec,
        out_shape=jax.ShapeDtypeStruct((n_tiles * tm * slab, LANES), F32),
        compiler_params=_cparams("arbitrary"),
        name="experts",
    )(tile_exp, next_exp, n_used, src_tok, h2_slabs, w1, w3, w2)


def _comb_kernel(dest_ref, xn_ref, wt_ref, gtp_ref, gts_ref, *rest, final, heads, n_prompt_tiles):
    if final:
        gf_ref, y_hbm, yp_ref, ys_ref, ybuf, sem = rest
    else:
        shp_ref, shs_ref, scp_ref, scs_ref, wg_ref, bg_ref, y_hbm, xo_ref, h_ref, g_ref, ybuf, sem = rest
    i = pl.program_id(0)
    n = pl.num_programs(0)
    tm, d = xn_ref.shape
    is_prompt = i < n_prompt_tiles

    def issue(tile, slot):
        for k in range(2):
            _gather_slabs(dest_ref, tile * tm * 2, tm, y_hbm, ybuf.at[slot, k], sem.at[slot], _slab_rows(d),
                          stride=2, offset=k)

    @pl.when(i == 0)
    def _():
        issue(0, 0)

    @pl.when(i + 1 < n)
    def _():
        issue(i + 1, (i + 1) % 2)

    slot = i % 2
    for k in range(2):
        _wait_rows(ybuf.at[slot, k], sem.at[slot])
    wt = wt_ref[...]
    y = wt[:, 0:1] * _load_slabs(ybuf.at[slot, 0], tm, d) + wt[:, 1:2] * _load_slabs(ybuf.at[slot, 1], tm, d)
    xo3 = _rows3(xn_ref[...]) + _pick_mod(is_prompt, gtp_ref, gts_ref) * _rows3(y)
    if final:
        xo = xo3.reshape(tm, d)
        yo = xo * lax.rsqrt(jnp.mean(xo * xo, axis=-1, keepdims=True) + EPS) * gf_ref[...]

        @pl.when(is_prompt)
        def _():
            yp_ref[...] = yo

        @pl.when(jnp.logical_not(is_prompt))
        def _():
            ys_ref[...] = yo
    else:
        xo_ref[...] = xo3.reshape(tm, d)
        _norm_gates(xo3, _pick_mod(is_prompt, scp_ref, scs_ref), _pick_mod(is_prompt, shp_ref, shs_ref),
                    wg_ref, bg_ref, h_ref, g_ref, heads)


def _comb_call(dest, xn, wts, mod_p, mod_s, w_gate, b_gate, g_final, y, layer, heads, tok, final):
    t, d = xn.shape
    tm = tok.tile
    full = lambda i, dst: (0, 0)
    in_specs = [
        pl.BlockSpec((tm, d), lambda i, dst: (i, 0)),
        pl.BlockSpec((tm, LANES), lambda i, dst: (i, 0)),
        *tok.mod_specs(layer, GT_F, d),
    ]
    args = [xn, wts, mod_p, mod_s]
    if final:
        n_p = tok.n_prompt_tiles
        in_specs.append(pl.BlockSpec((1, d), full))
        args.append(g_final)
        out_specs = [pl.BlockSpec((tm, d), lambda i, dst: (jnp.minimum(i, n_p - 1), 0)),
                     pl.BlockSpec((tm, d), lambda i, dst: (tok.sample_tile(i), 0))]
        out_shape = [jax.ShapeDtypeStruct((n_p * tm, d), F32), jax.ShapeDtypeStruct((t - n_p * tm, d), F32)]
    else:
        out_specs = [pl.BlockSpec((tm, d), lambda i, dst: (i, 0))]
        out_shape = [jax.ShapeDtypeStruct((t, d), F32)]
        nxt = layer + 1
        in_specs += [
            *tok.mod_specs(nxt, SH_A, d),
            *tok.mod_specs(nxt, SC_A, d),
            pl.BlockSpec((1, d, LANES), lambda i, dst: (nxt, 0, 0)),
            pl.BlockSpec((1, 1, LANES), lambda i, dst: (nxt, 0, 0)),
        ]
        args += [mod_p, mod_s, mod_p, mod_s, w_gate, b_gate]
        out_specs += [pl.BlockSpec((tm, d), lambda i, dst: (i, 0)), pl.BlockSpec((tm, LANES), lambda i, dst: (i, 0))]
        out_shape += [jax.ShapeDtypeStruct((t, d), BF16), jax.ShapeDtypeStruct((t, LANES), F32)]
    in_specs.append(pl.BlockSpec(memory_space=pl.ANY))
    args.append(y)
    grid_spec = pltpu.PrefetchScalarGridSpec(
        num_scalar_prefetch=1,
        grid=(t // tm,),
        in_specs=in_specs,
        out_specs=out_specs,
        scratch_shapes=[pltpu.VMEM((2, 2, tm * _slab_rows(d), LANES), F32), pltpu.SemaphoreType.DMA((2,))],
    )
    return pl.pallas_call(
        functools.partial(_comb_kernel, final=final, heads=heads, n_prompt_tiles=tok.n_prompt_tiles),
        grid_spec=grid_spec,
        out_shape=out_shape,
        compiler_params=_cparams("arbitrary"),
        name="combine",
    )(dest, *args)


def _invert_kernel(dest_ref, src_ref):
    n_pairs = dest_ref.shape[0]
    n_rows = src_ref.shape[0]

    def clear(i, carry):
        for u in range(INVERT_CLEAR_UNROLL):
            src_ref[i * INVERT_CLEAR_UNROLL + u] = 0
        return carry

    lax.fori_loop(0, n_rows // INVERT_CLEAR_UNROLL, clear, 0)

    def place(i, carry):
        for u in range(SUBLANES):
            pair = i * SUBLANES + u
            src_ref[dest_ref[pair]] = lax.shift_right_logical(pair, 1)
        return carry

    lax.fori_loop(0, n_pairs // SUBLANES, place, 0)


def _invert_call(dest, n_rows):
    assert dest.shape[0] % SUBLANES == 0 and n_rows % INVERT_CLEAR_UNROLL == 0
    return pl.pallas_call(
        _invert_kernel,
        in_specs=[pl.BlockSpec(memory_space=pltpu.SMEM)],
        out_specs=pl.BlockSpec(memory_space=pltpu.SMEM),
        out_shape=jax.ShapeDtypeStruct((n_rows,), jnp.int32),
        name="route_invert",
    )(dest)


def _route_plan(eid, n_experts, tm):
    t = eid.shape[0]
    e_flat = eid[:, :2].reshape(-1)
    n_pairs = 2 * t
    n_tiles = -(-n_pairs // tm) + n_experts
    onehot = (e_flat[:, None] == jnp.arange(n_experts, dtype=jnp.int32)[None, :]).astype(jnp.int32)
    csum = jnp.cumsum(onehot, axis=0)
    rank = jnp.sum(csum * onehot, axis=1) - 1
    counts = csum[-1]
    padded = ((counts + tm - 1) // tm) * tm
    ends = jnp.cumsum(padded)
    starts = ends - padded
    dest = (jnp.sum(starts[None, :] * onehot, axis=1) + rank).astype(jnp.int32)
    src_tok = _invert_call(dest, n_tiles * tm)
    n_used = (ends[-1] // tm).astype(jnp.int32)
    tile_id = jnp.minimum(jnp.arange(n_tiles, dtype=jnp.int32), n_used - 1)
    tile_exp = jnp.sum((ends[None, :] <= (tile_id * tm)[:, None]).astype(jnp.int32), axis=1)
    tile_exp = jnp.minimum(tile_exp, n_experts - 1).astype(jnp.int32)
    ids = jnp.arange(n_experts, dtype=jnp.int32)
    later = (ids[None, :] > ids[:, None]) & (padded[None, :] > 0)
    following = jnp.min(jnp.where(later, ids[None, :], n_experts), axis=1)
    following = jnp.where(following == n_experts, -1, following).astype(jnp.int32)
    return tile_exp, following[tile_exp], n_used.reshape(1), src_tok, dest


def _pick_tile(total, pref):
    tile = min(pref, total)
    while total % tile:
        tile //= 2
    return tile


def kernel(x_prompt, x_sample, c_prompt, c_sample, state_rg_conv, state_rg_h, state_mlstm_C, state_mlstm_n,
           state_mlstm_m, w_mod, b_mod, w_in, conv_w, conv_b, w_ra, b_ra, w_ri, b_ri, lam, g_rg, b_ig, b_fg,
           g_ml, w_out, w_grp, b_grp, w_er, b_er, w1, w3, w2, g_final):
    bp, seq, d = x_prompt.shape
    bs, dec_seq, _ = x_sample.shape
    depth = w_mod.shape[0]
    d_rg = conv_w.shape[2]
    conv_width = conv_w.shape[1]
    _, _, heads, dk, dv = state_mlstm_C.shape
    d_ml = heads * dv
    n_groups, per_group = w_er.shape[1], w_er.shape[3]
    n_experts = w1.shape[1]
    t_p, t_s = bp * seq, bs * dec_seq
    t = t_p + t_s
    n_main = 2 * d_rg + 2 * heads * dk + 2 * d_ml
    q_off, k_off, v_off, o_off = 2 * d_rg, 2 * d_rg + heads * dk, 2 * d_rg + 2 * heads * dk, n_main - d_ml
    assert dec_seq == SUBLANES and seq % LANES == 0 and conv_width - 1 <= SUBLANES
    assert 2 * heads <= LANES and n_groups * (per_group + 1) <= LANES
    assert q_off % (heads * dk) == 0 and k_off % (heads * dk) == 0 and v_off % d_ml == 0 and o_off % d_ml == 0

    tok = _Tokens(t_p, t_s, seq, _pick_tile(np.gcd(seq, t_s), 256))
    tok_out = _Tokens(t_p, t_s, seq, _pick_tile(np.gcd(seq, t_s), 512))
    tm_in = _pick_tile(np.gcd(t_p, t_s), 1024)
    tc = LANES
    assert t_s % tc == 0 and seq % tc == 0

    mc = -(-(bp + bs) // SUBLANES) * SUBLANES
    c_all = jnp.zeros((mc, d), F32).at[:bp].set(c_prompt).at[bp:bp + bs].set(c_sample)
    mod = _mod_call(c_all, w_mod, b_mod, _pick_tile(6 * d, 1024))
    mod_p = mod[:, :bp].reshape(depth, bp, 1, 6 * d)
    mod_s = mod[:, bp:bp + bs].reshape(depth, bs, 1, 6 * d)

    w_in_t = jnp.swapaxes(w_in, 1, 2)
    w_gate = jnp.zeros((depth, d, LANES), F32).at[:, :, :2 * heads].set(w_in[:, :, n_main:]).astype(BF16)
    b_gate = jnp.zeros((depth, 1, LANES), F32).at[:, 0, :heads].set(b_ig).at[:, 0, heads:2 * heads].set(b_fg)
    w_ra_b, w_ri_b = w_ra.astype(BF16), w_ri.astype(BF16)
    n_rt = n_groups * (per_group + 1)
    w_router = jnp.concatenate([w_grp, jnp.moveaxis(w_er, 1, 2).reshape(depth, d, n_groups * per_group)], axis=-1)
    w_router = jnp.zeros((depth, d, LANES), F32).at[:, :, :n_rt].set(w_router).astype(BF16)
    b_router = jnp.concatenate([b_grp, b_er.reshape(depth, -1)], axis=-1)
    b_router = jnp.zeros((depth, 1, LANES), F32).at[:, 0, :n_rt].set(b_router)
    vec = lambda p: p.reshape(depth, 1, -1)

    pad_rows = SUBLANES - (conv_width - 1)
    prev_s = jnp.pad(state_rg_conv, ((0, 0), (0, 0), (pad_rows, 0), (0, 0))).reshape(depth, bs * SUBLANES, d_rg)
    h0_s = state_rg_h.reshape(depth, bs, 1, d_rg)
    n_s = state_mlstm_n.reshape(depth, bs, heads, 1, dk)
    m_s_col = jnp.pad(jnp.repeat(state_mlstm_m, dec_seq, axis=1), ((0, 0), (0, 0), (0, LANES - heads)))

    x, h, gcol = _prenorm_call(x_prompt.reshape(t_p, d), x_sample.reshape(t_s, d), mod_p, mod_s, w_gate, b_gate,
                               heads, tok)
    outs_p = [[] for _ in range(5)]
    outs_s = [[] for _ in range(5)]
    cs_stack = jnp.zeros_like(state_mlstm_C)
    for l in range(depth):
        u = _in_call(h, w_in_t, l, n_main, tm_in, _pick_tile(n_main, 1024))
        hr, mix_rg = _rg_call(u, prev_s, h0_s, conv_w, vec(conv_b), w_ra_b, vec(b_ra), w_ri_b, vec(b_ri),
                              vec(lam), vec(g_rg), l, tok)
        grow = gcol[:, :2 * heads].T
        mix_ml, c_p, n_p, m_p, cs_stack, n_so, m_so = _ml_call(
            u, gcol, grow, state_mlstm_C, n_s, m_s_col, vec(g_ml), cs_stack, l, t_p, seq, dec_seq,
            q_off, k_off, v_off, o_off, tc)
        xn, h2, eid, wts = _out_call(x, mix_rg, mix_ml, mod_p, mod_s, w_out, w_router, b_router, l,
                                     n_groups, per_group, tok_out)
        tile_exp, next_exp, n_used, src_tok, dest = _route_plan(eid, n_experts, tok.tile)
        y = _moe_call(tile_exp, next_exp, n_used, src_tok, h2, w1, w3, w2, l, tok.tile)
        final = l == depth - 1
        res = _comb_call(dest, xn, wts, mod_p, mod_s, w_gate, b_gate, g_final[None], y, l, heads, tok, final)
        if final:
            y_p, y_s = res
        else:
            x, h, gcol = res

        tail = conv_width - 1
        outs_p[0].append(jnp.stack([u[(b + 1) * seq - tail:(b + 1) * seq, :d_rg] for b in range(bp)]))
        outs_p[1].append(jnp.stack([hr[(b + 1) * seq - 1] for b in range(bp)]))
        outs_p[2].append(c_p)
        outs_p[3].append(n_p.reshape(bp, heads, dk))
        outs_p[4].append(m_p[:, 0, :heads])
        outs_s[0].append(u[t_p:, :d_rg].reshape(bs, dec_seq, d_rg)[:, dec_seq - tail:])
        outs_s[1].append(hr[t_p:].reshape(bs, dec_seq, d_rg)[:, -1])
        outs_s[3].append(n_so.reshape(bs, heads, dk))
        outs_s[4].append(m_so.reshape(bs, dec_seq, LANES)[:, -1, :heads])

    stack = lambda parts: jnp.stack(parts)
    return (y_p.reshape(bp, seq, d), y_s.reshape(bs, dec_seq, d), *[stack(o) for o in outs_p],
            stack(outs_s[0]), stack(outs_s[1]), cs_stack, stack(outs_s[3]), stack(outs_s[4]))
```

```python
import functools

import numpy as np
import jax
import jax.numpy as jnp
from jax import lax
from jax.experimental import pallas as pl
from jax.experimental.pallas import tpu as pltpu

F32 = jnp.float32
BF16 = jnp.bfloat16
EPS = 1e-6
RG_C = 8.0
SUBLANES = 8
LANES = 128
VMEM_LIMIT = 56 * 1024 * 1024
HI = lax.Precision.HIGHEST
INVERT_CLEAR_UNROLL = 32
EXPERT_TILE_ROWS = (256, 128, 512, 512)
SH_A, SC_A, GT_A, SH_F, SC_F, GT_F = range(6)


def _cparams(*sem):
    return pltpu.CompilerParams(dimension_semantics=sem, vmem_limit_bytes=VMEM_LIMIT)


def _sigmoid(z):
    return 0.5 * jnp.tanh(0.5 * z) + 0.5


def _softplus(z):
    return jnp.maximum(z, 0.0) + jnp.log1p(jnp.exp(-jnp.abs(z)))


def _rows3(x):
    r, c = x.shape
    return x.reshape(r // SUBLANES, SUBLANES, c)


class _Tokens:
    def __init__(self, t_prompt, t_sample, seq, tile):
        self.tile = tile
        self.groups = tile // SUBLANES
        self.n_prompt_tiles = t_prompt // tile
        self.tiles_per_seq = seq // tile
        self.n_prompt_seqs = t_prompt // seq

    def prompt_seq(self, i):
        return jnp.minimum(i // self.tiles_per_seq, self.n_prompt_seqs - 1)

    def sample_tile(self, i):
        return jnp.maximum(i - self.n_prompt_tiles, 0)

    def mod_specs(self, layer, chunk, d, tile_axis=0):
        def prompt_map(*idx):
            return (layer, self.prompt_seq(idx[tile_axis]), 0, chunk)

        def sample_map(*idx):
            return (layer, self.sample_tile(idx[tile_axis]), 0, chunk)

        return [pl.BlockSpec((1, 1, 1, d), prompt_map), pl.BlockSpec((1, self.groups, 1, d), sample_map)]


def _pick_mod(is_prompt, mp_ref, ms_ref):
    return jnp.where(is_prompt, mp_ref[0], ms_ref[0])


def _mod_kernel(c_ref, w_ref, b_ref, o_ref):
    c = c_ref[...]
    s = (c * _sigmoid(c)).astype(BF16)
    o_ref[0] = jnp.dot(s, w_ref[0].astype(BF16), preferred_element_type=F32) + b_ref[0]


def _mod_call(c_all, w_mod, b_mod, tn):
    depth, d, n = w_mod.shape
    mc = c_all.shape[0]
    return pl.pallas_call(
        _mod_kernel,
        grid=(depth, n // tn),
        in_specs=[
            pl.BlockSpec((mc, d), lambda l, j: (0, 0)),
            pl.BlockSpec((1, d, tn), lambda l, j: (l, 0, j)),
            pl.BlockSpec((1, 1, tn), lambda l, j: (l, 0, j)),
        ],
        out_specs=pl.BlockSpec((1, mc, tn), lambda l, j: (l, 0, j)),
        out_shape=jax.ShapeDtypeStruct((depth, mc, n), F32),
        compiler_params=_cparams("arbitrary", "arbitrary"),
        name="mod",
    )(c_all, w_mod, b_mod.reshape(depth, 1, n))


def _norm_gates(x3, sc, sh, wg_ref, bg_ref, h_ref, g_ref, heads):
    groups, _, d = x3.shape
    xn = x3 * lax.rsqrt(jnp.mean(x3 * x3, axis=-1, keepdims=True) + EPS)
    hb = (xn * (1.0 + sc) + sh).reshape(groups * SUBLANES, d).astype(BF16)
    h_ref[...] = hb
    g = jnp.dot(hb, wg_ref[0], preferred_element_type=F32) + bg_ref[0]
    lane = lax.broadcasted_iota(jnp.int32, g.shape, 1)
    is_forget = (lane >= heads) & (lane < 2 * heads)
    g_ref[...] = jnp.where(is_forget, -_softplus(-g), g)


def _prenorm_kernel(xp_ref, xs_ref, shp_ref, shs_ref, scp_ref, scs_ref, wg_ref, bg_ref, x_ref, h_ref, g_ref,
                    *, heads, n_prompt_tiles):
    is_prompt = pl.program_id(0) < n_prompt_tiles
    x = jnp.where(is_prompt, xp_ref[...], xs_ref[...])
    x_ref[...] = x
    _norm_gates(_rows3(x), _pick_mod(is_prompt, scp_ref, scs_ref), _pick_mod(is_prompt, shp_ref, shs_ref),
                wg_ref, bg_ref, h_ref, g_ref, heads)


def _prenorm_call(x_p, x_s, mod_p, mod_s, w_gate, b_gate, heads, tok):
    d = x_p.shape[1]
    t = x_p.shape[0] + x_s.shape[0]
    tm = tok.tile
    n_p = tok.n_prompt_tiles
    return pl.pallas_call(
        functools.partial(_prenorm_kernel, heads=heads, n_prompt_tiles=n_p),
        grid=(t // tm,),
        in_specs=[
            pl.BlockSpec((tm, d), lambda i: (jnp.minimum(i, n_p - 1), 0)),
            pl.BlockSpec((tm, d), lambda i: (tok.sample_tile(i), 0)),
            *tok.mod_specs(0, SH_A, d),
            *tok.mod_specs(0, SC_A, d),
            pl.BlockSpec((1, d, LANES), lambda i: (0, 0, 0)),
            pl.BlockSpec((1, 1, LANES), lambda i: (0, 0, 0)),
        ],
        out_specs=[
            pl.BlockSpec((tm, d), lambda i: (i, 0)),
            pl.BlockSpec((tm, d), lambda i: (i, 0)),
            pl.BlockSpec((tm, LANES), lambda i: (i, 0)),
        ],
        out_shape=[
            jax.ShapeDtypeStruct((t, d), F32),
            jax.ShapeDtypeStruct((t, d), BF16),
            jax.ShapeDtypeStruct((t, LANES), F32),
        ],
        compiler_params=_cparams("arbitrary"),
        name="prenorm",
    )(x_p, x_s, mod_p, mod_s, mod_p, mod_s, w_gate, b_gate)


def _in_kernel(h_ref, w_ref, u_ref, wb_scr):
    @pl.when(pl.program_id(1) == 0)
    def _():
        wb_scr[...] = w_ref[0].astype(BF16)

    u_ref[...] = lax.dot_general(h_ref[...], wb_scr[...], (((1,), (1,)), ((), ())), preferred_element_type=F32)


def _in_call(h, w_in_t, layer, n_main, tm, tn):
    t, d = h.shape
    return pl.pallas_call(
        _in_kernel,
        grid=(n_main // tn, t // tm),
        in_specs=[
            pl.BlockSpec((tm, d), lambda j, i: (i, 0)),
            pl.BlockSpec((1, tn, d), lambda j, i: (layer, j, 0)),
        ],
        out_specs=pl.BlockSpec((tm, tn), lambda j, i: (i, j)),
        out_shape=jax.ShapeDtypeStruct((t, n_main), F32),
        scratch_shapes=[pltpu.VMEM((tn, d), BF16)],
        compiler_params=_cparams("arbitrary", "arbitrary"),
        name="in_proj",
    )(h, w_in_t)


def _rg_kernel(xr_ref, yr_ref, prev_ref, h0_ref, cw_ref, cb_ref, wra_ref, bra_ref, wri_ref, bri_ref,
               lam_ref, gain_ref, hr_ref, mix_ref, xprev_scr, hprev_scr, hstart_scr, a_scr, b_scr,
               *, n_prompt_tiles, tiles_per_seq):
    i = pl.program_id(0)
    tl, c = xr_ref.shape
    groups = tl // SUBLANES
    _, nblk, bw, _ = wra_ref.shape
    is_prompt = i < n_prompt_tiles

    @pl.when(is_prompt & (i % tiles_per_seq == 0))
    def _():
        xprev_scr[...] = jnp.zeros_like(xprev_scr)
        hprev_scr[...] = jnp.zeros_like(hprev_scr)

    x = xr_ref[...]
    prev_prompt = jnp.concatenate([xprev_scr[...], x[:tl - SUBLANES]], axis=0)
    prev = jnp.where(is_prompt, prev_prompt, prev_ref[0])
    xprev_scr[...] = x[tl - SUBLANES:]

    x3 = _rows3(x)
    p3 = _rows3(prev)
    row = lax.broadcasted_iota(jnp.int32, x3.shape, 1)
    cw = cw_ref[0]
    conv_w = cw.shape[0]
    xc = cb_ref[0] + x3 * cw[conv_w - 1:conv_w]
    for dlt in range(1, conv_w):
        shifted = jnp.where(row >= dlt, pltpu.roll(x3, dlt, 1), pltpu.roll(p3, dlt, 1))
        xc = xc + shifted * cw[conv_w - 1 - dlt:conv_w - dlt]

    xc2 = xc.reshape(tl, c)
    xcb = xc2.astype(BF16)
    r_parts, i_parts = [], []
    for nb in range(nblk):
        blk = xcb[:, nb * bw:(nb + 1) * bw]
        r_parts.append(jnp.dot(blk, wra_ref[0, nb], preferred_element_type=F32))
        i_parts.append(jnp.dot(blk, wri_ref[0, nb], preferred_element_type=F32))
    r = _sigmoid(jnp.concatenate(r_parts, axis=-1) + bra_ref[0])
    ig = _sigmoid(jnp.concatenate(i_parts, axis=-1) + bri_ref[0])
    log_a = (-RG_C) * r * _softplus(-lam_ref[0])
    a = jnp.exp(log_a)
    th = jnp.tanh(log_a)
    gap = -2.0 * th / (1.0 - th)
    mult = jnp.where(gap > 0.0, gap * lax.rsqrt(gap), 0.0)
    bx = mult * (ig * xc2)

    av = _rows3(a)
    bv = _rows3(bx)
    for s in (1, 2, 4):
        a_sh = jnp.where(row >= s, pltpu.roll(av, s, 1), 1.0)
        b_sh = jnp.where(row >= s, pltpu.roll(bv, s, 1), 0.0)
        bv = av * b_sh + bv
        av = av * a_sh

    @pl.when(is_prompt)
    def _():
        a_scr[...] = av.reshape(tl, c)
        b_scr[...] = bv.reshape(tl, c)

        def body(g, h):
            hstart_scr[g] = h
            last = g * SUBLANES + (SUBLANES - 1)
            return a_scr[pl.ds(last, 1), :] * h + b_scr[pl.ds(last, 1), :]

        hprev_scr[...] = lax.fori_loop(0, groups, body, hprev_scr[...])

    @pl.when(jnp.logical_not(is_prompt))
    def _():
        hstart_scr[...] = h0_ref[0]

    h3 = av * hstart_scr[...] + bv
    hr = h3.reshape(tl, c)
    hr_ref[...] = hr
    y = yr_ref[...]
    gelu = 0.5 * y * (1.0 + jnp.tanh(np.sqrt(2.0 / np.pi).astype(np.float32) * (y + 0.044715 * (y * y * y))))
    hn = hr * lax.rsqrt(jnp.mean(hr * hr, axis=-1, keepdims=True) + EPS)
    mix_ref[...] = (hn * gain_ref[0] * gelu).astype(BF16)


def _rg_call(u, prev_s, h0_s, conv_w, conv_b, w_ra, b_ra, w_ri, b_ri, lam, g_rg, layer, tok):
    t = u.shape[0]
    _, conv_width, c = conv_w.shape
    _, nblk, bw, _ = w_ra.shape
    tl, groups = tok.tile, tok.groups
    vec = pl.BlockSpec((1, 1, c), lambda i: (layer, 0, 0))
    return pl.pallas_call(
        functools.partial(_rg_kernel, n_prompt_tiles=tok.n_prompt_tiles, tiles_per_seq=tok.tiles_per_seq),
        grid=(t // tl,),
        in_specs=[
            pl.BlockSpec((tl, c), lambda i: (i, 0)),
            pl.BlockSpec((tl, c), lambda i: (i, 1)),
            pl.BlockSpec((1, tl, c), lambda i: (layer, tok.sample_tile(i), 0)),
            pl.BlockSpec((1, groups, 1, c), lambda i: (layer, tok.sample_tile(i), 0, 0)),
            pl.BlockSpec((1, conv_width, c), lambda i: (layer, 0, 0)),
            vec,
            pl.BlockSpec((1, nblk, bw, bw), lambda i: (layer, 0, 0, 0)),
            vec,
            pl.BlockSpec((1, nblk, bw, bw), lambda i: (layer, 0, 0, 0)),
            vec,
            vec,
            vec,
        ],
        out_specs=[
            pl.BlockSpec((tl, c), lambda i: (i, 0)),
            pl.BlockSpec((tl, c), lambda i: (i, 0)),
        ],
        out_shape=[
            jax.ShapeDtypeStruct((t, c), F32),
            jax.ShapeDtypeStruct((t, c), BF16),
        ],
        scratch_shapes=[
            pltpu.VMEM((SUBLANES, c), F32),
            pltpu.VMEM((1, c), F32),
            pltpu.VMEM((groups, 1, c), F32),
            pltpu.VMEM((tl, c), F32),
            pltpu.VMEM((tl, c), F32),
        ],
        compiler_params=_cparams("arbitrary"),
        name="rg_lru",
    )(u, u, prev_s, h0_s, conv_w, conv_b, w_ra, b_ra, w_ri, b_ri, lam, g_rg)


def _ml_chunk(nseq, heads, dk, dv, q_ref, k_ref, v_ref, o_ref, gc, gr, mprev, gain_ref, mix_ref,
              get_c, get_n, put_state):
    tc = q_ref.shape[0]
    ls = tc // nseq
    shift = int(np.log2(ls))
    t_idx = lax.broadcasted_iota(jnp.int32, (tc, tc), 0)
    s_idx = lax.broadcasted_iota(jnp.int32, (tc, tc), 1)
    mask = (lax.shift_right_logical(t_idx, shift) == lax.shift_right_logical(s_idx, shift)) & (s_idx <= t_idx)
    m_cum = mask.astype(F32)
    bcol = jnp.dot(m_cum, gc, precision=HI, preferred_element_type=F32)
    brow = lax.dot_general(gr, m_cum, (((1,), (1,)), ((), ())), precision=HI,
                           preferred_element_type=F32)
    lane = lax.broadcasted_iota(jnp.int32, (tc, LANES), 1)
    scale = np.float32(dk ** -0.5)

    def seg_last(col):
        if nseq == 1:
            return col[tc - 1:tc, :].reshape(1, 1, 1)
        return col.reshape(nseq, ls, 1)[:, ls - 1:ls, :]

    def seg_bcast(val):
        return jnp.broadcast_to(val, (nseq, ls, 1)).reshape(tc, 1)

    m_out = jnp.zeros((tc, LANES), F32)
    for h in range(heads):
        bc = bcol[:, heads + h:heads + h + 1]
        br = brow[heads + h:heads + h + 1, :]
        igr = gr[h:h + 1, :]
        igc = gc[:, h:h + 1]
        mp = mprev[:, h:h + 1]
        dmat = jnp.where(mask, bc - br + igr, -jnp.inf)
        m_inter = bc + mp
        m_t = jnp.maximum(m_inter, jnp.max(dmat, axis=-1, keepdims=True))
        p = jnp.exp(dmat - m_t)
        qh = q_ref[:, h * dk:(h + 1) * dk] * scale
        kh = k_ref[:, h * dk:(h + 1) * dk]
        vh = v_ref[:, h * dv:(h + 1) * dv]
        qb = qh.astype(BF16)
        kb = kh.astype(BF16)
        s = lax.dot_general(qb, kb, (((1,), (1,)), ((), ())), preferred_element_type=F32) * p
        w_inter = jnp.exp(m_inter - m_t)
        c_old = get_c(h)
        n_old = get_n(h)
        q3 = qh.reshape(nseq, ls, dk)
        inter = lax.dot_general(q3.astype(BF16), c_old.astype(BF16), (((2,), (1,)), ((0,), (0,))),
                                preferred_element_type=F32).reshape(tc, dv)
        qn = jnp.sum(q3 * n_old, axis=-1, keepdims=True).reshape(tc, 1)
        num = jnp.dot(s.astype(BF16), vh.astype(BF16), preferred_element_type=F32) + w_inter * inter
        den = jnp.sum(s, axis=-1, keepdims=True) + w_inter * qn
        hh = num / jnp.maximum(jnp.abs(den), jnp.exp(-m_t))
        hn = hh * lax.rsqrt(jnp.mean(hh * hh, axis=-1, keepdims=True) + EPS)
        og = _sigmoid(o_ref[:, h * dv:(h + 1) * dv])
        mix_ref[:, h * dv:(h + 1) * dv] = (hn * gain_ref[0, :, h * dv:(h + 1) * dv] * og).astype(BF16)

        m_new = seg_last(m_t)
        b_last = seg_last(bc)
        wl = jnp.exp(seg_bcast(b_last) - bc + igc - seg_bcast(m_new))
        decay = jnp.exp(b_last + seg_last(mp) - m_new)
        wv3 = (wl * vh).reshape(nseq, ls, dv).astype(BF16)
        k3 = kh.reshape(nseq, ls, dk).astype(BF16)
        upd = lax.dot_general(k3, wv3, (((1,), (1,)), ((0,), (0,))), preferred_element_type=F32)
        c_new = decay * c_old + upd
        n_new = decay * n_old + jnp.sum((wl * kh).reshape(nseq, ls, dk), axis=1, keepdims=True)
        put_state(h, c_new, n_new)
        m_out = jnp.where(lane == h, seg_bcast(m_new), m_out)
    return m_out


def _ml_kernel(q_ref, k_ref, v_ref, o_ref, gc_ref, gr_ref, cs_ref, ns_ref, ms_ref, gain_ref, cs_alias,
               mix_ref, cp_out, np_out, mp_out, cs_out, ns_out, ms_out, c_scr, n_scr, m_scr,
               *, order, chunks_per_seq, heads, dk, dv, nseq_sample):
    del cs_alias
    i = pl.program_id(0)
    tc = q_ref.shape[0]
    is_prompt = jnp.logical_not(order.is_sample(i))

    @pl.when(is_prompt & (order.prompt_chunk(i) % chunks_per_seq == 0))
    def _():
        c_scr[...] = jnp.zeros_like(c_scr)
        n_scr[...] = jnp.zeros_like(n_scr)
        m_scr[...] = jnp.zeros_like(m_scr)

    @pl.when(is_prompt)
    def _():
        def put(h, c_new, n_new):
            c_scr[h] = c_new[0]
            n_scr[h] = n_new[0]

        mprev = jnp.broadcast_to(m_scr[...], (tc, LANES))
        m_out = _ml_chunk(1, heads, dk, dv, q_ref, k_ref, v_ref, o_ref, gc_ref[...], gr_ref[...], mprev,
                          gain_ref, mix_ref, lambda h: c_scr[h][None], lambda h: n_scr[h][None], put)
        m_scr[...] = m_out[:1, :]
        cp_out[0] = c_scr[...]
        np_out[0] = n_scr[...]
        mp_out[0] = m_out[:1, :]

    @pl.when(jnp.logical_not(is_prompt))
    def _():
        def put(h, c_new, n_new):
            cs_out[0, :, h] = c_new
            ns_out[:, h] = n_new

        m_out = _ml_chunk(nseq_sample, heads, dk, dv, q_ref, k_ref, v_ref, o_ref, gc_ref[...], gr_ref[...],
                          ms_ref[0], gain_ref, mix_ref, lambda h: cs_ref[0, :, h], lambda h: ns_ref[0, :, h], put)
        ms_out[...] = m_out


class _ChunkOrder:
    def __init__(self, n_prompt, n_sample):
        self.n_prompt = n_prompt

    def is_sample(self, i):
        return i >= self.n_prompt

    def prompt_chunk(self, i):
        return jnp.minimum(i, self.n_prompt - 1)

    def sample_chunk(self, i):
        return jnp.maximum(i - self.n_prompt, 0)

    written_sample_chunk = sample_chunk

    def row_block(self, i):
        return i


def _ml_call(u, gcol, grow, c_s, n_s, m_s_col, g_ml, cs_stack, layer, t_prompt, seq, dec_seq,
             q_off, k_off, v_off, o_off, tc):
    t = u.shape[0]
    depth, bs, heads, dk, dv = c_s.shape
    bp = t_prompt // seq
    hdk, hdv = heads * dk, heads * dv
    n_p = t_prompt // tc
    cps = seq // tc
    nseq = tc // dec_seq
    t_s = t - t_prompt
    order = _ChunkOrder(n_p, t_s // tc)
    row = order.row_block
    samp_in = order.sample_chunk
    samp_out = order.written_sample_chunk
    pseq = lambda i: order.prompt_chunk(i) // cps
    in_specs = [
        pl.BlockSpec((tc, hdk), lambda i: (row(i), q_off // hdk)),
        pl.BlockSpec((tc, hdk), lambda i: (row(i), k_off // hdk)),
        pl.BlockSpec((tc, hdv), lambda i: (row(i), v_off // hdv)),
        pl.BlockSpec((tc, hdv), lambda i: (row(i), o_off // hdv)),
        pl.BlockSpec((tc, LANES), lambda i: (row(i), 0)),
        pl.BlockSpec((2 * heads, tc), lambda i: (0, row(i))),
        pl.BlockSpec((1, nseq, heads, dk, dv), lambda i: (layer, samp_in(i), 0, 0, 0)),
        pl.BlockSpec((1, nseq, heads, 1, dk), lambda i: (layer, samp_in(i), 0, 0, 0)),
        pl.BlockSpec((1, tc, LANES), lambda i: (layer, samp_in(i), 0)),
        pl.BlockSpec((1, 1, hdv), lambda i: (layer, 0, 0)),
        pl.BlockSpec(memory_space=pl.ANY),
    ]
    args = [u, u, u, u, gcol, grow, c_s, n_s, m_s_col, g_ml, cs_stack]
    return pl.pallas_call(
        functools.partial(_ml_kernel, order=order, chunks_per_seq=cps, heads=heads, dk=dk, dv=dv,
                          nseq_sample=nseq),
        grid=(t // tc,),
        in_specs=in_specs,
        out_specs=[
            pl.BlockSpec((tc, hdv), lambda i: (row(i), 0)),
            pl.BlockSpec((1, heads, dk, dv), lambda i: (pseq(i), 0, 0, 0)),
            pl.BlockSpec((1, heads, 1, dk), lambda i: (pseq(i), 0, 0, 0)),
            pl.BlockSpec((1, 1, LANES), lambda i: (pseq(i), 0, 0)),
            pl.BlockSpec((1, nseq, heads, dk, dv), lambda i: (layer, samp_out(i), 0, 0, 0)),
            pl.BlockSpec((nseq, heads, 1, dk), lambda i: (samp_out(i), 0, 0, 0)),
            pl.BlockSpec((tc, LANES), lambda i: (samp_out(i), 0)),
        ],
        out_shape=[
            jax.ShapeDtypeStruct((t, hdv), BF16),
            jax.ShapeDtypeStruct((bp, heads, dk, dv), F32),
            jax.ShapeDtypeStruct((bp, heads, 1, dk), F32),
            jax.ShapeDtypeStruct((bp, 1, LANES), F32),
            jax.ShapeDtypeStruct((depth, bs, heads, dk, dv), F32),
            jax.ShapeDtypeStruct((bs, heads, 1, dk), F32),
            jax.ShapeDtypeStruct((t_s, LANES), F32),
        ],
        scratch_shapes=[
            pltpu.VMEM((heads, dk, dv), F32),
            pltpu.VMEM((heads, 1, dk), F32),
            pltpu.VMEM((1, LANES), F32),
        ],
        input_output_aliases={len(args) - 1: 4},
        compiler_params=_cparams("arbitrary"),
        name="mlstm",
    )(*args)


def _out_kernel(x_ref, mr_ref, mm_ref, gtp_ref, gts_ref, shp_ref, shs_ref, scp_ref, scs_ref, wo_hbm, wr_ref, br_ref,
                xn_ref, h2_ref, eid_ref, wt_ref, wo_scr, stage, sem, *, layer, n_groups, per_group, n_prompt_tiles):
    i = pl.program_id(0)
    tm, d = x_ref.shape
    c = mr_ref.shape[1]
    rows = stage.shape[0]

    @pl.when(i == 0)
    def _():
        for part in range(wo_scr.shape[0] // rows):
            cp = pltpu.make_async_copy(wo_hbm.at[layer, pl.ds(part * rows, rows)], stage, sem)
            cp.start()
            cp.wait()
            wo_scr[pl.ds(part * rows, rows), :] = stage[...].astype(BF16)

    is_prompt = i < n_prompt_tiles
    acc = jnp.dot(mr_ref[...], wo_scr[:c, :], preferred_element_type=F32)
    acc = acc + jnp.dot(mm_ref[...], wo_scr[c:, :], preferred_element_type=F32)
    xn3 = _rows3(x_ref[...]) + _pick_mod(is_prompt, gtp_ref, gts_ref) * _rows3(acc)
    rs = lax.rsqrt(jnp.mean(xn3 * xn3, axis=-1, keepdims=True) + EPS)
    h2 = (xn3 * rs * (1.0 + _pick_mod(is_prompt, scp_ref, scs_ref))
          + _pick_mod(is_prompt, shp_ref, shs_ref)).reshape(tm, d)
    xn_ref[...] = xn3.reshape(tm, d)
    _store_slabs(h2_ref, h2)

    logits = jnp.dot(h2.astype(BF16), wr_ref[0], preferred_element_type=F32) + br_ref[0]
    lane = lax.broadcasted_iota(jnp.int32, logits.shape, 1)
    lane_f = lane.astype(F32)
    big = np.float32(LANES)

    def first_max(vals):
        vmax = jnp.max(vals, axis=-1, keepdims=True)
        idx = jnp.min(jnp.where(vals == vmax, lane_f, big), axis=-1, keepdims=True)
        return vmax, idx.astype(jnp.int32)

    gl = jnp.where(lane < n_groups, logits, -jnp.inf)
    gmax, g_sel = first_max(gl)
    p_g = 1.0 / jnp.sum(jnp.exp(gl - gmax), axis=-1, keepdims=True)
    lo = n_groups + g_sel * per_group
    el = jnp.where((lane >= lo) & (lane < lo + per_group), logits, -jnp.inf)
    v1, i1 = first_max(el)
    v2, i2 = first_max(jnp.where(lane == i1, -jnp.inf, el))
    e21 = jnp.exp(v2 - v1)
    w1 = 1.0 / (1.0 + e21)
    w2 = e21 * w1
    eid_ref[...] = jnp.where(lane == 0, i1 - n_groups, jnp.where(lane == 1, i2 - n_groups, 0))
    wt_ref[...] = jnp.where(lane == 0, w1 * p_g, jnp.where(lane == 1, w2 * p_g, 0.0))


def _out_call(x, mix_rg, mix_ml, mod_p, mod_s, w_out, w_router, b_router, layer, n_groups, per_group, tok):
    t, d = x.shape
    c = mix_rg.shape[1]
    cm = mix_ml.shape[1]
    tm = tok.tile
    stage_rows = min(256, c + cm)
    return pl.pallas_call(
        functools.partial(_out_kernel, layer=layer, n_groups=n_groups, per_group=per_group,
                          n_prompt_tiles=tok.n_prompt_tiles),
        grid=(t // tm,),
        in_specs=[
            pl.BlockSpec((tm, d), lambda i: (i, 0)),
            pl.BlockSpec((tm, c), lambda i: (i, 0)),
            pl.BlockSpec((tm, cm), lambda i: (i, 0)),
            *tok.mod_specs(layer, GT_A, d),
            *tok.mod_specs(layer, SH_F, d),
            *tok.mod_specs(layer, SC_F, d),
            pl.BlockSpec(memory_space=pl.ANY),
            pl.BlockSpec((1, d, LANES), lambda i: (layer, 0, 0)),
            pl.BlockSpec((1, 1, LANES), lambda i: (layer, 0, 0)),
        ],
        out_specs=[
            pl.BlockSpec((tm, d), lambda i: (i, 0)),
            pl.BlockSpec((tm * _slab_rows(d), LANES), lambda i: (i, 0)),
            pl.BlockSpec((tm, LANES), lambda i: (i, 0)),
            pl.BlockSpec((tm, LANES), lambda i: (i, 0)),
        ],
        out_shape=[
            jax.ShapeDtypeStruct((t, d), F32),
            jax.ShapeDtypeStruct((t * _slab_rows(d), LANES), F32),
            jax.ShapeDtypeStruct((t, LANES), jnp.int32),
            jax.ShapeDtypeStruct((t, LANES), F32),
        ],
        scratch_shapes=[
            pltpu.VMEM((c + cm, d), BF16),
            pltpu.VMEM((stage_rows, d), F32),
            pltpu.SemaphoreType.DMA(()),
        ],
        compiler_params=_cparams("arbitrary"),
        name="out_proj_router",
    )(x, mix_rg, mix_ml, mod_p, mod_s, mod_p, mod_s, mod_p, mod_s, w_out, w_router, b_router)


def _slab_rows(d):
    return d // LANES


def _store_slabs(ref, val):
    rows, d = val.shape
    slab = _slab_rows(d)
    for s in range(slab):
        ref[pl.ds(s, rows, stride=slab), :] = val[:, s * LANES:(s + 1) * LANES]


def _load_slabs(ref, rows, d):
    slab = _slab_rows(d)
    return jnp.concatenate([ref[pl.ds(s, rows, stride=slab), :] for s in range(slab)], axis=-1)


def _gather_slabs(idx_ref, base, n_rows, src_hbm, dst, sem, slab, stride=1, offset=0):
    def body(g, carry):
        for s in range(SUBLANES):
            r = g * SUBLANES + s
            tok = idx_ref[base + r * stride + offset]
            pltpu.make_async_copy(src_hbm.at[pl.ds(pl.multiple_of(tok * slab, slab), slab)],
                                  dst.at[pl.ds(pl.multiple_of(r * slab, slab), slab)], sem).start()
        return carry

    lax.fori_loop(0, n_rows // SUBLANES, body, 0)


def _wait_rows(buf, sem):
    pltpu.make_async_copy(buf, buf, sem).wait()


def _moe_kernel(te_ref, nx_ref, nu_ref, src_ref, h2_hbm, w1_hbm, w3_hbm, w2_hbm, y_ref,
                xbuf, st1, st3, st2, w1_scr, w3_scr, w2_scr, slot_ref, gsem, wsem,
                *, layer):
    j = pl.program_id(0)
    nbuf = xbuf.shape[0]
    d = w1_scr.shape[0]
    slab = _slab_rows(d)
    tm = xbuf.shape[1] // slab
    n_used = nu_ref[0]

    def weight_copies(expert, slot):
        return [pltpu.make_async_copy(w_hbm.at[layer, expert], stage.at[slot], wsem.at[slot])
                for w_hbm, stage in ((w1_hbm, st1), (w3_hbm, st3), (w2_hbm, st2))]

    def issue(tile):
        _gather_slabs(src_ref, tile * tm, tm, h2_hbm, xbuf.at[tile % nbuf], gsem.at[tile % nbuf], slab)

    @pl.when(j == 0)
    def _():
        slot_ref[0] = 1
        for cp in weight_copies(te_ref[0], 0):
            cp.start()
        for ahead in range(nbuf - 1):
            @pl.when(ahead < n_used)
            def _():
                issue(ahead)

    @pl.when(j + (nbuf - 1) < n_used)
    def _():
        issue(j + (nbuf - 1))

    @pl.when((j < n_used) & ((j == 0) | (te_ref[j] != te_ref[jnp.maximum(j - 1, 0)])))
    def _():
        slot = 1 - slot_ref[0]
        slot_ref[0] = slot
        for cp in weight_copies(te_ref[j], slot):
            cp.wait()
        w1_scr[...] = st1[slot].astype(BF16)
        w3_scr[...] = st3[slot].astype(BF16)
        w2_scr[...] = st2[slot].astype(BF16)

        @pl.when(nx_ref[j] >= 0)
        def _():
            for cp in weight_copies(nx_ref[j], 1 - slot):
                cp.start()

    @pl.when(j < n_used)
    def _():
        slot = j % nbuf
        _wait_rows(xbuf.at[slot], gsem.at[slot])
        xb = _load_slabs(xbuf.at[slot], tm, d).astype(BF16)
        a = jnp.dot(xb, w1_scr[...], preferred_element_type=F32)
        b = jnp.dot(xb, w3_scr[...], preferred_element_type=F32)
        hid = (a * _sigmoid(a)) * b
        _store_slabs(y_ref, jnp.dot(hid.astype(BF16), w2_scr[...], preferred_element_type=F32))

    @pl.when(j >= n_used)
    def _():
        y_ref[...] = jnp.zeros_like(y_ref)


def _moe_call(tile_exp, next_exp, n_used, src_tok, h2_slabs, w1, w3, w2, layer, tm):
    _, ne, d, f = w1.shape
    slab = _slab_rows(d)
    n_tiles = tile_exp.shape[0]
    gather_bufs = 3 if tm <= 256 else 2
    grid_spec = pltpu.PrefetchScalarGridSpec(
        num_scalar_prefetch=4,
        grid=(n_tiles,),
        in_specs=[pl.BlockSpec(memory_space=pl.ANY)] * 4,
        out_specs=pl.BlockSpec((tm * slab, LANES), lambda j, *_: (j, 0)),
        scratch_shapes=[
            pltpu.VMEM((gather_bufs, tm * slab, LANES), F32),
            pltpu.VMEM((2, d, f), F32),
            pltpu.VMEM((2, d, f), F32),
            pltpu.VMEM((2, f, d), F32),
            pltpu.VMEM((d, f), BF16),
            pltpu.VMEM((d, f), BF16),
            pltpu.VMEM((f, d), BF16),
            pltpu.SMEM((1,), jnp.int32),
            pltpu.SemaphoreType.DMA((gather_bufs,)),
            pltpu.SemaphoreType.DMA((2,)),
        ],
    )
    return pl.pallas_call(
        functools.partial(_moe_kernel, layer=layer),
        grid_spec=grid_spec,
        out_shape=jax.ShapeDtypeStruct((n_tiles * tm * slab, LANES), F32),
        compiler_params=_cparams("arbitrary"),
        name="experts",
    )(tile_exp, next_exp, n_used, src_tok, h2_slabs, w1, w3, w2)


def _comb_kernel(dest_ref, xn_ref, wt_ref, gtp_ref, gts_ref, *rest, final, heads, n_prompt_tiles):
    if final:
        gf_ref, y_hbm, yp_ref, ys_ref, ybuf, sem = rest
    else:
        shp_ref, shs_ref, scp_ref, scs_ref, wg_ref, bg_ref, y_hbm, xo_ref, h_ref, g_ref, ybuf, sem = rest
    i = pl.program_id(0)
    n = pl.num_programs(0)
    tm, d = xn_ref.shape
    is_prompt = i < n_prompt_tiles

    def issue(tile, slot):
        for k in range(2):
            _gather_slabs(dest_ref, tile * tm * 2, tm, y_hbm, ybuf.at[slot, k], sem.at[slot], _slab_rows(d),
                          stride=2, offset=k)

    @pl.when(i == 0)
    def _():
        issue(0, 0)

    @pl.when(i + 1 < n)
    def _():
        issue(i + 1, (i + 1) % 2)

    slot = i % 2
    for k in range(2):
        _wait_rows(ybuf.at[slot, k], sem.at[slot])
    wt = wt_ref[...]
    y = wt[:, 0:1] * _load_slabs(ybuf.at[slot, 0], tm, d) + wt[:, 1:2] * _load_slabs(ybuf.at[slot, 1], tm, d)
    xo3 = _rows3(xn_ref[...]) + _pick_mod(is_prompt, gtp_ref, gts_ref) * _rows3(y)
    if final:
        xo = xo3.reshape(tm, d)
        yo = xo * lax.rsqrt(jnp.mean(xo * xo, axis=-1, keepdims=True) + EPS) * gf_ref[...]

        @pl.when(is_prompt)
        def _():
            yp_ref[...] = yo

        @pl.when(jnp.logical_not(is_prompt))
        def _():
            ys_ref[...] = yo
    else:
        xo_ref[...] = xo3.reshape(tm, d)
        _norm_gates(xo3, _pick_mod(is_prompt, scp_ref, scs_ref), _pick_mod(is_prompt, shp_ref, shs_ref),
                    wg_ref, bg_ref, h_ref, g_ref, heads)


def _comb_call(dest, xn, wts, mod_p, mod_s, w_gate, b_gate, g_final, y, layer, heads, tok, final):
    t, d = xn.shape
    tm = tok.tile
    full = lambda i, dst: (0, 0)
    in_specs = [
        pl.BlockSpec((tm, d), lambda i, dst: (i, 0)),
        pl.BlockSpec((tm, LANES), lambda i, dst: (i, 0)),
        *tok.mod_specs(layer, GT_F, d),
    ]
    args = [xn, wts, mod_p, mod_s]
    if final:
        n_p = tok.n_prompt_tiles
        in_specs.append(pl.BlockSpec((1, d), full))
        args.append(g_final)
        out_specs = [pl.BlockSpec((tm, d), lambda i, dst: (jnp.minimum(i, n_p - 1), 0)),
                     pl.BlockSpec((tm, d), lambda i, dst: (tok.sample_tile(i), 0))]
        out_shape = [jax.ShapeDtypeStruct((n_p * tm, d), F32), jax.ShapeDtypeStruct((t - n_p * tm, d), F32)]
    else:
        out_specs = [pl.BlockSpec((tm, d), lambda i, dst: (i, 0))]
        out_shape = [jax.ShapeDtypeStruct((t, d), F32)]
        nxt = layer + 1
        in_specs += [
            *tok.mod_specs(nxt, SH_A, d),
            *tok.mod_specs(nxt, SC_A, d),
            pl.BlockSpec((1, d, LANES), lambda i, dst: (nxt, 0, 0)),
            pl.BlockSpec((1, 1, LANES), lambda i, dst: (nxt, 0, 0)),
        ]
        args += [mod_p, mod_s, mod_p, mod_s, w_gate, b_gate]
        out_specs += [pl.BlockSpec((tm, d), lambda i, dst: (i, 0)), pl.BlockSpec((tm, LANES), lambda i, dst: (i, 0))]
        out_shape += [jax.ShapeDtypeStruct((t, d), BF16), jax.ShapeDtypeStruct((t, LANES), F32)]
    in_specs.append(pl.BlockSpec(memory_space=pl.ANY))
    args.append(y)
    grid_spec = pltpu.PrefetchScalarGridSpec(
        num_scalar_prefetch=1,
        grid=(t // tm,),
        in_specs=in_specs,
        out_specs=out_specs,
        scratch_shapes=[pltpu.VMEM((2, 2, tm * _slab_rows(d), LANES), F32), pltpu.SemaphoreType.DMA((2,))],
    )
    return pl.pallas_call(
        functools.partial(_comb_kernel, final=final, heads=heads, n_prompt_tiles=tok.n_prompt_tiles),
        grid_spec=grid_spec,
        out_shape=out_shape,
        compiler_params=_cparams("arbitrary"),
        name="combine",
    )(dest, *args)


def _invert_kernel(dest_ref, src_ref):
    n_pairs = dest_ref.shape[0]
    n_rows = src_ref.shape[0]

    def clear(i, carry):
        for u in range(INVERT_CLEAR_UNROLL):
            src_ref[i * INVERT_CLEAR_UNROLL + u] = 0
        return carry

    lax.fori_loop(0, n_rows // INVERT_CLEAR_UNROLL, clear, 0)

    def place(i, carry):
        for u in range(SUBLANES):
            pair = i * SUBLANES + u
            src_ref[dest_ref[pair]] = lax.shift_right_logical(pair, 1)
        return carry

    lax.fori_loop(0, n_pairs // SUBLANES, place, 0)


def _invert_call(dest, n_rows):
    assert dest.shape[0] % SUBLANES == 0 and n_rows % INVERT_CLEAR_UNROLL == 0
    return pl.pallas_call(
        _invert_kernel,
        in_specs=[pl.BlockSpec(memory_space=pltpu.SMEM)],
        out_specs=pl.BlockSpec(memory_space=pltpu.SMEM),
        out_shape=jax.ShapeDtypeStruct((n_rows,), jnp.int32),
        name="route_invert",
    )(dest)


def _route_plan(eid, n_experts, tm):
    t = eid.shape[0]
    e_flat = eid[:, :2].reshape(-1)
    n_pairs = 2 * t
    n_tiles = -(-n_pairs // tm) + n_experts
    onehot = (e_flat[:, None] == jnp.arange(n_experts, dtype=jnp.int32)[None, :]).astype(jnp.int32)
    csum = jnp.cumsum(onehot, axis=0)
    rank = jnp.sum(csum * onehot, axis=1) - 1
    counts = csum[-1]
    padded = ((counts + tm - 1) // tm) * tm
    ends = jnp.cumsum(padded)
    starts = ends - padded
    dest = (jnp.sum(starts[None, :] * onehot, axis=1) + rank).astype(jnp.int32)
    src_tok = _invert_call(dest, n_tiles * tm)
    n_used = (ends[-1] // tm).astype(jnp.int32)
    tile_id = jnp.minimum(jnp.arange(n_tiles, dtype=jnp.int32), n_used - 1)
    tile_exp = jnp.sum((ends[None, :] <= (tile_id * tm)[:, None]).astype(jnp.int32), axis=1)
    tile_exp = jnp.minimum(tile_exp, n_experts - 1).astype(jnp.int32)
    ids = jnp.arange(n_experts, dtype=jnp.int32)
    later = (ids[None, :] > ids[:, None]) & (padded[None, :] > 0)
    following = jnp.min(jnp.where(later, ids[None, :], n_experts), axis=1)
    following = jnp.where(following == n_experts, -1, following).astype(jnp.int32)
    return tile_exp, following[tile_exp], n_used.reshape(1), src_tok, dest


def _pick_tile(total, pref):
    tile = min(pref, total)
    while total % tile:
        tile //= 2
    return tile


def kernel(x_prompt, x_sample, c_prompt, c_sample, state_rg_conv, state_rg_h, state_mlstm_C, state_mlstm_n,
           state_mlstm_m, w_mod, b_mod, w_in, conv_w, conv_b, w_ra, b_ra, w_ri, b_ri, lam, g_rg, b_ig, b_fg,
           g_ml, w_out, w_grp, b_grp, w_er, b_er, w1, w3, w2, g_final):
    bp, seq, d = x_prompt.shape
    bs, dec_seq, _ = x_sample.shape
    depth = w_mod.shape[0]
    d_rg = conv_w.shape[2]
    conv_width = conv_w.shape[1]
    _, _, heads, dk, dv = state_mlstm_C.shape
    d_ml = heads * dv
    n_groups, per_group = w_er.shape[1], w_er.shape[3]
    n_experts = w1.shape[1]
    t_p, t_s = bp * seq, bs * dec_seq
    t = t_p + t_s
    n_main = 2 * d_rg + 2 * heads * dk + 2 * d_ml
    q_off, k_off, v_off, o_off = 2 * d_rg, 2 * d_rg + heads * dk, 2 * d_rg + 2 * heads * dk, n_main - d_ml
    assert dec_seq == SUBLANES and seq % LANES == 0 and conv_width - 1 <= SUBLANES
    assert 2 * heads <= LANES and n_groups * (per_group + 1) <= LANES
    assert q_off % (heads * dk) == 0 and k_off % (heads * dk) == 0 and v_off % d_ml == 0 and o_off % d_ml == 0

    tok = _Tokens(t_p, t_s, seq, _pick_tile(np.gcd(seq, t_s), 256))
    tok_out = _Tokens(t_p, t_s, seq, _pick_tile(np.gcd(seq, t_s), 512))
    tm_in = _pick_tile(np.gcd(t_p, t_s), 1024)
    tc = LANES
    assert t_s % tc == 0 and seq % tc == 0

    mc = -(-(bp + bs) // SUBLANES) * SUBLANES
    c_all = jnp.zeros((mc, d), F32).at[:bp].set(c_prompt).at[bp:bp + bs].set(c_sample)
    mod = _mod_call(c_all, w_mod, b_mod, _pick_tile(6 * d, 1024))
    mod_p = mod[:, :bp].reshape(depth, bp, 1, 6 * d)
    mod_s = mod[:, bp:bp + bs].reshape(depth, bs, 1, 6 * d)

    w_in_t = jnp.swapaxes(w_in, 1, 2)
    w_gate = jnp.zeros((depth, d, LANES), F32).at[:, :, :2 * heads].set(w_in[:, :, n_main:]).astype(BF16)
    b_gate = jnp.zeros((depth, 1, LANES), F32).at[:, 0, :heads].set(b_ig).at[:, 0, heads:2 * heads].set(b_fg)
    w_ra_b, w_ri_b = w_ra.astype(BF16), w_ri.astype(BF16)
    n_rt = n_groups * (per_group + 1)
    w_router = jnp.concatenate([w_grp, jnp.moveaxis(w_er, 1, 2).reshape(depth, d, n_groups * per_group)], axis=-1)
    w_router = jnp.zeros((depth, d, LANES), F32).at[:, :, :n_rt].set(w_router).astype(BF16)
    b_router = jnp.concatenate([b_grp, b_er.reshape(depth, -1)], axis=-1)
    b_router = jnp.zeros((depth, 1, LANES), F32).at[:, 0, :n_rt].set(b_router)
    vec = lambda p: p.reshape(depth, 1, -1)

    pad_rows = SUBLANES - (conv_width - 1)
    prev_s = jnp.pad(state_rg_conv, ((0, 0), (0, 0), (pad_rows, 0), (0, 0))).reshape(depth, bs * SUBLANES, d_rg)
    h0_s = state_rg_h.reshape(depth, bs, 1, d_rg)
    n_s = state_mlstm_n.reshape(depth, bs, heads, 1, dk)
    m_s_col = jnp.pad(jnp.repeat(state_mlstm_m, dec_seq, axis=1), ((0, 0), (0, 0), (0, LANES - heads)))

    x, h, gcol = _prenorm_call(x_prompt.reshape(t_p, d), x_sample.reshape(t_s, d), mod_p, mod_s, w_gate, b_gate,
                               heads, tok)
    outs_p = [[] for _ in range(5)]
    outs_s = [[] for _ in range(5)]
    cs_stack = jnp.zeros_like(state_mlstm_C)
    for l in range(depth):
        u = _in_call(h, w_in_t, l, n_main, tm_in, _pick_tile(n_main, 1024))
        hr, mix_rg = _rg_call(u, prev_s, h0_s, conv_w, vec(conv_b), w_ra_b, vec(b_ra), w_ri_b, vec(b_ri),
                              vec(lam), vec(g_rg), l, tok)
        grow = gcol[:, :2 * heads].T
        mix_ml, c_p, n_p, m_p, cs_stack, n_so, m_so = _ml_call(
            u, gcol, grow, state_mlstm_C, n_s, m_s_col, vec(g_ml), cs_stack, l, t_p, seq, dec_seq,
            q_off, k_off, v_off, o_off, tc)
        xn, h2, eid, wts = _out_call(x, mix_rg, mix_ml, mod_p, mod_s, w_out, w_router, b_router, l,
                                     n_groups, per_group, tok_out)
        tm_e = min(EXPERT_TILE_ROWS[l % len(EXPERT_TILE_ROWS)], tok.tile * 2)
        tile_exp, next_exp, n_used, src_tok, dest = _route_plan(eid, n_experts, tm_e)
        y = _moe_call(tile_exp, next_exp, n_used, src_tok, h2, w1, w3, w2, l, tm_e)
        final = l == depth - 1
        res = _comb_call(dest, xn, wts, mod_p, mod_s, w_gate, b_gate, g_final[None], y, l, heads, tok, final)
        if final:
            y_p, y_s = res
        else:
            x, h, gcol = res

        tail = conv_width - 1
        outs_p[0].append(jnp.stack([u[(b + 1) * seq - tail:(b + 1) * seq, :d_rg] for b in range(bp)]))
        outs_p[1].append(jnp.stack([hr[(b + 1) * seq - 1] for b in range(bp)]))
        outs_p[2].append(c_p)
        outs_p[3].append(n_p.reshape(bp, heads, dk))
        outs_p[4].append(m_p[:, 0, :heads])
        outs_s[0].append(u[t_p:, :d_rg].reshape(bs, dec_seq, d_rg)[:, dec_seq - tail:])
        outs_s[1].append(hr[t_p:].reshape(bs, dec_seq, d_rg)[:, -1])
        outs_s[3].append(n_so.reshape(bs, heads, dk))
        outs_s[4].append(m_so.reshape(bs, dec_seq, LANES)[:, -1, :heads])

    stack = lambda parts: jnp.stack(parts)
    return (y_p.reshape(bp, seq, d), y_s.reshape(bs, dec_seq, d), *[stack(o) for o in outs_p],
            stack(outs_s[0]), stack(outs_s[1]), cs_stack, stack(outs_s[3]), stack(outs_s[4]))
```

```python
import functools

import numpy as np
import jax
import jax.numpy as jnp
from jax import lax
from jax.experimental import pallas as pl
from jax.experimental.pallas import tpu as pltpu

F32 = jnp.float32
BF16 = jnp.bfloat16
EPS = 1e-6
RG_C = 8.0
SUBLANES = 8
LANES = 128
VMEM_LIMIT = 56 * 1024 * 1024
HI = lax.Precision.HIGHEST
INVERT_CLEAR_UNROLL = 32
EXPERT_TILE_ROWS = 128
SH_A, SC_A, GT_A, SH_F, SC_F, GT_F = range(6)


def _cparams(*sem):
    return pltpu.CompilerParams(dimension_semantics=sem, vmem_limit_bytes=VMEM_LIMIT)


def _sigmoid(z):
    return 0.5 * jnp.tanh(0.5 * z) + 0.5


def _softplus(z):
    return jnp.maximum(z, 0.0) + jnp.log1p(jnp.exp(-jnp.abs(z)))


def _rows3(x):
    r, c = x.shape
    return x.reshape(r // SUBLANES, SUBLANES, c)


class _Tokens:
    def __init__(self, t_prompt, t_sample, seq, tile):
        self.tile = tile
        self.groups = tile // SUBLANES
        self.n_prompt_tiles = t_prompt // tile
        self.tiles_per_seq = seq // tile
        self.n_prompt_seqs = t_prompt // seq

    def prompt_seq(self, i):
        return jnp.minimum(i // self.tiles_per_seq, self.n_prompt_seqs - 1)

    def sample_tile(self, i):
        return jnp.maximum(i - self.n_prompt_tiles, 0)

    def mod_specs(self, layer, chunk, d, tile_axis=0):
        def prompt_map(*idx):
            return (layer, self.prompt_seq(idx[tile_axis]), 0, chunk)

        def sample_map(*idx):
            return (layer, self.sample_tile(idx[tile_axis]), 0, chunk)

        return [pl.BlockSpec((1, 1, 1, d), prompt_map), pl.BlockSpec((1, self.groups, 1, d), sample_map)]


def _pick_mod(is_prompt, mp_ref, ms_ref):
    return jnp.where(is_prompt, mp_ref[0], ms_ref[0])


def _mod_kernel(c_ref, w_ref, b_ref, o_ref):
    c = c_ref[...]
    s = (c * _sigmoid(c)).astype(BF16)
    o_ref[0] = jnp.dot(s, w_ref[0].astype(BF16), preferred_element_type=F32) + b_ref[0]


def _mod_call(c_all, w_mod, b_mod, tn):
    depth, d, n = w_mod.shape
    mc = c_all.shape[0]
    return pl.pallas_call(
        _mod_kernel,
        grid=(depth, n // tn),
        in_specs=[
            pl.BlockSpec((mc, d), lambda l, j: (0, 0)),
            pl.BlockSpec((1, d, tn), lambda l, j: (l, 0, j)),
            pl.BlockSpec((1, 1, tn), lambda l, j: (l, 0, j)),
        ],
        out_specs=pl.BlockSpec((1, mc, tn), lambda l, j: (l, 0, j)),
        out_shape=jax.ShapeDtypeStruct((depth, mc, n), F32),
        compiler_params=_cparams("arbitrary", "arbitrary"),
        name="mod",
    )(c_all, w_mod, b_mod.reshape(depth, 1, n))


def _norm_gates(x3, sc, sh, wg_ref, bg_ref, h_ref, g_ref, heads):
    groups, _, d = x3.shape
    xn = x3 * lax.rsqrt(jnp.mean(x3 * x3, axis=-1, keepdims=True) + EPS)
    hb = (xn * (1.0 + sc) + sh).reshape(groups * SUBLANES, d).astype(BF16)
    h_ref[...] = hb
    g = jnp.dot(hb, wg_ref[0], preferred_element_type=F32) + bg_ref[0]
    lane = lax.broadcasted_iota(jnp.int32, g.shape, 1)
    is_forget = (lane >= heads) & (lane < 2 * heads)
    g_ref[...] = jnp.where(is_forget, -_softplus(-g), g)


def _prenorm_kernel(xp_ref, xs_ref, shp_ref, shs_ref, scp_ref, scs_ref, wg_ref, bg_ref, x_ref, h_ref, g_ref,
                    *, heads, n_prompt_tiles):
    is_prompt = pl.program_id(0) < n_prompt_tiles
    x = jnp.where(is_prompt, xp_ref[...], xs_ref[...])
    x_ref[...] = x
    _norm_gates(_rows3(x), _pick_mod(is_prompt, scp_ref, scs_ref), _pick_mod(is_prompt, shp_ref, shs_ref),
                wg_ref, bg_ref, h_ref, g_ref, heads)


def _prenorm_call(x_p, x_s, mod_p, mod_s, w_gate, b_gate, heads, tok):
    d = x_p.shape[1]
    t = x_p.shape[0] + x_s.shape[0]
    tm = tok.tile
    n_p = tok.n_prompt_tiles
    return pl.pallas_call(
        functools.partial(_prenorm_kernel, heads=heads, n_prompt_tiles=n_p),
        grid=(t // tm,),
        in_specs=[
            pl.BlockSpec((tm, d), lambda i: (jnp.minimum(i, n_p - 1), 0)),
            pl.BlockSpec((tm, d), lambda i: (tok.sample_tile(i), 0)),
            *tok.mod_specs(0, SH_A, d),
            *tok.mod_specs(0, SC_A, d),
            pl.BlockSpec((1, d, LANES), lambda i: (0, 0, 0)),
            pl.BlockSpec((1, 1, LANES), lambda i: (0, 0, 0)),
        ],
        out_specs=[
            pl.BlockSpec((tm, d), lambda i: (i, 0)),
            pl.BlockSpec((tm, d), lambda i: (i, 0)),
            pl.BlockSpec((tm, LANES), lambda i: (i, 0)),
        ],
        out_shape=[
            jax.ShapeDtypeStruct((t, d), F32),
            jax.ShapeDtypeStruct((t, d), BF16),
            jax.ShapeDtypeStruct((t, LANES), F32),
        ],
        compiler_params=_cparams("arbitrary"),
        name="prenorm",
    )(x_p, x_s, mod_p, mod_s, mod_p, mod_s, w_gate, b_gate)


def _in_kernel(h_ref, w_ref, u_ref, wb_scr):
    @pl.when(pl.program_id(1) == 0)
    def _():
        wb_scr[...] = w_ref[0].astype(BF16)

    u_ref[...] = lax.dot_general(h_ref[...], wb_scr[...], (((1,), (1,)), ((), ())), preferred_element_type=F32)


def _in_call(h, w_in_t, layer, n_main, tm, tn):
    t, d = h.shape
    return pl.pallas_call(
        _in_kernel,
        grid=(n_main // tn, t // tm),
        in_specs=[
            pl.BlockSpec((tm, d), lambda j, i: (i, 0)),
            pl.BlockSpec((1, tn, d), lambda j, i: (layer, j, 0)),
        ],
        out_specs=pl.BlockSpec((tm, tn), lambda j, i: (i, j)),
        out_shape=jax.ShapeDtypeStruct((t, n_main), F32),
        scratch_shapes=[pltpu.VMEM((tn, d), BF16)],
        compiler_params=_cparams("arbitrary", "arbitrary"),
        name="in_proj",
    )(h, w_in_t)


def _rg_kernel(xr_ref, yr_ref, prev_ref, h0_ref, cw_ref, cb_ref, wra_ref, bra_ref, wri_ref, bri_ref,
               lam_ref, gain_ref, hr_ref, mix_ref, xprev_scr, hprev_scr, hstart_scr, a_scr, b_scr,
               *, n_prompt_tiles, tiles_per_seq):
    i = pl.program_id(0)
    tl, c = xr_ref.shape
    groups = tl // SUBLANES
    _, nblk, bw, _ = wra_ref.shape
    is_prompt = i < n_prompt_tiles

    @pl.when(is_prompt & (i % tiles_per_seq == 0))
    def _():
        xprev_scr[...] = jnp.zeros_like(xprev_scr)
        hprev_scr[...] = jnp.zeros_like(hprev_scr)

    x = xr_ref[...]
    prev_prompt = jnp.concatenate([xprev_scr[...], x[:tl - SUBLANES]], axis=0)
    prev = jnp.where(is_prompt, prev_prompt, prev_ref[0])
    xprev_scr[...] = x[tl - SUBLANES:]

    x3 = _rows3(x)
    p3 = _rows3(prev)
    row = lax.broadcasted_iota(jnp.int32, x3.shape, 1)
    cw = cw_ref[0]
    conv_w = cw.shape[0]
    xc = cb_ref[0] + x3 * cw[conv_w - 1:conv_w]
    for dlt in range(1, conv_w):
        shifted = jnp.where(row >= dlt, pltpu.roll(x3, dlt, 1), pltpu.roll(p3, dlt, 1))
        xc = xc + shifted * cw[conv_w - 1 - dlt:conv_w - dlt]

    xc2 = xc.reshape(tl, c)
    xcb = xc2.astype(BF16)
    r_parts, i_parts = [], []
    for nb in range(nblk):
        blk = xcb[:, nb * bw:(nb + 1) * bw]
        r_parts.append(jnp.dot(blk, wra_ref[0, nb], preferred_element_type=F32))
        i_parts.append(jnp.dot(blk, wri_ref[0, nb], preferred_element_type=F32))
    r = _sigmoid(jnp.concatenate(r_parts, axis=-1) + bra_ref[0])
    ig = _sigmoid(jnp.concatenate(i_parts, axis=-1) + bri_ref[0])
    log_a = (-RG_C) * r * _softplus(-lam_ref[0])
    a = jnp.exp(log_a)
    th = jnp.tanh(log_a)
    gap = -2.0 * th / (1.0 - th)
    mult = jnp.where(gap > 0.0, gap * lax.rsqrt(gap), 0.0)
    bx = mult * (ig * xc2)

    av = _rows3(a)
    bv = _rows3(bx)
    for s in (1, 2, 4):
        a_sh = jnp.where(row >= s, pltpu.roll(av, s, 1), 1.0)
        b_sh = jnp.where(row >= s, pltpu.roll(bv, s, 1), 0.0)
        bv = av * b_sh + bv
        av = av * a_sh

    @pl.when(is_prompt)
    def _():
        a_scr[...] = av.reshape(tl, c)
        b_scr[...] = bv.reshape(tl, c)

        def body(g, h):
            hstart_scr[g] = h
            last = g * SUBLANES + (SUBLANES - 1)
            return a_scr[pl.ds(last, 1), :] * h + b_scr[pl.ds(last, 1), :]

        hprev_scr[...] = lax.fori_loop(0, groups, body, hprev_scr[...])

    @pl.when(jnp.logical_not(is_prompt))
    def _():
        hstart_scr[...] = h0_ref[0]

    h3 = av * hstart_scr[...] + bv
    hr = h3.reshape(tl, c)
    hr_ref[...] = hr
    y = yr_ref[...]
    gelu = 0.5 * y * (1.0 + jnp.tanh(np.sqrt(2.0 / np.pi).astype(np.float32) * (y + 0.044715 * (y * y * y))))
    hn = hr * lax.rsqrt(jnp.mean(hr * hr, axis=-1, keepdims=True) + EPS)
    mix_ref[...] = (hn * gain_ref[0] * gelu).astype(BF16)


def _rg_call(u, prev_s, h0_s, conv_w, conv_b, w_ra, b_ra, w_ri, b_ri, lam, g_rg, layer, tok):
    t = u.shape[0]
    _, conv_width, c = conv_w.shape
    _, nblk, bw, _ = w_ra.shape
    tl, groups = tok.tile, tok.groups
    vec = pl.BlockSpec((1, 1, c), lambda i: (layer, 0, 0))
    return pl.pallas_call(
        functools.partial(_rg_kernel, n_prompt_tiles=tok.n_prompt_tiles, tiles_per_seq=tok.tiles_per_seq),
        grid=(t // tl,),
        in_specs=[
            pl.BlockSpec((tl, c), lambda i: (i, 0)),
            pl.BlockSpec((tl, c), lambda i: (i, 1)),
            pl.BlockSpec((1, tl, c), lambda i: (layer, tok.sample_tile(i), 0)),
            pl.BlockSpec((1, groups, 1, c), lambda i: (layer, tok.sample_tile(i), 0, 0)),
            pl.BlockSpec((1, conv_width, c), lambda i: (layer, 0, 0)),
            vec,
            pl.BlockSpec((1, nblk, bw, bw), lambda i: (layer, 0, 0, 0)),
            vec,
            pl.BlockSpec((1, nblk, bw, bw), lambda i: (layer, 0, 0, 0)),
            vec,
            vec,
            vec,
        ],
        out_specs=[
            pl.BlockSpec((tl, c), lambda i: (i, 0)),
            pl.BlockSpec((tl, c), lambda i: (i, 0)),
        ],
        out_shape=[
            jax.ShapeDtypeStruct((t, c), F32),
            jax.ShapeDtypeStruct((t, c), BF16),
        ],
        scratch_shapes=[
            pltpu.VMEM((SUBLANES, c), F32),
            pltpu.VMEM((1, c), F32),
            pltpu.VMEM((groups, 1, c), F32),
            pltpu.VMEM((tl, c), F32),
            pltpu.VMEM((tl, c), F32),
        ],
        compiler_params=_cparams("arbitrary"),
        name="rg_lru",
    )(u, u, prev_s, h0_s, conv_w, conv_b, w_ra, b_ra, w_ri, b_ri, lam, g_rg)


def _ml_chunk(nseq, heads, dk, dv, q_ref, k_ref, v_ref, o_ref, gc, gr, mprev, gain_ref, mix_ref,
              get_c, get_n, put_state):
    tc = q_ref.shape[0]
    ls = tc // nseq
    shift = int(np.log2(ls))
    t_idx = lax.broadcasted_iota(jnp.int32, (tc, tc), 0)
    s_idx = lax.broadcasted_iota(jnp.int32, (tc, tc), 1)
    mask = (lax.shift_right_logical(t_idx, shift) == lax.shift_right_logical(s_idx, shift)) & (s_idx <= t_idx)
    m_cum = mask.astype(F32)
    bcol = jnp.dot(m_cum, gc, precision=HI, preferred_element_type=F32)
    brow = lax.dot_general(gr, m_cum, (((1,), (1,)), ((), ())), precision=HI,
                           preferred_element_type=F32)
    lane = lax.broadcasted_iota(jnp.int32, (tc, LANES), 1)
    scale = np.float32(dk ** -0.5)

    def seg_last(col):
        if nseq == 1:
            return col[tc - 1:tc, :].reshape(1, 1, 1)
        return col.reshape(nseq, ls, 1)[:, ls - 1:ls, :]

    def seg_bcast(val):
        return jnp.broadcast_to(val, (nseq, ls, 1)).reshape(tc, 1)

    m_out = jnp.zeros((tc, LANES), F32)
    for h in range(heads):
        bc = bcol[:, heads + h:heads + h + 1]
        br = brow[heads + h:heads + h + 1, :]
        igr = gr[h:h + 1, :]
        igc = gc[:, h:h + 1]
        mp = mprev[:, h:h + 1]
        dmat = jnp.where(mask, bc - br + igr, -jnp.inf)
        m_inter = bc + mp
        m_t = jnp.maximum(m_inter, jnp.max(dmat, axis=-1, keepdims=True))
        p = jnp.exp(dmat - m_t)
        qh = q_ref[:, h * dk:(h + 1) * dk] * scale
        kh = k_ref[:, h * dk:(h + 1) * dk]
        vh = v_ref[:, h * dv:(h + 1) * dv]
        qb = qh.astype(BF16)
        kb = kh.astype(BF16)
        s = lax.dot_general(qb, kb, (((1,), (1,)), ((), ())), preferred_element_type=F32) * p
        w_inter = jnp.exp(m_inter - m_t)
        c_old = get_c(h)
        n_old = get_n(h)
        q3 = qh.reshape(nseq, ls, dk)
        inter = lax.dot_general(q3.astype(BF16), c_old.astype(BF16), (((2,), (1,)), ((0,), (0,))),
                                preferred_element_type=F32).reshape(tc, dv)
        qn = jnp.sum(q3 * n_old, axis=-1, keepdims=True).reshape(tc, 1)
        num = jnp.dot(s.astype(BF16), vh.astype(BF16), preferred_element_type=F32) + w_inter * inter
        den = jnp.sum(s, axis=-1, keepdims=True) + w_inter * qn
        hh = num / jnp.maximum(jnp.abs(den), jnp.exp(-m_t))
        hn = hh * lax.rsqrt(jnp.mean(hh * hh, axis=-1, keepdims=True) + EPS)
        og = _sigmoid(o_ref[:, h * dv:(h + 1) * dv])
        mix_ref[:, h * dv:(h + 1) * dv] = (hn * gain_ref[0, :, h * dv:(h + 1) * dv] * og).astype(BF16)

        m_new = seg_last(m_t)
        b_last = seg_last(bc)
        wl = jnp.exp(seg_bcast(b_last) - bc + igc - seg_bcast(m_new))
        decay = jnp.exp(b_last + seg_last(mp) - m_new)
        wv3 = (wl * vh).reshape(nseq, ls, dv).astype(BF16)
        k3 = kh.reshape(nseq, ls, dk).astype(BF16)
        upd = lax.dot_general(k3, wv3, (((1,), (1,)), ((0,), (0,))), preferred_element_type=F32)
        c_new = decay * c_old + upd
        n_new = decay * n_old + jnp.sum((wl * kh).reshape(nseq, ls, dk), axis=1, keepdims=True)
        put_state(h, c_new, n_new)
        m_out = jnp.where(lane == h, seg_bcast(m_new), m_out)
    return m_out


def _ml_kernel(q_ref, k_ref, v_ref, o_ref, gc_ref, gr_ref, cs_ref, ns_ref, ms_ref, gain_ref, cs_alias,
               mix_ref, cp_out, np_out, mp_out, cs_out, ns_out, ms_out, c_scr, n_scr, m_scr,
               *, order, chunks_per_seq, heads, dk, dv, nseq_sample):
    del cs_alias
    i = pl.program_id(0)
    tc = q_ref.shape[0]
    is_prompt = jnp.logical_not(order.is_sample(i))

    @pl.when(is_prompt & (order.prompt_chunk(i) % chunks_per_seq == 0))
    def _():
        c_scr[...] = jnp.zeros_like(c_scr)
        n_scr[...] = jnp.zeros_like(n_scr)
        m_scr[...] = jnp.zeros_like(m_scr)

    @pl.when(is_prompt)
    def _():
        def put(h, c_new, n_new):
            c_scr[h] = c_new[0]
            n_scr[h] = n_new[0]

        mprev = jnp.broadcast_to(m_scr[...], (tc, LANES))
        m_out = _ml_chunk(1, heads, dk, dv, q_ref, k_ref, v_ref, o_ref, gc_ref[...], gr_ref[...], mprev,
                          gain_ref, mix_ref, lambda h: c_scr[h][None], lambda h: n_scr[h][None], put)
        m_scr[...] = m_out[:1, :]
        cp_out[0] = c_scr[...]
        np_out[0] = n_scr[...]
        mp_out[0] = m_out[:1, :]

    @pl.when(jnp.logical_not(is_prompt))
    def _():
        def put(h, c_new, n_new):
            cs_out[0, :, h] = c_new
            ns_out[:, h] = n_new

        m_out = _ml_chunk(nseq_sample, heads, dk, dv, q_ref, k_ref, v_ref, o_ref, gc_ref[...], gr_ref[...],
                          ms_ref[0], gain_ref, mix_ref, lambda h: cs_ref[0, :, h], lambda h: ns_ref[0, :, h], put)
        ms_out[...] = m_out


class _ChunkOrder:
    def __init__(self, n_prompt, n_sample):
        self.n_prompt = n_prompt

    def is_sample(self, i):
        return i >= self.n_prompt

    def prompt_chunk(self, i):
        return jnp.minimum(i, self.n_prompt - 1)

    def sample_chunk(self, i):
        return jnp.maximum(i - self.n_prompt, 0)

    written_sample_chunk = sample_chunk

    def row_block(self, i):
        return i


def _ml_call(u, gcol, grow, c_s, n_s, m_s_col, g_ml, cs_stack, layer, t_prompt, seq, dec_seq,
             q_off, k_off, v_off, o_off, tc):
    t = u.shape[0]
    depth, bs, heads, dk, dv = c_s.shape
    bp = t_prompt // seq
    hdk, hdv = heads * dk, heads * dv
    n_p = t_prompt // tc
    cps = seq // tc
    nseq = tc // dec_seq
    t_s = t - t_prompt
    order = _ChunkOrder(n_p, t_s // tc)
    row = order.row_block
    samp_in = order.sample_chunk
    samp_out = order.written_sample_chunk
    pseq = lambda i: order.prompt_chunk(i) // cps
    in_specs = [
        pl.BlockSpec((tc, hdk), lambda i: (row(i), q_off // hdk)),
        pl.BlockSpec((tc, hdk), lambda i: (row(i), k_off // hdk)),
        pl.BlockSpec((tc, hdv), lambda i: (row(i), v_off // hdv)),
        pl.BlockSpec((tc, hdv), lambda i: (row(i), o_off // hdv)),
        pl.BlockSpec((tc, LANES), lambda i: (row(i), 0)),
        pl.BlockSpec((2 * heads, tc), lambda i: (0, row(i))),
        pl.BlockSpec((1, nseq, heads, dk, dv), lambda i: (layer, samp_in(i), 0, 0, 0)),
        pl.BlockSpec((1, nseq, heads, 1, dk), lambda i: (layer, samp_in(i), 0, 0, 0)),
        pl.BlockSpec((1, tc, LANES), lambda i: (layer, samp_in(i), 0)),
        pl.BlockSpec((1, 1, hdv), lambda i: (layer, 0, 0)),
        pl.BlockSpec(memory_space=pl.ANY),
    ]
    args = [u, u, u, u, gcol, grow, c_s, n_s, m_s_col, g_ml, cs_stack]
    return pl.pallas_call(
        functools.partial(_ml_kernel, order=order, chunks_per_seq=cps, heads=heads, dk=dk, dv=dv,
                          nseq_sample=nseq),
        grid=(t // tc,),
        in_specs=in_specs,
        out_specs=[
            pl.BlockSpec((tc, hdv), lambda i: (row(i), 0)),
            pl.BlockSpec((1, heads, dk, dv), lambda i: (pseq(i), 0, 0, 0)),
            pl.BlockSpec((1, heads, 1, dk), lambda i: (pseq(i), 0, 0, 0)),
            pl.BlockSpec((1, 1, LANES), lambda i: (pseq(i), 0, 0)),
            pl.BlockSpec((1, nseq, heads, dk, dv), lambda i: (layer, samp_out(i), 0, 0, 0)),
            pl.BlockSpec((nseq, heads, 1, dk), lambda i: (samp_out(i), 0, 0, 0)),
            pl.BlockSpec((tc, LANES), lambda i: (samp_out(i), 0)),
        ],
        out_shape=[
            jax.ShapeDtypeStruct((t, hdv), BF16),
            jax.ShapeDtypeStruct((bp, heads, dk, dv), F32),
            jax.ShapeDtypeStruct((bp, heads, 1, dk), F32),
            jax.ShapeDtypeStruct((bp, 1, LANES), F32),
            jax.ShapeDtypeStruct((depth, bs, heads, dk, dv), F32),
            jax.ShapeDtypeStruct((bs, heads, 1, dk), F32),
            jax.ShapeDtypeStruct((t_s, LANES), F32),
        ],
        scratch_shapes=[
            pltpu.VMEM((heads, dk, dv), F32),
            pltpu.VMEM((heads, 1, dk), F32),
            pltpu.VMEM((1, LANES), F32),
        ],
        input_output_aliases={len(args) - 1: 4},
        compiler_params=_cparams("arbitrary"),
        name="mlstm",
    )(*args)


def _out_kernel(x_ref, mr_ref, mm_ref, gtp_ref, gts_ref, shp_ref, shs_ref, scp_ref, scs_ref, wo_hbm, wr_ref, br_ref,
                xn_ref, h2_ref, eid_ref, wt_ref, wo_scr, stage, sem, *, layer, n_groups, per_group, n_prompt_tiles):
    i = pl.program_id(0)
    tm, d = x_ref.shape
    c = mr_ref.shape[1]
    rows = stage.shape[0]

    @pl.when(i == 0)
    def _():
        for part in range(wo_scr.shape[0] // rows):
            cp = pltpu.make_async_copy(wo_hbm.at[layer, pl.ds(part * rows, rows)], stage, sem)
            cp.start()
            cp.wait()
            wo_scr[pl.ds(part * rows, rows), :] = stage[...].astype(BF16)

    is_prompt = i < n_prompt_tiles
    acc = jnp.dot(mr_ref[...], wo_scr[:c, :], preferred_element_type=F32)
    acc = acc + jnp.dot(mm_ref[...], wo_scr[c:, :], preferred_element_type=F32)
    xn3 = _rows3(x_ref[...]) + _pick_mod(is_prompt, gtp_ref, gts_ref) * _rows3(acc)
    rs = lax.rsqrt(jnp.mean(xn3 * xn3, axis=-1, keepdims=True) + EPS)
    h2 = (xn3 * rs * (1.0 + _pick_mod(is_prompt, scp_ref, scs_ref))
          + _pick_mod(is_prompt, shp_ref, shs_ref)).reshape(tm, d)
    xn_ref[...] = xn3.reshape(tm, d)
    _store_slabs(h2_ref, h2)

    logits = jnp.dot(h2.astype(BF16), wr_ref[0], preferred_element_type=F32) + br_ref[0]
    lane = lax.broadcasted_iota(jnp.int32, logits.shape, 1)
    lane_f = lane.astype(F32)
    big = np.float32(LANES)

    def first_max(vals):
        vmax = jnp.max(vals, axis=-1, keepdims=True)
        idx = jnp.min(jnp.where(vals == vmax, lane_f, big), axis=-1, keepdims=True)
        return vmax, idx.astype(jnp.int32)

    gl = jnp.where(lane < n_groups, logits, -jnp.inf)
    gmax, g_sel = first_max(gl)
    p_g = 1.0 / jnp.sum(jnp.exp(gl - gmax), axis=-1, keepdims=True)
    lo = n_groups + g_sel * per_group
    el = jnp.where((lane >= lo) & (lane < lo + per_group), logits, -jnp.inf)
    v1, i1 = first_max(el)
    v2, i2 = first_max(jnp.where(lane == i1, -jnp.inf, el))
    e21 = jnp.exp(v2 - v1)
    w1 = 1.0 / (1.0 + e21)
    w2 = e21 * w1
    eid_ref[...] = jnp.where(lane == 0, i1 - n_groups, jnp.where(lane == 1, i2 - n_groups, 0))
    wt_ref[...] = jnp.where(lane == 0, w1 * p_g, jnp.where(lane == 1, w2 * p_g, 0.0))


def _out_call(x, mix_rg, mix_ml, mod_p, mod_s, w_out, w_router, b_router, layer, n_groups, per_group, tok):
    t, d = x.shape
    c = mix_rg.shape[1]
    cm = mix_ml.shape[1]
    tm = tok.tile
    stage_rows = min(256, c + cm)
    return pl.pallas_call(
        functools.partial(_out_kernel, layer=layer, n_groups=n_groups, per_group=per_group,
                          n_prompt_tiles=tok.n_prompt_tiles),
        grid=(t // tm,),
        in_specs=[
            pl.BlockSpec((tm, d), lambda i: (i, 0)),
            pl.BlockSpec((tm, c), lambda i: (i, 0)),
            pl.BlockSpec((tm, cm), lambda i: (i, 0)),
            *tok.mod_specs(layer, GT_A, d),
            *tok.mod_specs(layer, SH_F, d),
            *tok.mod_specs(layer, SC_F, d),
            pl.BlockSpec(memory_space=pl.ANY),
            pl.BlockSpec((1, d, LANES), lambda i: (layer, 0, 0)),
            pl.BlockSpec((1, 1, LANES), lambda i: (layer, 0, 0)),
        ],
        out_specs=[
            pl.BlockSpec((tm, d), lambda i: (i, 0)),
            pl.BlockSpec((tm * _slab_rows(d), LANES), lambda i: (i, 0)),
            pl.BlockSpec((tm, LANES), lambda i: (i, 0)),
            pl.BlockSpec((tm, LANES), lambda i: (i, 0)),
        ],
        out_shape=[
            jax.ShapeDtypeStruct((t, d), F32),
            jax.ShapeDtypeStruct((t * _slab_rows(d), LANES), F32),
            jax.ShapeDtypeStruct((t, LANES), jnp.int32),
            jax.ShapeDtypeStruct((t, LANES), F32),
        ],
        scratch_shapes=[
            pltpu.VMEM((c + cm, d), BF16),
            pltpu.VMEM((stage_rows, d), F32),
            pltpu.SemaphoreType.DMA(()),
        ],
        compiler_params=_cparams("arbitrary"),
        name="out_proj_router",
    )(x, mix_rg, mix_ml, mod_p, mod_s, mod_p, mod_s, mod_p, mod_s, w_out, w_router, b_router)


def _slab_rows(d):
    return d // LANES


def _store_slabs(ref, val):
    rows, d = val.shape
    slab = _slab_rows(d)
    for s in range(slab):
        ref[pl.ds(s, rows, stride=slab), :] = val[:, s * LANES:(s + 1) * LANES]


def _load_slabs(ref, rows, d):
    slab = _slab_rows(d)
    return jnp.concatenate([ref[pl.ds(s, rows, stride=slab), :] for s in range(slab)], axis=-1)


def _gather_slabs(idx_ref, base, n_rows, src_hbm, dst, sem, slab, stride=1, offset=0):
    def body(g, carry):
        for s in range(SUBLANES):
            r = g * SUBLANES + s
            tok = idx_ref[base + r * stride + offset]
            pltpu.make_async_copy(src_hbm.at[pl.ds(pl.multiple_of(tok * slab, slab), slab)],
                                  dst.at[pl.ds(pl.multiple_of(r * slab, slab), slab)], sem).start()
        return carry

    lax.fori_loop(0, n_rows // SUBLANES, body, 0)


def _wait_rows(buf, sem):
    pltpu.make_async_copy(buf, buf, sem).wait()


def _moe_kernel(te_ref, nx_ref, nu_ref, src_ref, h2_hbm, w1_hbm, w3_hbm, w2_hbm, y_ref,
                xbuf, st1, st3, st2, w1_scr, w3_scr, w2_scr, slot_ref, gsem, wsem,
                *, layer):
    j = pl.program_id(0)
    nbuf = xbuf.shape[0]
    d = w1_scr.shape[0]
    slab = _slab_rows(d)
    tm = xbuf.shape[1] // slab
    n_used = nu_ref[0]

    def weight_copies(expert, slot):
        return [pltpu.make_async_copy(w_hbm.at[layer, expert], stage.at[slot], wsem.at[slot])
                for w_hbm, stage in ((w1_hbm, st1), (w3_hbm, st3), (w2_hbm, st2))]

    def issue(tile):
        _gather_slabs(src_ref, tile * tm, tm, h2_hbm, xbuf.at[tile % nbuf], gsem.at[tile % nbuf], slab)

    @pl.when(j == 0)
    def _():
        slot_ref[0] = 1
        for cp in weight_copies(te_ref[0], 0):
            cp.start()
        for ahead in range(nbuf - 1):
            @pl.when(ahead < n_used)
            def _():
                issue(ahead)

    @pl.when(j + (nbuf - 1) < n_used)
    def _():
        issue(j + (nbuf - 1))

    @pl.when((j < n_used) & ((j == 0) | (te_ref[j] != te_ref[jnp.maximum(j - 1, 0)])))
    def _():
        slot = 1 - slot_ref[0]
        slot_ref[0] = slot
        for cp in weight_copies(te_ref[j], slot):
            cp.wait()
        w1_scr[...] = st1[slot].astype(BF16)
        w3_scr[...] = st3[slot].astype(BF16)
        w2_scr[...] = st2[slot].astype(BF16)

        @pl.when(nx_ref[j] >= 0)
        def _():
            for cp in weight_copies(nx_ref[j], 1 - slot):
                cp.start()

    @pl.when(j < n_used)
    def _():
        slot = j % nbuf
        _wait_rows(xbuf.at[slot], gsem.at[slot])
        xb = _load_slabs(xbuf.at[slot], tm, d).astype(BF16)
        a = jnp.dot(xb, w1_scr[...], preferred_element_type=F32)
        b = jnp.dot(xb, w3_scr[...], preferred_element_type=F32)
        hid = (a * _sigmoid(a)) * b
        _store_slabs(y_ref, jnp.dot(hid.astype(BF16), w2_scr[...], preferred_element_type=F32))

    @pl.when(j >= n_used)
    def _():
        y_ref[...] = jnp.zeros_like(y_ref)


def _moe_call(tile_exp, next_exp, n_used, src_tok, h2_slabs, w1, w3, w2, layer, tm):
    _, ne, d, f = w1.shape
    slab = _slab_rows(d)
    n_tiles = tile_exp.shape[0]
    gather_bufs = 3
    grid_spec = pltpu.PrefetchScalarGridSpec(
        num_scalar_prefetch=4,
        grid=(n_tiles,),
        in_specs=[pl.BlockSpec(memory_space=pl.ANY)] * 4,
        out_specs=pl.BlockSpec((tm * slab, LANES), lambda j, *_: (j, 0)),
        scratch_shapes=[
            pltpu.VMEM((gather_bufs, tm * slab, LANES), F32),
            pltpu.VMEM((2, d, f), F32),
            pltpu.VMEM((2, d, f), F32),
            pltpu.VMEM((2, f, d), F32),
            pltpu.VMEM((d, f), BF16),
            pltpu.VMEM((d, f), BF16),
            pltpu.VMEM((f, d), BF16),
            pltpu.SMEM((1,), jnp.int32),
            pltpu.SemaphoreType.DMA((gather_bufs,)),
            pltpu.SemaphoreType.DMA((2,)),
        ],
    )
    return pl.pallas_call(
        functools.partial(_moe_kernel, layer=layer),
        grid_spec=grid_spec,
        out_shape=jax.ShapeDtypeStruct((n_tiles * tm * slab, LANES), F32),
        compiler_params=_cparams("arbitrary"),
        name="experts",
    )(tile_exp, next_exp, n_used, src_tok, h2_slabs, w1, w3, w2)


def _comb_kernel(dest_ref, xn_ref, wt_ref, gtp_ref, gts_ref, *rest, final, heads, n_prompt_tiles):
    if final:
        gf_ref, y_hbm, yp_ref, ys_ref, ybuf, sem = rest
    else:
        shp_ref, shs_ref, scp_ref, scs_ref, wg_ref, bg_ref, y_hbm, xo_ref, h_ref, g_ref, ybuf, sem = rest
    i = pl.program_id(0)
    n = pl.num_programs(0)
    tm, d = xn_ref.shape
    is_prompt = i < n_prompt_tiles

    def issue(tile, slot):
        for k in range(2):
            _gather_slabs(dest_ref, tile * tm * 2, tm, y_hbm, ybuf.at[slot, k], sem.at[slot], _slab_rows(d),
                          stride=2, offset=k)

    @pl.when(i == 0)
    def _():
        issue(0, 0)

    @pl.when(i + 1 < n)
    def _():
        issue(i + 1, (i + 1) % 2)

    slot = i % 2
    for k in range(2):
        _wait_rows(ybuf.at[slot, k], sem.at[slot])
    wt = wt_ref[...]
    y = wt[:, 0:1] * _load_slabs(ybuf.at[slot, 0], tm, d) + wt[:, 1:2] * _load_slabs(ybuf.at[slot, 1], tm, d)
    xo3 = _rows3(xn_ref[...]) + _pick_mod(is_prompt, gtp_ref, gts_ref) * _rows3(y)
    if final:
        xo = xo3.reshape(tm, d)
        yo = xo * lax.rsqrt(jnp.mean(xo * xo, axis=-1, keepdims=True) + EPS) * gf_ref[...]

        @pl.when(is_prompt)
        def _():
            yp_ref[...] = yo

        @pl.when(jnp.logical_not(is_prompt))
        def _():
            ys_ref[...] = yo
    else:
        xo_ref[...] = xo3.reshape(tm, d)
        _norm_gates(xo3, _pick_mod(is_prompt, scp_ref, scs_ref), _pick_mod(is_prompt, shp_ref, shs_ref),
                    wg_ref, bg_ref, h_ref, g_ref, heads)


def _comb_call(dest, xn, wts, mod_p, mod_s, w_gate, b_gate, g_final, y, layer, heads, tok, final):
    t, d = xn.shape
    tm = tok.tile
    full = lambda i, dst: (0, 0)
    in_specs = [
        pl.BlockSpec((tm, d), lambda i, dst: (i, 0)),
        pl.BlockSpec((tm, LANES), lambda i, dst: (i, 0)),
        *tok.mod_specs(layer, GT_F, d),
    ]
    args = [xn, wts, mod_p, mod_s]
    if final:
        n_p = tok.n_prompt_tiles
        in_specs.append(pl.BlockSpec((1, d), full))
        args.append(g_final)
        out_specs = [pl.BlockSpec((tm, d), lambda i, dst: (jnp.minimum(i, n_p - 1), 0)),
                     pl.BlockSpec((tm, d), lambda i, dst: (tok.sample_tile(i), 0))]
        out_shape = [jax.ShapeDtypeStruct((n_p * tm, d), F32), jax.ShapeDtypeStruct((t - n_p * tm, d), F32)]
    else:
        out_specs = [pl.BlockSpec((tm, d), lambda i, dst: (i, 0))]
        out_shape = [jax.ShapeDtypeStruct((t, d), F32)]
        nxt = layer + 1
        in_specs += [
            *tok.mod_specs(nxt, SH_A, d),
            *tok.mod_specs(nxt, SC_A, d),
            pl.BlockSpec((1, d, LANES), lambda i, dst: (nxt, 0, 0)),
            pl.BlockSpec((1, 1, LANES), lambda i, dst: (nxt, 0, 0)),
        ]
        args += [mod_p, mod_s, mod_p, mod_s, w_gate, b_gate]
        out_specs += [pl.BlockSpec((tm, d), lambda i, dst: (i, 0)), pl.BlockSpec((tm, LANES), lambda i, dst: (i, 0))]
        out_shape += [jax.ShapeDtypeStruct((t, d), BF16), jax.ShapeDtypeStruct((t, LANES), F32)]
    in_specs.append(pl.BlockSpec(memory_space=pl.ANY))
    args.append(y)
    grid_spec = pltpu.PrefetchScalarGridSpec(
        num_scalar_prefetch=1,
        grid=(t // tm,),
        in_specs=in_specs,
        out_specs=out_specs,
        scratch_shapes=[pltpu.VMEM((2, 2, tm * _slab_rows(d), LANES), F32), pltpu.SemaphoreType.DMA((2,))],
    )
    return pl.pallas_call(
        functools.partial(_comb_kernel, final=final, heads=heads, n_prompt_tiles=tok.n_prompt_tiles),
        grid_spec=grid_spec,
        out_shape=out_shape,
        compiler_params=_cparams("arbitrary"),
        name="combine",
    )(dest, *args)


def _invert_kernel(dest_ref, src_ref):
    n_pairs = dest_ref.shape[0]
    n_rows = src_ref.shape[0]

    def clear(i, carry):
        for u in range(INVERT_CLEAR_UNROLL):
            src_ref[i * INVERT_CLEAR_UNROLL + u] = 0
        return carry

    lax.fori_loop(0, n_rows // INVERT_CLEAR_UNROLL, clear, 0)

    def place(i, carry):
        for u in range(SUBLANES):
            pair = i * SUBLANES + u
            src_ref[dest_ref[pair]] = lax.shift_right_logical(pair, 1)
        return carry

    lax.fori_loop(0, n_pairs // SUBLANES, place, 0)


def _invert_call(dest, n_rows):
    assert dest.shape[0] % SUBLANES == 0 and n_rows % INVERT_CLEAR_UNROLL == 0
    return pl.pallas_call(
        _invert_kernel,
        in_specs=[pl.BlockSpec(memory_space=pltpu.SMEM)],
        out_specs=pl.BlockSpec(memory_space=pltpu.SMEM),
        out_shape=jax.ShapeDtypeStruct((n_rows,), jnp.int32),
        name="route_invert",
    )(dest)


def _route_plan(eid, n_experts, tm):
    t = eid.shape[0]
    e_flat = eid[:, :2].reshape(-1)
    n_pairs = 2 * t
    n_tiles = -(-n_pairs // tm) + n_experts
    onehot = (e_flat[:, None] == jnp.arange(n_experts, dtype=jnp.int32)[None, :]).astype(jnp.int32)
    csum = jnp.cumsum(onehot, axis=0)
    rank = jnp.sum(csum * onehot, axis=1) - 1
    counts = csum[-1]
    padded = ((counts + tm - 1) // tm) * tm
    ends = jnp.cumsum(padded)
    starts = ends - padded
    dest = (jnp.sum(starts[None, :] * onehot, axis=1) + rank).astype(jnp.int32)
    src_tok = _invert_call(dest, n_tiles * tm)
    n_used = (ends[-1] // tm).astype(jnp.int32)
    tile_id = jnp.minimum(jnp.arange(n_tiles, dtype=jnp.int32), n_used - 1)
    tile_exp = jnp.sum((ends[None, :] <= (tile_id * tm)[:, None]).astype(jnp.int32), axis=1)
    tile_exp = jnp.minimum(tile_exp, n_experts - 1).astype(jnp.int32)
    ids = jnp.arange(n_experts, dtype=jnp.int32)
    later = (ids[None, :] > ids[:, None]) & (padded[None, :] > 0)
    following = jnp.min(jnp.where(later, ids[None, :], n_experts), axis=1)
    following = jnp.where(following == n_experts, -1, following).astype(jnp.int32)
    return tile_exp, following[tile_exp], n_used.reshape(1), src_tok, dest


def _pick_tile(total, pref):
    tile = min(pref, total)
    while total % tile:
        tile //= 2
    return tile


def kernel(x_prompt, x_sample, c_prompt, c_sample, state_rg_conv, state_rg_h, state_mlstm_C, state_mlstm_n,
           state_mlstm_m, w_mod, b_mod, w_in, conv_w, conv_b, w_ra, b_ra, w_ri, b_ri, lam, g_rg, b_ig, b_fg,
           g_ml, w_out, w_grp, b_grp, w_er, b_er, w1, w3, w2, g_final):
    bp, seq, d = x_prompt.shape
    bs, dec_seq, _ = x_sample.shape
    depth = w_mod.shape[0]
    d_rg = conv_w.shape[2]
    conv_width = conv_w.shape[1]
    _, _, heads, dk, dv = state_mlstm_C.shape
    d_ml = heads * dv
    n_groups, per_group = w_er.shape[1], w_er.shape[3]
    n_experts = w1.shape[1]
    t_p, t_s = bp * seq, bs * dec_seq
    t = t_p + t_s
    n_main = 2 * d_rg + 2 * heads * dk + 2 * d_ml
    q_off, k_off, v_off, o_off = 2 * d_rg, 2 * d_rg + heads * dk, 2 * d_rg + 2 * heads * dk, n_main - d_ml
    assert dec_seq == SUBLANES and seq % LANES == 0 and conv_width - 1 <= SUBLANES
    assert 2 * heads <= LANES and n_groups * (per_group + 1) <= LANES
    assert q_off % (heads * dk) == 0 and k_off % (heads * dk) == 0 and v_off % d_ml == 0 and o_off % d_ml == 0

    tok = _Tokens(t_p, t_s, seq, _pick_tile(np.gcd(seq, t_s), 256))
    tok_out = _Tokens(t_p, t_s, seq, _pick_tile(np.gcd(seq, t_s), 512))
    tm_in = _pick_tile(np.gcd(t_p, t_s), 1024)
    tc = LANES
    assert t_s % tc == 0 and seq % tc == 0

    mc = -(-(bp + bs) // SUBLANES) * SUBLANES
    c_all = jnp.zeros((mc, d), F32).at[:bp].set(c_prompt).at[bp:bp + bs].set(c_sample)
    mod = _mod_call(c_all, w_mod, b_mod, _pick_tile(6 * d, 1024))
    mod_p = mod[:, :bp].reshape(depth, bp, 1, 6 * d)
    mod_s = mod[:, bp:bp + bs].reshape(depth, bs, 1, 6 * d)

    w_in_t = jnp.swapaxes(w_in, 1, 2)
    w_gate = jnp.zeros((depth, d, LANES), F32).at[:, :, :2 * heads].set(w_in[:, :, n_main:]).astype(BF16)
    b_gate = jnp.zeros((depth, 1, LANES), F32).at[:, 0, :heads].set(b_ig).at[:, 0, heads:2 * heads].set(b_fg)
    w_ra_b, w_ri_b = w_ra.astype(BF16), w_ri.astype(BF16)
    n_rt = n_groups * (per_group + 1)
    w_router = jnp.concatenate([w_grp, jnp.moveaxis(w_er, 1, 2).reshape(depth, d, n_groups * per_group)], axis=-1)
    w_router = jnp.zeros((depth, d, LANES), F32).at[:, :, :n_rt].set(w_router).astype(BF16)
    b_router = jnp.concatenate([b_grp, b_er.reshape(depth, -1)], axis=-1)
    b_router = jnp.zeros((depth, 1, LANES), F32).at[:, 0, :n_rt].set(b_router)
    vec = lambda p: p.reshape(depth, 1, -1)

    pad_rows = SUBLANES - (conv_width - 1)
    prev_s = jnp.pad(state_rg_conv, ((0, 0), (0, 0), (pad_rows, 0), (0, 0))).reshape(depth, bs * SUBLANES, d_rg)
    h0_s = state_rg_h.reshape(depth, bs, 1, d_rg)
    n_s = state_mlstm_n.reshape(depth, bs, heads, 1, dk)
    m_s_col = jnp.pad(jnp.repeat(state_mlstm_m, dec_seq, axis=1), ((0, 0), (0, 0), (0, LANES - heads)))

    x, h, gcol = _prenorm_call(x_prompt.reshape(t_p, d), x_sample.reshape(t_s, d), mod_p, mod_s, w_gate, b_gate,
                               heads, tok)
    outs_p = [[] for _ in range(5)]
    outs_s = [[] for _ in range(5)]
    cs_stack = jnp.zeros_like(state_mlstm_C)
    for l in range(depth):
        u = _in_call(h, w_in_t, l, n_main, tm_in, _pick_tile(n_main, 1024))
        hr, mix_rg = _rg_call(u, prev_s, h0_s, conv_w, vec(conv_b), w_ra_b, vec(b_ra), w_ri_b, vec(b_ri),
                              vec(lam), vec(g_rg), l, tok)
        grow = gcol[:, :2 * heads].T
        mix_ml, c_p, n_p, m_p, cs_stack, n_so, m_so = _ml_call(
            u, gcol, grow, state_mlstm_C, n_s, m_s_col, vec(g_ml), cs_stack, l, t_p, seq, dec_seq,
            q_off, k_off, v_off, o_off, tc)
        xn, h2, eid, wts = _out_call(x, mix_rg, mix_ml, mod_p, mod_s, w_out, w_router, b_router, l,
                                     n_groups, per_group, tok_out)
        tile_exp, next_exp, n_used, src_tok, dest = _route_plan(eid, n_experts, EXPERT_TILE_ROWS)
        y = _moe_call(tile_exp, next_exp, n_used, src_tok, h2, w1, w3, w2, l, EXPERT_TILE_ROWS)
        final = l == depth - 1
        res = _comb_call(dest, xn, wts, mod_p, mod_s, w_gate, b_gate, g_final[None], y, l, heads, tok, final)
        if final:
            y_p, y_s = res
        else:
            x, h, gcol = res

        tail = conv_width - 1
        outs_p[0].append(jnp.stack([u[(b + 1) * seq - tail:(b + 1) * seq, :d_rg] for b in range(bp)]))
        outs_p[1].append(jnp.stack([hr[(b + 1) * seq - 1] for b in range(bp)]))
        outs_p[2].append(c_p)
        outs_p[3].append(n_p.reshape(bp, heads, dk))
        outs_p[4].append(m_p[:, 0, :heads])
        outs_s[0].append(u[t_p:, :d_rg].reshape(bs, dec_seq, d_rg)[:, dec_seq - tail:])
        outs_s[1].append(hr[t_p:].reshape(bs, dec_seq, d_rg)[:, -1])
        outs_s[3].append(n_so.reshape(bs, heads, dk))
        outs_s[4].append(m_so.reshape(bs, dec_seq, LANES)[:, -1, :heads])

    stack = lambda parts: jnp.stack(parts)
    return (y_p.reshape(bp, seq, d), y_s.reshape(bs, dec_seq, d), *[stack(o) for o in outs_p],
            stack(outs_s[0]), stack(outs_s[1]), cs_stack, stack(outs_s[3]), stack(outs_s[4]))
```

```python
import functools

import numpy as np
import jax
import jax.numpy as jnp
from jax import lax
from jax.experimental import pallas as pl
from jax.experimental.pallas import tpu as pltpu

F32 = jnp.float32
BF16 = jnp.bfloat16
EPS = 1e-6
RG_C = 8.0
SUBLANES = 8
LANES = 128
VMEM_LIMIT = 56 * 1024 * 1024
HI = lax.Precision.HIGHEST
INVERT_CLEAR_UNROLL = 32
EXPERT_TILE_ROWS = (128, 64, 192, 128)
SH_A, SC_A, GT_A, SH_F, SC_F, GT_F = range(6)


def _cparams(*sem):
    return pltpu.CompilerParams(dimension_semantics=sem, vmem_limit_bytes=VMEM_LIMIT)


def _sigmoid(z):
    return 0.5 * jnp.tanh(0.5 * z) + 0.5


def _softplus(z):
    return jnp.maximum(z, 0.0) + jnp.log1p(jnp.exp(-jnp.abs(z)))


def _rows3(x):
    r, c = x.shape
    return x.reshape(r // SUBLANES, SUBLANES, c)


class _Tokens:
    def __init__(self, t_prompt, t_sample, seq, tile):
        self.tile = tile
        self.groups = tile // SUBLANES
        self.n_prompt_tiles = t_prompt // tile
        self.tiles_per_seq = seq // tile
        self.n_prompt_seqs = t_prompt // seq

    def prompt_seq(self, i):
        return jnp.minimum(i // self.tiles_per_seq, self.n_prompt_seqs - 1)

    def sample_tile(self, i):
        return jnp.maximum(i - self.n_prompt_tiles, 0)

    def mod_specs(self, layer, chunk, d, tile_axis=0):
        def prompt_map(*idx):
            return (layer, self.prompt_seq(idx[tile_axis]), 0, chunk)

        def sample_map(*idx):
            return (layer, self.sample_tile(idx[tile_axis]), 0, chunk)

        return [pl.BlockSpec((1, 1, 1, d), prompt_map), pl.BlockSpec((1, self.groups, 1, d), sample_map)]


def _pick_mod(is_prompt, mp_ref, ms_ref):
    return jnp.where(is_prompt, mp_ref[0], ms_ref[0])


def _mod_kernel(c_ref, w_ref, b_ref, o_ref):
    c = c_ref[...]
    s = (c * _sigmoid(c)).astype(BF16)
    o_ref[0] = jnp.dot(s, w_ref[0].astype(BF16), preferred_element_type=F32) + b_ref[0]


def _mod_call(c_all, w_mod, b_mod, tn):
    depth, d, n = w_mod.shape
    mc = c_all.shape[0]
    return pl.pallas_call(
        _mod_kernel,
        grid=(depth, n // tn),
        in_specs=[
            pl.BlockSpec((mc, d), lambda l, j: (0, 0)),
            pl.BlockSpec((1, d, tn), lambda l, j: (l, 0, j)),
            pl.BlockSpec((1, 1, tn), lambda l, j: (l, 0, j)),
        ],
        out_specs=pl.BlockSpec((1, mc, tn), lambda l, j: (l, 0, j)),
        out_shape=jax.ShapeDtypeStruct((depth, mc, n), F32),
        compiler_params=_cparams("arbitrary", "arbitrary"),
        name="mod",
    )(c_all, w_mod, b_mod.reshape(depth, 1, n))


def _norm_gates(x3, sc, sh, wg_ref, bg_ref, h_ref, g_ref, heads):
    groups, _, d = x3.shape
    xn = x3 * lax.rsqrt(jnp.mean(x3 * x3, axis=-1, keepdims=True) + EPS)
    hb = (xn * (1.0 + sc) + sh).reshape(groups * SUBLANES, d).astype(BF16)
    h_ref[...] = hb
    g = jnp.dot(hb, wg_ref[0], preferred_element_type=F32) + bg_ref[0]
    lane = lax.broadcasted_iota(jnp.int32, g.shape, 1)
    is_forget = (lane >= heads) & (lane < 2 * heads)
    g_ref[...] = jnp.where(is_forget, -_softplus(-g), g)


def _prenorm_kernel(xp_ref, xs_ref, shp_ref, shs_ref, scp_ref, scs_ref, wg_ref, bg_ref, x_ref, h_ref, g_ref,
                    *, heads, n_prompt_tiles):
    is_prompt = pl.program_id(0) < n_prompt_tiles
    x = jnp.where(is_prompt, xp_ref[...], xs_ref[...])
    x_ref[...] = x
    _norm_gates(_rows3(x), _pick_mod(is_prompt, scp_ref, scs_ref), _pick_mod(is_prompt, shp_ref, shs_ref),
                wg_ref, bg_ref, h_ref, g_ref, heads)


def _prenorm_call(x_p, x_s, mod_p, mod_s, w_gate, b_gate, heads, tok):
    d = x_p.shape[1]
    t = x_p.shape[0] + x_s.shape[0]
    tm = tok.tile
    n_p = tok.n_prompt_tiles
    return pl.pallas_call(
        functools.partial(_prenorm_kernel, heads=heads, n_prompt_tiles=n_p),
        grid=(t // tm,),
        in_specs=[
            pl.BlockSpec((tm, d), lambda i: (jnp.minimum(i, n_p - 1), 0)),
            pl.BlockSpec((tm, d), lambda i: (tok.sample_tile(i), 0)),
            *tok.mod_specs(0, SH_A, d),
            *tok.mod_specs(0, SC_A, d),
            pl.BlockSpec((1, d, LANES), lambda i: (0, 0, 0)),
            pl.BlockSpec((1, 1, LANES), lambda i: (0, 0, 0)),
        ],
        out_specs=[
            pl.BlockSpec((tm, d), lambda i: (i, 0)),
            pl.BlockSpec((tm, d), lambda i: (i, 0)),
            pl.BlockSpec((tm, LANES), lambda i: (i, 0)),
        ],
        out_shape=[
            jax.ShapeDtypeStruct((t, d), F32),
            jax.ShapeDtypeStruct((t, d), BF16),
            jax.ShapeDtypeStruct((t, LANES), F32),
        ],
        compiler_params=_cparams("arbitrary"),
        name="prenorm",
    )(x_p, x_s, mod_p, mod_s, mod_p, mod_s, w_gate, b_gate)


def _in_kernel(h_ref, w_ref, u_ref, wb_scr):
    @pl.when(pl.program_id(1) == 0)
    def _():
        wb_scr[...] = w_ref[0].astype(BF16)

    u_ref[...] = lax.dot_general(h_ref[...], wb_scr[...], (((1,), (1,)), ((), ())), preferred_element_type=F32)


def _in_call(h, w_in_t, layer, n_main, tm, tn):
    t, d = h.shape
    return pl.pallas_call(
        _in_kernel,
        grid=(n_main // tn, t // tm),
        in_specs=[
            pl.BlockSpec((tm, d), lambda j, i: (i, 0)),
            pl.BlockSpec((1, tn, d), lambda j, i: (layer, j, 0)),
        ],
        out_specs=pl.BlockSpec((tm, tn), lambda j, i: (i, j)),
        out_shape=jax.ShapeDtypeStruct((t, n_main), F32),
        scratch_shapes=[pltpu.VMEM((tn, d), BF16)],
        compiler_params=_cparams("arbitrary", "arbitrary"),
        name="in_proj",
    )(h, w_in_t)


def _rg_kernel(xr_ref, yr_ref, prev_ref, h0_ref, cw_ref, cb_ref, wra_ref, bra_ref, wri_ref, bri_ref,
               lam_ref, gain_ref, hr_ref, mix_ref, xprev_scr, hprev_scr, hstart_scr, a_scr, b_scr,
               *, n_prompt_tiles, tiles_per_seq):
    i = pl.program_id(0)
    tl, c = xr_ref.shape
    groups = tl // SUBLANES
    _, nblk, bw, _ = wra_ref.shape
    is_prompt = i < n_prompt_tiles

    @pl.when(is_prompt & (i % tiles_per_seq == 0))
    def _():
        xprev_scr[...] = jnp.zeros_like(xprev_scr)
        hprev_scr[...] = jnp.zeros_like(hprev_scr)

    x = xr_ref[...]
    prev_prompt = jnp.concatenate([xprev_scr[...], x[:tl - SUBLANES]], axis=0)
    prev = jnp.where(is_prompt, prev_prompt, prev_ref[0])
    xprev_scr[...] = x[tl - SUBLANES:]

    x3 = _rows3(x)
    p3 = _rows3(prev)
    row = lax.broadcasted_iota(jnp.int32, x3.shape, 1)
    cw = cw_ref[0]
    conv_w = cw.shape[0]
    xc = cb_ref[0] + x3 * cw[conv_w - 1:conv_w]
    for dlt in range(1, conv_w):
        shifted = jnp.where(row >= dlt, pltpu.roll(x3, dlt, 1), pltpu.roll(p3, dlt, 1))
        xc = xc + shifted * cw[conv_w - 1 - dlt:conv_w - dlt]

    xc2 = xc.reshape(tl, c)
    xcb = xc2.astype(BF16)
    r_parts, i_parts = [], []
    for nb in range(nblk):
        blk = xcb[:, nb * bw:(nb + 1) * bw]
        r_parts.append(jnp.dot(blk, wra_ref[0, nb], preferred_element_type=F32))
        i_parts.append(jnp.dot(blk, wri_ref[0, nb], preferred_element_type=F32))
    r = _sigmoid(jnp.concatenate(r_parts, axis=-1) + bra_ref[0])
    ig = _sigmoid(jnp.concatenate(i_parts, axis=-1) + bri_ref[0])
    log_a = (-RG_C) * r * _softplus(-lam_ref[0])
    a = jnp.exp(log_a)
    th = jnp.tanh(log_a)
    gap = -2.0 * th / (1.0 - th)
    mult = jnp.where(gap > 0.0, gap * lax.rsqrt(gap), 0.0)
    bx = mult * (ig * xc2)

    av = _rows3(a)
    bv = _rows3(bx)
    for s in (1, 2, 4):
        a_sh = jnp.where(row >= s, pltpu.roll(av, s, 1), 1.0)
        b_sh = jnp.where(row >= s, pltpu.roll(bv, s, 1), 0.0)
        bv = av * b_sh + bv
        av = av * a_sh

    @pl.when(is_prompt)
    def _():
        a_scr[...] = av.reshape(tl, c)
        b_scr[...] = bv.reshape(tl, c)

        def body(g, h):
            hstart_scr[g] = h
            last = g * SUBLANES + (SUBLANES - 1)
            return a_scr[pl.ds(last, 1), :] * h + b_scr[pl.ds(last, 1), :]

        hprev_scr[...] = lax.fori_loop(0, groups, body, hprev_scr[...])

    @pl.when(jnp.logical_not(is_prompt))
    def _():
        hstart_scr[...] = h0_ref[0]

    h3 = av * hstart_scr[...] + bv
    hr = h3.reshape(tl, c)
    hr_ref[...] = hr
    y = yr_ref[...]
    gelu = 0.5 * y * (1.0 + jnp.tanh(np.sqrt(2.0 / np.pi).astype(np.float32) * (y + 0.044715 * (y * y * y))))
    hn = hr * lax.rsqrt(jnp.mean(hr * hr, axis=-1, keepdims=True) + EPS)
    mix_ref[...] = (hn * gain_ref[0] * gelu).astype(BF16)


def _rg_call(u, prev_s, h0_s, conv_w, conv_b, w_ra, b_ra, w_ri, b_ri, lam, g_rg, layer, tok):
    t = u.shape[0]
    _, conv_width, c = conv_w.shape
    _, nblk, bw, _ = w_ra.shape
    tl, groups = tok.tile, tok.groups
    vec = pl.BlockSpec((1, 1, c), lambda i: (layer, 0, 0))
    return pl.pallas_call(
        functools.partial(_rg_kernel, n_prompt_tiles=tok.n_prompt_tiles, tiles_per_seq=tok.tiles_per_seq),
        grid=(t // tl,),
        in_specs=[
            pl.BlockSpec((tl, c), lambda i: (i, 0)),
            pl.BlockSpec((tl, c), lambda i: (i, 1)),
            pl.BlockSpec((1, tl, c), lambda i: (layer, tok.sample_tile(i), 0)),
            pl.BlockSpec((1, groups, 1, c), lambda i: (layer, tok.sample_tile(i), 0, 0)),
            pl.BlockSpec((1, conv_width, c), lambda i: (layer, 0, 0)),
            vec,
            pl.BlockSpec((1, nblk, bw, bw), lambda i: (layer, 0, 0, 0)),
            vec,
            pl.BlockSpec((1, nblk, bw, bw), lambda i: (layer, 0, 0, 0)),
            vec,
            vec,
            vec,
        ],
        out_specs=[
            pl.BlockSpec((tl, c), lambda i: (i, 0)),
            pl.BlockSpec((tl, c), lambda i: (i, 0)),
        ],
        out_shape=[
            jax.ShapeDtypeStruct((t, c), F32),
            jax.ShapeDtypeStruct((t, c), BF16),
        ],
        scratch_shapes=[
            pltpu.VMEM((SUBLANES, c), F32),
            pltpu.VMEM((1, c), F32),
            pltpu.VMEM((groups, 1, c), F32),
            pltpu.VMEM((tl, c), F32),
            pltpu.VMEM((tl, c), F32),
        ],
        compiler_params=_cparams("arbitrary"),
        name="rg_lru",
    )(u, u, prev_s, h0_s, conv_w, conv_b, w_ra, b_ra, w_ri, b_ri, lam, g_rg)


def _ml_chunk(nseq, heads, dk, dv, q_ref, k_ref, v_ref, o_ref, gc, gr, mprev, gain_ref, mix_ref,
              get_c, get_n, put_state):
    tc = q_ref.shape[0]
    ls = tc // nseq
    shift = int(np.log2(ls))
    t_idx = lax.broadcasted_iota(jnp.int32, (tc, tc), 0)
    s_idx = lax.broadcasted_iota(jnp.int32, (tc, tc), 1)
    mask = (lax.shift_right_logical(t_idx, shift) == lax.shift_right_logical(s_idx, shift)) & (s_idx <= t_idx)
    m_cum = mask.astype(F32)
    bcol = jnp.dot(m_cum, gc, precision=HI, preferred_element_type=F32)
    brow = lax.dot_general(gr, m_cum, (((1,), (1,)), ((), ())), precision=HI,
                           preferred_element_type=F32)
    lane = lax.broadcasted_iota(jnp.int32, (tc, LANES), 1)
    scale = np.float32(dk ** -0.5)

    def seg_last(col):
        if nseq == 1:
            return col[tc - 1:tc, :].reshape(1, 1, 1)
        return col.reshape(nseq, ls, 1)[:, ls - 1:ls, :]

    def seg_bcast(val):
        return jnp.broadcast_to(val, (nseq, ls, 1)).reshape(tc, 1)

    m_out = jnp.zeros((tc, LANES), F32)
    for h in range(heads):
        bc = bcol[:, heads + h:heads + h + 1]
        br = brow[heads + h:heads + h + 1, :]
        igr = gr[h:h + 1, :]
        igc = gc[:, h:h + 1]
        mp = mprev[:, h:h + 1]
        dmat = jnp.where(mask, bc - br + igr, -jnp.inf)
        m_inter = bc + mp
        m_t = jnp.maximum(m_inter, jnp.max(dmat, axis=-1, keepdims=True))
        p = jnp.exp(dmat - m_t)
        qh = q_ref[:, h * dk:(h + 1) * dk] * scale
        kh = k_ref[:, h * dk:(h + 1) * dk]
        vh = v_ref[:, h * dv:(h + 1) * dv]
        qb = qh.astype(BF16)
        kb = kh.astype(BF16)
        s = lax.dot_general(qb, kb, (((1,), (1,)), ((), ())), preferred_element_type=F32) * p
        w_inter = jnp.exp(m_inter - m_t)
        c_old = get_c(h)
        n_old = get_n(h)
        q3 = qh.reshape(nseq, ls, dk)
        inter = lax.dot_general(q3.astype(BF16), c_old.astype(BF16), (((2,), (1,)), ((0,), (0,))),
                                preferred_element_type=F32).reshape(tc, dv)
        qn = jnp.sum(q3 * n_old, axis=-1, keepdims=True).reshape(tc, 1)
        num = jnp.dot(s.astype(BF16), vh.astype(BF16), preferred_element_type=F32) + w_inter * inter
        den = jnp.sum(s, axis=-1, keepdims=True) + w_inter * qn
        hh = num / jnp.maximum(jnp.abs(den), jnp.exp(-m_t))
        hn = hh * lax.rsqrt(jnp.mean(hh * hh, axis=-1, keepdims=True) + EPS)
        og = _sigmoid(o_ref[:, h * dv:(h + 1) * dv])
        mix_ref[:, h * dv:(h + 1) * dv] = (hn * gain_ref[0, :, h * dv:(h + 1) * dv] * og).astype(BF16)

        m_new = seg_last(m_t)
        b_last = seg_last(bc)
        wl = jnp.exp(seg_bcast(b_last) - bc + igc - seg_bcast(m_new))
        decay = jnp.exp(b_last + seg_last(mp) - m_new)
        wv3 = (wl * vh).reshape(nseq, ls, dv).astype(BF16)
        k3 = kh.reshape(nseq, ls, dk).astype(BF16)
        upd = lax.dot_general(k3, wv3, (((1,), (1,)), ((0,), (0,))), preferred_element_type=F32)
        c_new = decay * c_old + upd
        n_new = decay * n_old + jnp.sum((wl * kh).reshape(nseq, ls, dk), axis=1, keepdims=True)
        put_state(h, c_new, n_new)
        m_out = jnp.where(lane == h, seg_bcast(m_new), m_out)
    return m_out


def _ml_kernel(q_ref, k_ref, v_ref, o_ref, gc_ref, gr_ref, cs_ref, ns_ref, ms_ref, gain_ref, cs_alias,
               mix_ref, cp_out, np_out, mp_out, cs_out, ns_out, ms_out, c_scr, n_scr, m_scr,
               *, n_prompt_chunks, chunks_per_seq, heads, dk, dv, nseq_sample):
    del cs_alias
    i = pl.program_id(0)
    tc = q_ref.shape[0]
    is_prompt = i < n_prompt_chunks

    @pl.when(is_prompt & (i % chunks_per_seq == 0))
    def _():
        c_scr[...] = jnp.zeros_like(c_scr)
        n_scr[...] = jnp.zeros_like(n_scr)
        m_scr[...] = jnp.zeros_like(m_scr)

    @pl.when(is_prompt)
    def _():
        def put(h, c_new, n_new):
            c_scr[h] = c_new[0]
            n_scr[h] = n_new[0]

        mprev = jnp.broadcast_to(m_scr[...], (tc, LANES))
        m_out = _ml_chunk(1, heads, dk, dv, q_ref, k_ref, v_ref, o_ref, gc_ref[...], gr_ref[...], mprev,
                          gain_ref, mix_ref, lambda h: c_scr[h][None], lambda h: n_scr[h][None], put)
        m_scr[...] = m_out[:1, :]
        cp_out[0] = c_scr[...]
        np_out[0] = n_scr[...]
        mp_out[0] = m_out[:1, :]

    @pl.when(jnp.logical_not(is_prompt))
    def _():
        def put(h, c_new, n_new):
            cs_out[0, :, h] = c_new
            ns_out[:, h] = n_new

        m_out = _ml_chunk(nseq_sample, heads, dk, dv, q_ref, k_ref, v_ref, o_ref, gc_ref[...], gr_ref[...],
                          ms_ref[0], gain_ref, mix_ref, lambda h: cs_ref[0, :, h], lambda h: ns_ref[0, :, h], put)
        ms_out[...] = m_out


def _ml_call(u, gcol, grow, c_s, n_s, m_s_col, g_ml, cs_stack, layer, t_prompt, seq, dec_seq,
             q_off, k_off, v_off, o_off, tc):
    t = u.shape[0]
    depth, bs, heads, dk, dv = c_s.shape
    bp = t_prompt // seq
    hdk, hdv = heads * dk, heads * dv
    n_p = t_prompt // tc
    cps = seq // tc
    nseq = tc // dec_seq
    t_s = t - t_prompt
    row = lambda i: i
    samp_in = samp_out = lambda i: jnp.maximum(i - n_p, 0)
    pseq = lambda i: jnp.minimum(i // cps, bp - 1)
    in_specs = [
        pl.BlockSpec((tc, hdk), lambda i: (row(i), q_off // hdk)),
        pl.BlockSpec((tc, hdk), lambda i: (row(i), k_off // hdk)),
        pl.BlockSpec((tc, hdv), lambda i: (row(i), v_off // hdv)),
        pl.BlockSpec((tc, hdv), lambda i: (row(i), o_off // hdv)),
        pl.BlockSpec((tc, LANES), lambda i: (row(i), 0)),
        pl.BlockSpec((2 * heads, tc), lambda i: (0, row(i))),
        pl.BlockSpec((1, nseq, heads, dk, dv), lambda i: (layer, samp_in(i), 0, 0, 0)),
        pl.BlockSpec((1, nseq, heads, 1, dk), lambda i: (layer, samp_in(i), 0, 0, 0)),
        pl.BlockSpec((1, tc, LANES), lambda i: (layer, samp_in(i), 0)),
        pl.BlockSpec((1, 1, hdv), lambda i: (layer, 0, 0)),
        pl.BlockSpec(memory_space=pl.ANY),
    ]
    args = [u, u, u, u, gcol, grow, c_s, n_s, m_s_col, g_ml, cs_stack]
    return pl.pallas_call(
        functools.partial(_ml_kernel, n_prompt_chunks=n_p, chunks_per_seq=cps, heads=heads, dk=dk, dv=dv,
                          nseq_sample=nseq),
        grid=(t // tc,),
        in_specs=in_specs,
        out_specs=[
            pl.BlockSpec((tc, hdv), lambda i: (row(i), 0)),
            pl.BlockSpec((1, heads, dk, dv), lambda i: (pseq(i), 0, 0, 0)),
            pl.BlockSpec((1, heads, 1, dk), lambda i: (pseq(i), 0, 0, 0)),
            pl.BlockSpec((1, 1, LANES), lambda i: (pseq(i), 0, 0)),
            pl.BlockSpec((1, nseq, heads, dk, dv), lambda i: (layer, samp_out(i), 0, 0, 0)),
            pl.BlockSpec((nseq, heads, 1, dk), lambda i: (samp_out(i), 0, 0, 0)),
            pl.BlockSpec((tc, LANES), lambda i: (samp_out(i), 0)),
        ],
        out_shape=[
            jax.ShapeDtypeStruct((t, hdv), BF16),
            jax.ShapeDtypeStruct((bp, heads, dk, dv), F32),
            jax.ShapeDtypeStruct((bp, heads, 1, dk), F32),
            jax.ShapeDtypeStruct((bp, 1, LANES), F32),
            jax.ShapeDtypeStruct((depth, bs, heads, dk, dv), F32),
            jax.ShapeDtypeStruct((bs, heads, 1, dk), F32),
            jax.ShapeDtypeStruct((t_s, LANES), F32),
        ],
        scratch_shapes=[
            pltpu.VMEM((heads, dk, dv), F32),
            pltpu.VMEM((heads, 1, dk), F32),
            pltpu.VMEM((1, LANES), F32),
        ],
        input_output_aliases={len(args) - 1: 4},
        compiler_params=_cparams("arbitrary"),
        name="mlstm",
    )(*args)


def _out_kernel(x_ref, mr_ref, mm_ref, gtp_ref, gts_ref, shp_ref, shs_ref, scp_ref, scs_ref, wo_hbm, wr_ref, br_ref,
                xn_ref, h2_ref, eid_ref, wt_ref, wo_scr, stage, sem, *, layer, n_groups, per_group, n_prompt_tiles):
    i = pl.program_id(0)
    tm, d = x_ref.shape
    c = mr_ref.shape[1]
    rows = stage.shape[0]

    @pl.when(i == 0)
    def _():
        for part in range(wo_scr.shape[0] // rows):
            cp = pltpu.make_async_copy(wo_hbm.at[layer, pl.ds(part * rows, rows)], stage, sem)
            cp.start()
            cp.wait()
            wo_scr[pl.ds(part * rows, rows), :] = stage[...].astype(BF16)

    is_prompt = i < n_prompt_tiles
    acc = jnp.dot(mr_ref[...], wo_scr[:c, :], preferred_element_type=F32)
    acc = acc + jnp.dot(mm_ref[...], wo_scr[c:, :], preferred_element_type=F32)
    xn3 = _rows3(x_ref[...]) + _pick_mod(is_prompt, gtp_ref, gts_ref) * _rows3(acc)
    rs = lax.rsqrt(jnp.mean(xn3 * xn3, axis=-1, keepdims=True) + EPS)
    h2 = (xn3 * rs * (1.0 + _pick_mod(is_prompt, scp_ref, scs_ref))
          + _pick_mod(is_prompt, shp_ref, shs_ref)).reshape(tm, d)
    xn_ref[...] = xn3.reshape(tm, d)
    _store_slabs(h2_ref, h2)

    logits = jnp.dot(h2.astype(BF16), wr_ref[0], preferred_element_type=F32) + br_ref[0]
    lane = lax.broadcasted_iota(jnp.int32, logits.shape, 1)
    lane_f = lane.astype(F32)
    big = np.float32(LANES)

    def first_max(vals):
        vmax = jnp.max(vals, axis=-1, keepdims=True)
        idx = jnp.min(jnp.where(vals == vmax, lane_f, big), axis=-1, keepdims=True)
        return vmax, idx.astype(jnp.int32)

    gl = jnp.where(lane < n_groups, logits, -jnp.inf)
    gmax, g_sel = first_max(gl)
    p_g = 1.0 / jnp.sum(jnp.exp(gl - gmax), axis=-1, keepdims=True)
    lo = n_groups + g_sel * per_group
    el = jnp.where((lane >= lo) & (lane < lo + per_group), logits, -jnp.inf)
    v1, i1 = first_max(el)
    v2, i2 = first_max(jnp.where(lane == i1, -jnp.inf, el))
    e21 = jnp.exp(v2 - v1)
    w1 = 1.0 / (1.0 + e21)
    w2 = e21 * w1
    eid_ref[...] = jnp.where(lane == 0, i1 - n_groups, jnp.where(lane == 1, i2 - n_groups, 0))
    wt_ref[...] = jnp.where(lane == 0, w1 * p_g, jnp.where(lane == 1, w2 * p_g, 0.0))


def _out_call(x, mix_rg, mix_ml, mod_p, mod_s, w_out, w_router, b_router, layer, n_groups, per_group, tok):
    t, d = x.shape
    c = mix_rg.shape[1]
    cm = mix_ml.shape[1]
    tm = tok.tile
    stage_rows = min(256, c + cm)
    return pl.pallas_call(
        functools.partial(_out_kernel, layer=layer, n_groups=n_groups, per_group=per_group,
                          n_prompt_tiles=tok.n_prompt_tiles),
        grid=(t // tm,),
        in_specs=[
            pl.BlockSpec((tm, d), lambda i: (i, 0)),
            pl.BlockSpec((tm, c), lambda i: (i, 0)),
            pl.BlockSpec((tm, cm), lambda i: (i, 0)),
            *tok.mod_specs(layer, GT_A, d),
            *tok.mod_specs(layer, SH_F, d),
            *tok.mod_specs(layer, SC_F, d),
            pl.BlockSpec(memory_space=pl.ANY),
            pl.BlockSpec((1, d, LANES), lambda i: (layer, 0, 0)),
            pl.BlockSpec((1, 1, LANES), lambda i: (layer, 0, 0)),
        ],
        out_specs=[
            pl.BlockSpec((tm, d), lambda i: (i, 0)),
            pl.BlockSpec((tm * _slab_rows(d), LANES), lambda i: (i, 0)),
            pl.BlockSpec((tm, LANES), lambda i: (i, 0)),
            pl.BlockSpec((tm, LANES), lambda i: (i, 0)),
        ],
        out_shape=[
            jax.ShapeDtypeStruct((t, d), F32),
            jax.ShapeDtypeStruct((t * _slab_rows(d), LANES), F32),
            jax.ShapeDtypeStruct((t, LANES), jnp.int32),
            jax.ShapeDtypeStruct((t, LANES), F32),
        ],
        scratch_shapes=[
            pltpu.VMEM((c + cm, d), BF16),
            pltpu.VMEM((stage_rows, d), F32),
            pltpu.SemaphoreType.DMA(()),
        ],
        compiler_params=_cparams("arbitrary"),
        name="out_proj_router",
    )(x, mix_rg, mix_ml, mod_p, mod_s, mod_p, mod_s, mod_p, mod_s, w_out, w_router, b_router)


def _slab_rows(d):
    return d // LANES


def _store_slabs(ref, val):
    rows, d = val.shape
    slab = _slab_rows(d)
    for s in range(slab):
        ref[pl.ds(s, rows, stride=slab), :] = val[:, s * LANES:(s + 1) * LANES]


def _load_slabs(ref, rows, d):
    slab = _slab_rows(d)
    return jnp.concatenate([ref[pl.ds(s, rows, stride=slab), :] for s in range(slab)], axis=-1)


def _gather_slabs(idx_ref, base, n_rows, src_hbm, dst, sem, slab, stride=1, offset=0):
    def body(g, carry):
        for s in range(SUBLANES):
            r = g * SUBLANES + s
            tok = idx_ref[base + r * stride + offset]
            pltpu.make_async_copy(src_hbm.at[pl.ds(pl.multiple_of(tok * slab, slab), slab)],
                                  dst.at[pl.ds(pl.multiple_of(r * slab, slab), slab)], sem).start()
        return carry

    lax.fori_loop(0, n_rows // SUBLANES, body, 0)


def _wait_rows(buf, sem):
    pltpu.make_async_copy(buf, buf, sem).wait()


def _moe_kernel(te_ref, nx_ref, nu_ref, src_ref, h2_hbm, w1_hbm, w3_hbm, w2_hbm, y_ref,
                xbuf, st1, st3, st2, w1_scr, w3_scr, w2_scr, slot_ref, gsem, wsem,
                *, layer):
    j = pl.program_id(0)
    nbuf = xbuf.shape[0]
    d = w1_scr.shape[0]
    slab = _slab_rows(d)
    tm = xbuf.shape[1] // slab
    n_used = nu_ref[0]

    def weight_copies(expert, slot):
        return [pltpu.make_async_copy(w_hbm.at[layer, expert], stage.at[slot], wsem.at[slot])
                for w_hbm, stage in ((w1_hbm, st1), (w3_hbm, st3), (w2_hbm, st2))]

    def issue(tile):
        _gather_slabs(src_ref, tile * tm, tm, h2_hbm, xbuf.at[tile % nbuf], gsem.at[tile % nbuf], slab)

    @pl.when(j == 0)
    def _():
        slot_ref[0] = 1
        for cp in weight_copies(te_ref[0], 0):
            cp.start()
        for ahead in range(nbuf - 1):
            @pl.when(ahead < n_used)
            def _():
                issue(ahead)

    @pl.when(j + (nbuf - 1) < n_used)
    def _():
        issue(j + (nbuf - 1))

    @pl.when((j < n_used) & ((j == 0) | (te_ref[j] != te_ref[jnp.maximum(j - 1, 0)])))
    def _():
        slot = 1 - slot_ref[0]
        slot_ref[0] = slot
        for cp in weight_copies(te_ref[j], slot):
            cp.wait()
        w1_scr[...] = st1[slot].astype(BF16)
        w3_scr[...] = st3[slot].astype(BF16)
        w2_scr[...] = st2[slot].astype(BF16)

        @pl.when(nx_ref[j] >= 0)
        def _():
            for cp in weight_copies(nx_ref[j], 1 - slot):
                cp.start()

    @pl.when(j < n_used)
    def _():
        slot = j % nbuf
        _wait_rows(xbuf.at[slot], gsem.at[slot])
        xb = _load_slabs(xbuf.at[slot], tm, d).astype(BF16)
        a = jnp.dot(xb, w1_scr[...], preferred_element_type=F32)
        b = jnp.dot(xb, w3_scr[...], preferred_element_type=F32)
        hid = (a * _sigmoid(a)) * b
        _store_slabs(y_ref, jnp.dot(hid.astype(BF16), w2_scr[...], preferred_element_type=F32))

    @pl.when(j >= n_used)
    def _():
        y_ref[...] = jnp.zeros_like(y_ref)


def _moe_call(tile_exp, next_exp, n_used, src_tok, h2_slabs, w1, w3, w2, layer, tm):
    _, ne, d, f = w1.shape
    slab = _slab_rows(d)
    n_tiles = tile_exp.shape[0]
    gather_bufs = 3
    grid_spec = pltpu.PrefetchScalarGridSpec(
        num_scalar_prefetch=4,
        grid=(n_tiles,),
        in_specs=[pl.BlockSpec(memory_space=pl.ANY)] * 4,
        out_specs=pl.BlockSpec((tm * slab, LANES), lambda j, *_: (j, 0)),
        scratch_shapes=[
            pltpu.VMEM((gather_bufs, tm * slab, LANES), F32),
            pltpu.VMEM((2, d, f), F32),
            pltpu.VMEM((2, d, f), F32),
            pltpu.VMEM((2, f, d), F32),
            pltpu.VMEM((d, f), BF16),
            pltpu.VMEM((d, f), BF16),
            pltpu.VMEM((f, d), BF16),
            pltpu.SMEM((1,), jnp.int32),
            pltpu.SemaphoreType.DMA((gather_bufs,)),
            pltpu.SemaphoreType.DMA((2,)),
        ],
    )
    return pl.pallas_call(
        functools.partial(_moe_kernel, layer=layer),
        grid_spec=grid_spec,
        out_shape=jax.ShapeDtypeStruct((n_tiles * tm * slab, LANES), F32),
        compiler_params=_cparams("arbitrary"),
        name="experts",
    )(tile_exp, next_exp, n_used, src_tok, h2_slabs, w1, w3, w2)


def _comb_kernel(dest_ref, xn_ref, wt_ref, gtp_ref, gts_ref, *rest, final, heads, n_prompt_tiles):
    if final:
        gf_ref, y_hbm, yp_ref, ys_ref, ybuf, sem = rest
    else:
        shp_ref, shs_ref, scp_ref, scs_ref, wg_ref, bg_ref, y_hbm, xo_ref, h_ref, g_ref, ybuf, sem = rest
    i = pl.program_id(0)
    n = pl.num_programs(0)
    tm, d = xn_ref.shape
    is_prompt = i < n_prompt_tiles

    def issue(tile, slot):
        for k in range(2):
            _gather_slabs(dest_ref, tile * tm * 2, tm, y_hbm, ybuf.at[slot, k], sem.at[slot], _slab_rows(d),
                          stride=2, offset=k)

    @pl.when(i == 0)
    def _():
        issue(0, 0)

    @pl.when(i + 1 < n)
    def _():
        issue(i + 1, (i + 1) % 2)

    slot = i % 2
    for k in range(2):
        _wait_rows(ybuf.at[slot, k], sem.at[slot])
    wt = wt_ref[...]
    y = wt[:, 0:1] * _load_slabs(ybuf.at[slot, 0], tm, d) + wt[:, 1:2] * _load_slabs(ybuf.at[slot, 1], tm, d)
    xo3 = _rows3(xn_ref[...]) + _pick_mod(is_prompt, gtp_ref, gts_ref) * _rows3(y)
    if final:
        xo = xo3.reshape(tm, d)
        yo = xo * lax.rsqrt(jnp.mean(xo * xo, axis=-1, keepdims=True) + EPS) * gf_ref[...]

        @pl.when(is_prompt)
        def _():
            yp_ref[...] = yo

        @pl.when(jnp.logical_not(is_prompt))
        def _():
            ys_ref[...] = yo
    else:
        xo_ref[...] = xo3.reshape(tm, d)
        _norm_gates(xo3, _pick_mod(is_prompt, scp_ref, scs_ref), _pick_mod(is_prompt, shp_ref, shs_ref),
                    wg_ref, bg_ref, h_ref, g_ref, heads)


def _comb_call(dest, xn, wts, mod_p, mod_s, w_gate, b_gate, g_final, y, layer, heads, tok, final):
    t, d = xn.shape
    tm = tok.tile
    full = lambda i, dst: (0, 0)
    in_specs = [
        pl.BlockSpec((tm, d), lambda i, dst: (i, 0)),
        pl.BlockSpec((tm, LANES), lambda i, dst: (i, 0)),
        *tok.mod_specs(layer, GT_F, d),
    ]
    args = [xn, wts, mod_p, mod_s]
    if final:
        n_p = tok.n_prompt_tiles
        in_specs.append(pl.BlockSpec((1, d), full))
        args.append(g_final)
        out_specs = [pl.BlockSpec((tm, d), lambda i, dst: (jnp.minimum(i, n_p - 1), 0)),
                     pl.BlockSpec((tm, d), lambda i, dst: (tok.sample_tile(i), 0))]
        out_shape = [jax.ShapeDtypeStruct((n_p * tm, d), F32), jax.ShapeDtypeStruct((t - n_p * tm, d), F32)]
    else:
        out_specs = [pl.BlockSpec((tm, d), lambda i, dst: (i, 0))]
        out_shape = [jax.ShapeDtypeStruct((t, d), F32)]
        nxt = layer + 1
        in_specs += [
            *tok.mod_specs(nxt, SH_A, d),
            *tok.mod_specs(nxt, SC_A, d),
            pl.BlockSpec((1, d, LANES), lambda i, dst: (nxt, 0, 0)),
            pl.BlockSpec((1, 1, LANES), lambda i, dst: (nxt, 0, 0)),
        ]
        args += [mod_p, mod_s, mod_p, mod_s, w_gate, b_gate]
        out_specs += [pl.BlockSpec((tm, d), lambda i, dst: (i, 0)), pl.BlockSpec((tm, LANES), lambda i, dst: (i, 0))]
        out_shape += [jax.ShapeDtypeStruct((t, d), BF16), jax.ShapeDtypeStruct((t, LANES), F32)]
    in_specs.append(pl.BlockSpec(memory_space=pl.ANY))
    args.append(y)
    grid_spec = pltpu.PrefetchScalarGridSpec(
        num_scalar_prefetch=1,
        grid=(t // tm,),
        in_specs=in_specs,
        out_specs=out_specs,
        scratch_shapes=[pltpu.VMEM((2, 2, tm * _slab_rows(d), LANES), F32), pltpu.SemaphoreType.DMA((2,))],
    )
    return pl.pallas_call(
        functools.partial(_comb_kernel, final=final, heads=heads, n_prompt_tiles=tok.n_prompt_tiles),
        grid_spec=grid_spec,
        out_shape=out_shape,
        compiler_params=_cparams("arbitrary"),
        name="combine",
    )(dest, *args)


def _invert_kernel(dest_ref, src_ref):
    n_pairs = dest_ref.shape[0]
    n_rows = src_ref.shape[0]

    def clear(i, carry):
        for u in range(INVERT_CLEAR_UNROLL):
            src_ref[i * INVERT_CLEAR_UNROLL + u] = 0
        return carry

    lax.fori_loop(0, n_rows // INVERT_CLEAR_UNROLL, clear, 0)

    def place(i, carry):
        for u in range(SUBLANES):
            pair = i * SUBLANES + u
            src_ref[dest_ref[pair]] = lax.shift_right_logical(pair, 1)
        return carry

    lax.fori_loop(0, n_pairs // SUBLANES, place, 0)


def _invert_call(dest, n_rows):
    assert dest.shape[0] % SUBLANES == 0 and n_rows % INVERT_CLEAR_UNROLL == 0
    return pl.pallas_call(
        _invert_kernel,
        in_specs=[pl.BlockSpec(memory_space=pltpu.SMEM)],
        out_specs=pl.BlockSpec(memory_space=pltpu.SMEM),
        out_shape=jax.ShapeDtypeStruct((n_rows,), jnp.int32),
        name="route_invert",
    )(dest)


def _route_plan(eid, n_experts, tm):
    t = eid.shape[0]
    e_flat = eid[:, :2].reshape(-1)
    n_pairs = 2 * t
    n_tiles = -(-n_pairs // tm) + n_experts
    onehot = (e_flat[:, None] == jnp.arange(n_experts, dtype=jnp.int32)[None, :]).astype(jnp.int32)
    csum = jnp.cumsum(onehot, axis=0)
    rank = jnp.sum(csum * onehot, axis=1) - 1
    counts = csum[-1]
    padded = ((counts + tm - 1) // tm) * tm
    ends = jnp.cumsum(padded)
    starts = ends - padded
    dest = (jnp.sum(starts[None, :] * onehot, axis=1) + rank).astype(jnp.int32)
    src_tok = _invert_call(dest, n_tiles * tm)
    n_used = (ends[-1] // tm).astype(jnp.int32)
    tile_id = jnp.minimum(jnp.arange(n_tiles, dtype=jnp.int32), n_used - 1)
    tile_exp = jnp.sum((ends[None, :] <= (tile_id * tm)[:, None]).astype(jnp.int32), axis=1)
    tile_exp = jnp.minimum(tile_exp, n_experts - 1).astype(jnp.int32)
    ids = jnp.arange(n_experts, dtype=jnp.int32)
    later = (ids[None, :] > ids[:, None]) & (padded[None, :] > 0)
    following = jnp.min(jnp.where(later, ids[None, :], n_experts), axis=1)
    following = jnp.where(following == n_experts, -1, following).astype(jnp.int32)
    return tile_exp, following[tile_exp], n_used.reshape(1), src_tok, dest


def _pick_tile(total, pref):
    tile = min(pref, total)
    while total % tile:
        tile //= 2
    return tile


def kernel(x_prompt, x_sample, c_prompt, c_sample, state_rg_conv, state_rg_h, state_mlstm_C, state_mlstm_n,
           state_mlstm_m, w_mod, b_mod, w_in, conv_w, conv_b, w_ra, b_ra, w_ri, b_ri, lam, g_rg, b_ig, b_fg,
           g_ml, w_out, w_grp, b_grp, w_er, b_er, w1, w3, w2, g_final):
    bp, seq, d = x_prompt.shape
    bs, dec_seq, _ = x_sample.shape
    depth = w_mod.shape[0]
    d_rg = conv_w.shape[2]
    conv_width = conv_w.shape[1]
    _, _, heads, dk, dv = state_mlstm_C.shape
    d_ml = heads * dv
    n_groups, per_group = w_er.shape[1], w_er.shape[3]
    n_experts = w1.shape[1]
    t_p, t_s = bp * seq, bs * dec_seq
    t = t_p + t_s
    n_main = 2 * d_rg + 2 * heads * dk + 2 * d_ml
    q_off, k_off, v_off, o_off = 2 * d_rg, 2 * d_rg + heads * dk, 2 * d_rg + 2 * heads * dk, n_main - d_ml
    assert dec_seq == SUBLANES and seq % LANES == 0 and conv_width - 1 <= SUBLANES
    assert 2 * heads <= LANES and n_groups * (per_group + 1) <= LANES
    assert q_off % (heads * dk) == 0 and k_off % (heads * dk) == 0 and v_off % d_ml == 0 and o_off % d_ml == 0

    tok = _Tokens(t_p, t_s, seq, _pick_tile(np.gcd(seq, t_s), 256))
    tok_out = _Tokens(t_p, t_s, seq, _pick_tile(np.gcd(seq, t_s), 512))
    tm_in = _pick_tile(np.gcd(t_p, t_s), 1024)
    tc = LANES
    assert t_s % tc == 0 and seq % tc == 0

    mc = -(-(bp + bs) // SUBLANES) * SUBLANES
    c_all = jnp.zeros((mc, d), F32).at[:bp].set(c_prompt).at[bp:bp + bs].set(c_sample)
    mod = _mod_call(c_all, w_mod, b_mod, _pick_tile(6 * d, 1024))
    mod_p = mod[:, :bp].reshape(depth, bp, 1, 6 * d)
    mod_s = mod[:, bp:bp + bs].reshape(depth, bs, 1, 6 * d)

    w_in_t = jnp.swapaxes(w_in, 1, 2)
    w_gate = jnp.zeros((depth, d, LANES), F32).at[:, :, :2 * heads].set(w_in[:, :, n_main:]).astype(BF16)
    b_gate = jnp.zeros((depth, 1, LANES), F32).at[:, 0, :heads].set(b_ig).at[:, 0, heads:2 * heads].set(b_fg)
    w_ra_b, w_ri_b = w_ra.astype(BF16), w_ri.astype(BF16)
    n_rt = n_groups * (per_group + 1)
    w_router = jnp.concatenate([w_grp, jnp.moveaxis(w_er, 1, 2).reshape(depth, d, n_groups * per_group)], axis=-1)
    w_router = jnp.zeros((depth, d, LANES), F32).at[:, :, :n_rt].set(w_router).astype(BF16)
    b_router = jnp.concatenate([b_grp, b_er.reshape(depth, -1)], axis=-1)
    b_router = jnp.zeros((depth, 1, LANES), F32).at[:, 0, :n_rt].set(b_router)
    vec = lambda p: p.reshape(depth, 1, -1)

    pad_rows = SUBLANES - (conv_width - 1)
    prev_s = jnp.pad(state_rg_conv, ((0, 0), (0, 0), (pad_rows, 0), (0, 0))).reshape(depth, bs * SUBLANES, d_rg)
    h0_s = state_rg_h.reshape(depth, bs, 1, d_rg)
    n_s = state_mlstm_n.reshape(depth, bs, heads, 1, dk)
    m_s_col = jnp.pad(jnp.repeat(state_mlstm_m, dec_seq, axis=1), ((0, 0), (0, 0), (0, LANES - heads)))

    x, h, gcol = _prenorm_call(x_prompt.reshape(t_p, d), x_sample.reshape(t_s, d), mod_p, mod_s, w_gate, b_gate,
                               heads, tok)
    outs_p = [[] for _ in range(5)]
    outs_s = [[] for _ in range(5)]
    cs_stack = jnp.zeros_like(state_mlstm_C)
    for l in range(depth):
        u = _in_call(h, w_in_t, l, n_main, tm_in, _pick_tile(n_main, 1024))
        hr, mix_rg = _rg_call(u, prev_s, h0_s, conv_w, vec(conv_b), w_ra_b, vec(b_ra), w_ri_b, vec(b_ri),
                              vec(lam), vec(g_rg), l, tok)
        grow = gcol[:, :2 * heads].T
        mix_ml, c_p, n_p, m_p, cs_stack, n_so, m_so = _ml_call(
            u, gcol, grow, state_mlstm_C, n_s, m_s_col, vec(g_ml), cs_stack, l, t_p, seq, dec_seq,
            q_off, k_off, v_off, o_off, tc)
        xn, h2, eid, wts = _out_call(x, mix_rg, mix_ml, mod_p, mod_s, w_out, w_router, b_router, l,
                                     n_groups, per_group, tok_out)
        tm_e = EXPERT_TILE_ROWS[l % len(EXPERT_TILE_ROWS)]
        tile_exp, next_exp, n_used, src_tok, dest = _route_plan(eid, n_experts, tm_e)
        y = _moe_call(tile_exp, next_exp, n_used, src_tok, h2, w1, w3, w2, l, tm_e)
        final = l == depth - 1
        res = _comb_call(dest, xn, wts, mod_p, mod_s, w_gate, b_gate, g_final[None], y, l, heads, tok, final)
        if final:
            y_p, y_s = res
        else:
            x, h, gcol = res

        tail = conv_width - 1
        outs_p[0].append(jnp.stack([u[(b + 1) * seq - tail:(b + 1) * seq, :d_rg] for b in range(bp)]))
        outs_p[1].append(jnp.stack([hr[(b + 1) * seq - 1] for b in range(bp)]))
        outs_p[2].append(c_p)
        outs_p[3].append(n_p.reshape(bp, heads, dk))
        outs_p[4].append(m_p[:, 0, :heads])
        outs_s[0].append(u[t_p:, :d_rg].reshape(bs, dec_seq, d_rg)[:, dec_seq - tail:])
        outs_s[1].append(hr[t_p:].reshape(bs, dec_seq, d_rg)[:, -1])
        outs_s[3].append(n_so.reshape(bs, heads, dk))
        outs_s[4].append(m_so.reshape(bs, dec_seq, LANES)[:, -1, :heads])

    stack = lambda parts: jnp.stack(parts)
    return (y_p.reshape(bp, seq, d), y_s.reshape(bs, dec_seq, d), *[stack(o) for o in outs_p],
            stack(outs_s[0]), stack(outs_s[1]), cs_stack, stack(outs_s[3]), stack(outs_s[4]))
```

```python
import functools

import numpy as np
import jax
import jax.numpy as jnp
from jax import lax
from jax.experimental import pallas as pl
from jax.experimental.pallas import tpu as pltpu

F32 = jnp.float32
BF16 = jnp.bfloat16
EPS = 1e-6
RG_C = 8.0
SUBLANES = 8
LANES = 128
VMEM_LIMIT = 56 * 1024 * 1024
HI = lax.Precision.HIGHEST
INVERT_CLEAR_UNROLL = 32
EXPERT_TILE_ROWS = 128
IN_PROJ_TILE_COLS = 1280
SH_A, SC_A, GT_A, SH_F, SC_F, GT_F = range(6)


def _cparams(*sem):
    return pltpu.CompilerParams(dimension_semantics=sem, vmem_limit_bytes=VMEM_LIMIT)


def _sigmoid(z):
    return 0.5 * jnp.tanh(0.5 * z) + 0.5


def _softplus(z):
    return jnp.maximum(z, 0.0) + jnp.log1p(jnp.exp(-jnp.abs(z)))


def _rows3(x):
    r, c = x.shape
    return x.reshape(r // SUBLANES, SUBLANES, c)


class _Tokens:
    def __init__(self, t_prompt, t_sample, seq, tile):
        self.tile = tile
        self.groups = tile // SUBLANES
        self.n_prompt_tiles = t_prompt // tile
        self.tiles_per_seq = seq // tile
        self.n_prompt_seqs = t_prompt // seq

    def prompt_seq(self, i):
        return jnp.minimum(i // self.tiles_per_seq, self.n_prompt_seqs - 1)

    def sample_tile(self, i):
        return jnp.maximum(i - self.n_prompt_tiles, 0)

    def mod_specs(self, layer, chunk, d, tile_axis=0):
        def prompt_map(*idx):
            return (layer, self.prompt_seq(idx[tile_axis]), 0, chunk)

        def sample_map(*idx):
            return (layer, self.sample_tile(idx[tile_axis]), 0, chunk)

        return [pl.BlockSpec((1, 1, 1, d), prompt_map), pl.BlockSpec((1, self.groups, 1, d), sample_map)]


def _pick_mod(is_prompt, mp_ref, ms_ref):
    return jnp.where(is_prompt, mp_ref[0], ms_ref[0])


def _mod_kernel(c_ref, w_ref, b_ref, o_ref):
    c = c_ref[...]
    s = (c * _sigmoid(c)).astype(BF16)
    o_ref[0] = jnp.dot(s, w_ref[0].astype(BF16), preferred_element_type=F32) + b_ref[0]


def _mod_call(c_all, w_mod, b_mod, tn):
    depth, d, n = w_mod.shape
    mc = c_all.shape[0]
    return pl.pallas_call(
        _mod_kernel,
        grid=(depth, n // tn),
        in_specs=[
            pl.BlockSpec((mc, d), lambda l, j: (0, 0)),
            pl.BlockSpec((1, d, tn), lambda l, j: (l, 0, j)),
            pl.BlockSpec((1, 1, tn), lambda l, j: (l, 0, j)),
        ],
        out_specs=pl.BlockSpec((1, mc, tn), lambda l, j: (l, 0, j)),
        out_shape=jax.ShapeDtypeStruct((depth, mc, n), F32),
        compiler_params=_cparams("arbitrary", "arbitrary"),
        name="mod",
    )(c_all, w_mod, b_mod.reshape(depth, 1, n))


def _norm_gates(x3, sc, sh, wg_ref, bg_ref, h_ref, g_ref, heads):
    groups, _, d = x3.shape
    xn = x3 * lax.rsqrt(jnp.mean(x3 * x3, axis=-1, keepdims=True) + EPS)
    hb = (xn * (1.0 + sc) + sh).reshape(groups * SUBLANES, d).astype(BF16)
    h_ref[...] = hb
    g = jnp.dot(hb, wg_ref[0], preferred_element_type=F32) + bg_ref[0]
    lane = lax.broadcasted_iota(jnp.int32, g.shape, 1)
    is_forget = (lane >= heads) & (lane < 2 * heads)
    g_ref[...] = jnp.where(is_forget, -_softplus(-g), g)


def _prenorm_kernel(xp_ref, xs_ref, shp_ref, shs_ref, scp_ref, scs_ref, wg_ref, bg_ref, x_ref, h_ref, g_ref,
                    *, heads, n_prompt_tiles):
    is_prompt = pl.program_id(0) < n_prompt_tiles
    x = jnp.where(is_prompt, xp_ref[...], xs_ref[...])
    x_ref[...] = x
    _norm_gates(_rows3(x), _pick_mod(is_prompt, scp_ref, scs_ref), _pick_mod(is_prompt, shp_ref, shs_ref),
                wg_ref, bg_ref, h_ref, g_ref, heads)


def _prenorm_call(x_p, x_s, mod_p, mod_s, w_gate, b_gate, heads, tok):
    d = x_p.shape[1]
    t = x_p.shape[0] + x_s.shape[0]
    tm = tok.tile
    n_p = tok.n_prompt_tiles
    return pl.pallas_call(
        functools.partial(_prenorm_kernel, heads=heads, n_prompt_tiles=n_p),
        grid=(t // tm,),
        in_specs=[
            pl.BlockSpec((tm, d), lambda i: (jnp.minimum(i, n_p - 1), 0)),
            pl.BlockSpec((tm, d), lambda i: (tok.sample_tile(i), 0)),
            *tok.mod_specs(0, SH_A, d),
            *tok.mod_specs(0, SC_A, d),
            pl.BlockSpec((1, d, LANES), lambda i: (0, 0, 0)),
            pl.BlockSpec((1, 1, LANES), lambda i: (0, 0, 0)),
        ],
        out_specs=[
            pl.BlockSpec((tm, d), lambda i: (i, 0)),
            pl.BlockSpec((tm, d), lambda i: (i, 0)),
            pl.BlockSpec((tm, LANES), lambda i: (i, 0)),
        ],
        out_shape=[
            jax.ShapeDtypeStruct((t, d), F32),
            jax.ShapeDtypeStruct((t, d), BF16),
            jax.ShapeDtypeStruct((t, LANES), F32),
        ],
        compiler_params=_cparams("arbitrary"),
        name="prenorm",
    )(x_p, x_s, mod_p, mod_s, mod_p, mod_s, w_gate, b_gate)


def _in_kernel(h_ref, w_ref, u_ref, wb_scr):
    @pl.when(pl.program_id(1) == 0)
    def _():
        wb_scr[...] = w_ref[0].astype(BF16)

    u_ref[...] = lax.dot_general(h_ref[...], wb_scr[...], (((1,), (1,)), ((), ())), preferred_element_type=F32)


def _in_call(h, w_in_t, layer, n_main, tm, tn):
    t, d = h.shape
    return pl.pallas_call(
        _in_kernel,
        grid=(n_main // tn, t // tm),
        in_specs=[
            pl.BlockSpec((tm, d), lambda j, i: (i, 0)),
            pl.BlockSpec((1, tn, d), lambda j, i: (layer, j, 0)),
        ],
        out_specs=pl.BlockSpec((tm, tn), lambda j, i: (i, j)),
        out_shape=jax.ShapeDtypeStruct((t, n_main), F32),
        scratch_shapes=[pltpu.VMEM((tn, d), BF16)],
        compiler_params=_cparams("arbitrary", "arbitrary"),
        name="in_proj",
    )(h, w_in_t)


def _rg_kernel(xr_ref, yr_ref, prev_ref, h0_ref, cw_ref, cb_ref, wra_ref, bra_ref, wri_ref, bri_ref,
               lam_ref, gain_ref, hr_ref, mix_ref, xprev_scr, hprev_scr, hstart_scr, a_scr, b_scr,
               *, n_prompt_tiles, tiles_per_seq):
    i = pl.program_id(0)
    tl, c = xr_ref.shape
    groups = tl // SUBLANES
    _, nblk, bw, _ = wra_ref.shape
    is_prompt = i < n_prompt_tiles

    @pl.when(is_prompt & (i % tiles_per_seq == 0))
    def _():
        xprev_scr[...] = jnp.zeros_like(xprev_scr)
        hprev_scr[...] = jnp.zeros_like(hprev_scr)

    x = xr_ref[...]
    prev_prompt = jnp.concatenate([xprev_scr[...], x[:tl - SUBLANES]], axis=0)
    prev = jnp.where(is_prompt, prev_prompt, prev_ref[0])
    xprev_scr[...] = x[tl - SUBLANES:]

    x3 = _rows3(x)
    p3 = _rows3(prev)
    row = lax.broadcasted_iota(jnp.int32, x3.shape, 1)
    cw = cw_ref[0]
    conv_w = cw.shape[0]
    xc = cb_ref[0] + x3 * cw[conv_w - 1:conv_w]
    for dlt in range(1, conv_w):
        shifted = jnp.where(row >= dlt, pltpu.roll(x3, dlt, 1), pltpu.roll(p3, dlt, 1))
        xc = xc + shifted * cw[conv_w - 1 - dlt:conv_w - dlt]

    xc2 = xc.reshape(tl, c)
    xcb = xc2.astype(BF16)
    r_parts, i_parts = [], []
    for nb in range(nblk):
        blk = xcb[:, nb * bw:(nb + 1) * bw]
        r_parts.append(jnp.dot(blk, wra_ref[0, nb], preferred_element_type=F32))
        i_parts.append(jnp.dot(blk, wri_ref[0, nb], preferred_element_type=F32))
    r = _sigmoid(jnp.concatenate(r_parts, axis=-1) + bra_ref[0])
    ig = _sigmoid(jnp.concatenate(i_parts, axis=-1) + bri_ref[0])
    log_a = (-RG_C) * r * _softplus(-lam_ref[0])
    a = jnp.exp(log_a)
    th = jnp.tanh(log_a)
    gap = -2.0 * th / (1.0 - th)
    mult = jnp.where(gap > 0.0, gap * lax.rsqrt(gap), 0.0)
    bx = mult * (ig * xc2)

    av = _rows3(a)
    bv = _rows3(bx)
    for s in (1, 2, 4):
        a_sh = jnp.where(row >= s, pltpu.roll(av, s, 1), 1.0)
        b_sh = jnp.where(row >= s, pltpu.roll(bv, s, 1), 0.0)
        bv = av * b_sh + bv
        av = av * a_sh

    @pl.when(is_prompt)
    def _():
        a_scr[...] = av.reshape(tl, c)
        b_scr[...] = bv.reshape(tl, c)

        def body(g, h):
            hstart_scr[g] = h
            last = g * SUBLANES + (SUBLANES - 1)
            return a_scr[pl.ds(last, 1), :] * h + b_scr[pl.ds(last, 1), :]

        hprev_scr[...] = lax.fori_loop(0, groups, body, hprev_scr[...])

    @pl.when(jnp.logical_not(is_prompt))
    def _():
        hstart_scr[...] = h0_ref[0]

    h3 = av * hstart_scr[...] + bv
    hr = h3.reshape(tl, c)
    hr_ref[...] = hr
    y = yr_ref[...]
    gelu = 0.5 * y * (1.0 + jnp.tanh(np.sqrt(2.0 / np.pi).astype(np.float32) * (y + 0.044715 * (y * y * y))))
    hn = hr * lax.rsqrt(jnp.mean(hr * hr, axis=-1, keepdims=True) + EPS)
    mix_ref[...] = (hn * gain_ref[0] * gelu).astype(BF16)


def _rg_call(u, prev_s, h0_s, conv_w, conv_b, w_ra, b_ra, w_ri, b_ri, lam, g_rg, layer, tok):
    t = u.shape[0]
    _, conv_width, c = conv_w.shape
    _, nblk, bw, _ = w_ra.shape
    tl, groups = tok.tile, tok.groups
    vec = pl.BlockSpec((1, 1, c), lambda i: (layer, 0, 0))
    return pl.pallas_call(
        functools.partial(_rg_kernel, n_prompt_tiles=tok.n_prompt_tiles, tiles_per_seq=tok.tiles_per_seq),
        grid=(t // tl,),
        in_specs=[
            pl.BlockSpec((tl, c), lambda i: (i, 0)),
            pl.BlockSpec((tl, c), lambda i: (i, 1)),
            pl.BlockSpec((1, tl, c), lambda i: (layer, tok.sample_tile(i), 0)),
            pl.BlockSpec((1, groups, 1, c), lambda i: (layer, tok.sample_tile(i), 0, 0)),
            pl.BlockSpec((1, conv_width, c), lambda i: (layer, 0, 0)),
            vec,
            pl.BlockSpec((1, nblk, bw, bw), lambda i: (layer, 0, 0, 0)),
            vec,
            pl.BlockSpec((1, nblk, bw, bw), lambda i: (layer, 0, 0, 0)),
            vec,
            vec,
            vec,
        ],
        out_specs=[
            pl.BlockSpec((tl, c), lambda i: (i, 0)),
            pl.BlockSpec((tl, c), lambda i: (i, 0)),
        ],
        out_shape=[
            jax.ShapeDtypeStruct((t, c), F32),
            jax.ShapeDtypeStruct((t, c), BF16),
        ],
        scratch_shapes=[
            pltpu.VMEM((SUBLANES, c), F32),
            pltpu.VMEM((1, c), F32),
            pltpu.VMEM((groups, 1, c), F32),
            pltpu.VMEM((tl, c), F32),
            pltpu.VMEM((tl, c), F32),
        ],
        compiler_params=_cparams("arbitrary"),
        name="rg_lru",
    )(u, u, prev_s, h0_s, conv_w, conv_b, w_ra, b_ra, w_ri, b_ri, lam, g_rg)


def _ml_chunk(nseq, heads, dk, dv, q_ref, k_ref, v_ref, o_ref, gc, gr, mprev, gain_ref, mix_ref,
              get_c, get_n, put_state):
    tc = q_ref.shape[0]
    ls = tc // nseq
    shift = int(np.log2(ls))
    t_idx = lax.broadcasted_iota(jnp.int32, (tc, tc), 0)
    s_idx = lax.broadcasted_iota(jnp.int32, (tc, tc), 1)
    mask = (lax.shift_right_logical(t_idx, shift) == lax.shift_right_logical(s_idx, shift)) & (s_idx <= t_idx)
    m_cum = mask.astype(F32)
    bcol = jnp.dot(m_cum, gc, precision=HI, preferred_element_type=F32)
    brow = lax.dot_general(gr, m_cum, (((1,), (1,)), ((), ())), precision=HI,
                           preferred_element_type=F32)
    lane = lax.broadcasted_iota(jnp.int32, (tc, LANES), 1)
    scale = np.float32(dk ** -0.5)

    def seg_last(col):
        if nseq == 1:
            return col[tc - 1:tc, :].reshape(1, 1, 1)
        return col.reshape(nseq, ls, 1)[:, ls - 1:ls, :]

    def seg_bcast(val):
        return jnp.broadcast_to(val, (nseq, ls, 1)).reshape(tc, 1)

    m_out = jnp.zeros((tc, LANES), F32)
    for h in range(heads):
        bc = bcol[:, heads + h:heads + h + 1]
        br = brow[heads + h:heads + h + 1, :]
        igr = gr[h:h + 1, :]
        igc = gc[:, h:h + 1]
        mp = mprev[:, h:h + 1]
        dmat = jnp.where(mask, bc - br + igr, -jnp.inf)
        m_inter = bc + mp
        m_t = jnp.maximum(m_inter, jnp.max(dmat, axis=-1, keepdims=True))
        p = jnp.exp(dmat - m_t)
        qh = q_ref[:, h * dk:(h + 1) * dk] * scale
        kh = k_ref[:, h * dk:(h + 1) * dk]
        vh = v_ref[:, h * dv:(h + 1) * dv]
        qb = qh.astype(BF16)
        kb = kh.astype(BF16)
        s = lax.dot_general(qb, kb, (((1,), (1,)), ((), ())), preferred_element_type=F32) * p
        w_inter = jnp.exp(m_inter - m_t)
        c_old = get_c(h)
        n_old = get_n(h)
        q3 = qh.reshape(nseq, ls, dk)
        inter = lax.dot_general(q3.astype(BF16), c_old.astype(BF16), (((2,), (1,)), ((0,), (0,))),
                                preferred_element_type=F32).reshape(tc, dv)
        qn = jnp.sum(q3 * n_old, axis=-1, keepdims=True).reshape(tc, 1)
        num = jnp.dot(s.astype(BF16), vh.astype(BF16), preferred_element_type=F32) + w_inter * inter
        den = jnp.sum(s, axis=-1, keepdims=True) + w_inter * qn
        hh = num / jnp.maximum(jnp.abs(den), jnp.exp(-m_t))
        hn = hh * lax.rsqrt(jnp.mean(hh * hh, axis=-1, keepdims=True) + EPS)
        og = _sigmoid(o_ref[:, h * dv:(h + 1) * dv])
        mix_ref[:, h * dv:(h + 1) * dv] = (hn * gain_ref[0, :, h * dv:(h + 1) * dv] * og).astype(BF16)

        m_new = seg_last(m_t)
        b_last = seg_last(bc)
        wl = jnp.exp(seg_bcast(b_last) - bc + igc - seg_bcast(m_new))
        decay = jnp.exp(b_last + seg_last(mp) - m_new)
        wv3 = (wl * vh).reshape(nseq, ls, dv).astype(BF16)
        k3 = kh.reshape(nseq, ls, dk).astype(BF16)
        upd = lax.dot_general(k3, wv3, (((1,), (1,)), ((0,), (0,))), preferred_element_type=F32)
        c_new = decay * c_old + upd
        n_new = decay * n_old + jnp.sum((wl * kh).reshape(nseq, ls, dk), axis=1, keepdims=True)
        put_state(h, c_new, n_new)
        m_out = jnp.where(lane == h, seg_bcast(m_new), m_out)
    return m_out


def _ml_kernel(q_ref, k_ref, v_ref, o_ref, gc_ref, gr_ref, cs_ref, ns_ref, ms_ref, gain_ref, cs_alias,
               mix_ref, cp_out, np_out, mp_out, cs_out, ns_out, ms_out, c_scr, n_scr, m_scr,
               *, n_prompt_chunks, chunks_per_seq, heads, dk, dv, nseq_sample):
    del cs_alias
    i = pl.program_id(0)
    tc = q_ref.shape[0]
    is_prompt = i < n_prompt_chunks

    @pl.when(is_prompt & (i % chunks_per_seq == 0))
    def _():
        c_scr[...] = jnp.zeros_like(c_scr)
        n_scr[...] = jnp.zeros_like(n_scr)
        m_scr[...] = jnp.zeros_like(m_scr)

    @pl.when(is_prompt)
    def _():
        def put(h, c_new, n_new):
            c_scr[h] = c_new[0]
            n_scr[h] = n_new[0]

        mprev = jnp.broadcast_to(m_scr[...], (tc, LANES))
        m_out = _ml_chunk(1, heads, dk, dv, q_ref, k_ref, v_ref, o_ref, gc_ref[...], gr_ref[...], mprev,
                          gain_ref, mix_ref, lambda h: c_scr[h][None], lambda h: n_scr[h][None], put)
        m_scr[...] = m_out[:1, :]
        cp_out[0] = c_scr[...]
        np_out[0] = n_scr[...]
        mp_out[0] = m_out[:1, :]

    @pl.when(jnp.logical_not(is_prompt))
    def _():
        def put(h, c_new, n_new):
            cs_out[0, :, h] = c_new
            ns_out[:, h] = n_new

        m_out = _ml_chunk(nseq_sample, heads, dk, dv, q_ref, k_ref, v_ref, o_ref, gc_ref[...], gr_ref[...],
                          ms_ref[0], gain_ref, mix_ref, lambda h: cs_ref[0, :, h], lambda h: ns_ref[0, :, h], put)
        ms_out[...] = m_out


def _ml_call(u, gcol, grow, c_s, n_s, m_s_col, g_ml, cs_stack, layer, t_prompt, seq, dec_seq,
             q_off, k_off, v_off, o_off, tc):
    t = u.shape[0]
    depth, bs, heads, dk, dv = c_s.shape
    bp = t_prompt // seq
    hdk, hdv = heads * dk, heads * dv
    n_p = t_prompt // tc
    cps = seq // tc
    nseq = tc // dec_seq
    t_s = t - t_prompt
    row = lambda i: i
    samp_in = samp_out = lambda i: jnp.maximum(i - n_p, 0)
    pseq = lambda i: jnp.minimum(i // cps, bp - 1)
    in_specs = [
        pl.BlockSpec((tc, hdk), lambda i: (row(i), q_off // hdk)),
        pl.BlockSpec((tc, hdk), lambda i: (row(i), k_off // hdk)),
        pl.BlockSpec((tc, hdv), lambda i: (row(i), v_off // hdv)),
        pl.BlockSpec((tc, hdv), lambda i: (row(i), o_off // hdv)),
        pl.BlockSpec((tc, LANES), lambda i: (row(i), 0)),
        pl.BlockSpec((2 * heads, tc), lambda i: (0, row(i))),
        pl.BlockSpec((1, nseq, heads, dk, dv), lambda i: (layer, samp_in(i), 0, 0, 0)),
        pl.BlockSpec((1, nseq, heads, 1, dk), lambda i: (layer, samp_in(i), 0, 0, 0)),
        pl.BlockSpec((1, tc, LANES), lambda i: (layer, samp_in(i), 0)),
        pl.BlockSpec((1, 1, hdv), lambda i: (layer, 0, 0)),
        pl.BlockSpec(memory_space=pl.ANY),
    ]
    args = [u, u, u, u, gcol, grow, c_s, n_s, m_s_col, g_ml, cs_stack]
    return pl.pallas_call(
        functools.partial(_ml_kernel, n_prompt_chunks=n_p, chunks_per_seq=cps, heads=heads, dk=dk, dv=dv,
                          nseq_sample=nseq),
        grid=(t // tc,),
        in_specs=in_specs,
        out_specs=[
            pl.BlockSpec((tc, hdv), lambda i: (row(i), 0)),
            pl.BlockSpec((1, heads, dk, dv), lambda i: (pseq(i), 0, 0, 0)),
            pl.BlockSpec((1, heads, 1, dk), lambda i: (pseq(i), 0, 0, 0)),
            pl.BlockSpec((1, 1, LANES), lambda i: (pseq(i), 0, 0)),
            pl.BlockSpec((1, nseq, heads, dk, dv), lambda i: (layer, samp_out(i), 0, 0, 0)),
            pl.BlockSpec((nseq, heads, 1, dk), lambda i: (samp_out(i), 0, 0, 0)),
            pl.BlockSpec((tc, LANES), lambda i: (samp_out(i), 0)),
        ],
        out_shape=[
            jax.ShapeDtypeStruct((t, hdv), BF16),
            jax.ShapeDtypeStruct((bp, heads, dk, dv), F32),
            jax.ShapeDtypeStruct((bp, heads, 1, dk), F32),
            jax.ShapeDtypeStruct((bp, 1, LANES), F32),
            jax.ShapeDtypeStruct((depth, bs, heads, dk, dv), F32),
            jax.ShapeDtypeStruct((bs, heads, 1, dk), F32),
            jax.ShapeDtypeStruct((t_s, LANES), F32),
        ],
        scratch_shapes=[
            pltpu.VMEM((heads, dk, dv), F32),
            pltpu.VMEM((heads, 1, dk), F32),
            pltpu.VMEM((1, LANES), F32),
        ],
        input_output_aliases={len(args) - 1: 4},
        compiler_params=_cparams("arbitrary"),
        name="mlstm",
    )(*args)


def _out_kernel(x_ref, mr_ref, mm_ref, gtp_ref, gts_ref, shp_ref, shs_ref, scp_ref, scs_ref, wo_hbm, wr_ref, br_ref,
                xn_ref, h2_ref, eid_ref, wt_ref, wo_scr, stage, sem, *, layer, n_groups, per_group, n_prompt_tiles):
    i = pl.program_id(0)
    tm, d = x_ref.shape
    c = mr_ref.shape[1]
    rows = stage.shape[0]

    @pl.when(i == 0)
    def _():
        for part in range(wo_scr.shape[0] // rows):
            cp = pltpu.make_async_copy(wo_hbm.at[layer, pl.ds(part * rows, rows)], stage, sem)
            cp.start()
            cp.wait()
            wo_scr[pl.ds(part * rows, rows), :] = stage[...].astype(BF16)

    is_prompt = i < n_prompt_tiles
    acc = jnp.dot(mr_ref[...], wo_scr[:c, :], preferred_element_type=F32)
    acc = acc + jnp.dot(mm_ref[...], wo_scr[c:, :], preferred_element_type=F32)
    xn3 = _rows3(x_ref[...]) + _pick_mod(is_prompt, gtp_ref, gts_ref) * _rows3(acc)
    rs = lax.rsqrt(jnp.mean(xn3 * xn3, axis=-1, keepdims=True) + EPS)
    h2 = (xn3 * rs * (1.0 + _pick_mod(is_prompt, scp_ref, scs_ref))
          + _pick_mod(is_prompt, shp_ref, shs_ref)).reshape(tm, d)
    xn_ref[...] = xn3.reshape(tm, d)
    _store_slabs(h2_ref, h2)

    logits = jnp.dot(h2.astype(BF16), wr_ref[0], preferred_element_type=F32) + br_ref[0]
    lane = lax.broadcasted_iota(jnp.int32, logits.shape, 1)
    lane_f = lane.astype(F32)
    big = np.float32(LANES)

    def first_max(vals):
        vmax = jnp.max(vals, axis=-1, keepdims=True)
        idx = jnp.min(jnp.where(vals == vmax, lane_f, big), axis=-1, keepdims=True)
        return vmax, idx.astype(jnp.int32)

    gl = jnp.where(lane < n_groups, logits, -jnp.inf)
    gmax, g_sel = first_max(gl)
    p_g = 1.0 / jnp.sum(jnp.exp(gl - gmax), axis=-1, keepdims=True)
    lo = n_groups + g_sel * per_group
    el = jnp.where((lane >= lo) & (lane < lo + per_group), logits, -jnp.inf)
    v1, i1 = first_max(el)
    v2, i2 = first_max(jnp.where(lane == i1, -jnp.inf, el))
    e21 = jnp.exp(v2 - v1)
    w1 = 1.0 / (1.0 + e21)
    w2 = e21 * w1
    eid_ref[...] = jnp.where(lane == 0, i1 - n_groups, jnp.where(lane == 1, i2 - n_groups, 0))
    wt_ref[...] = jnp.where(lane == 0, w1 * p_g, jnp.where(lane == 1, w2 * p_g, 0.0))


def _out_call(x, mix_rg, mix_ml, mod_p, mod_s, w_out, w_router, b_router, layer, n_groups, per_group, tok):
    t, d = x.shape
    c = mix_rg.shape[1]
    cm = mix_ml.shape[1]
    tm = tok.tile
    stage_rows = min(256, c + cm)
    return pl.pallas_call(
        functools.partial(_out_kernel, layer=layer, n_groups=n_groups, per_group=per_group,
                          n_prompt_tiles=tok.n_prompt_tiles),
        grid=(t // tm,),
        in_specs=[
            pl.BlockSpec((tm, d), lambda i: (i, 0)),
            pl.BlockSpec((tm, c), lambda i: (i, 0)),
            pl.BlockSpec((tm, cm), lambda i: (i, 0)),
            *tok.mod_specs(layer, GT_A, d),
            *tok.mod_specs(layer, SH_F, d),
            *tok.mod_specs(layer, SC_F, d),
            pl.BlockSpec(memory_space=pl.ANY),
            pl.BlockSpec((1, d, LANES), lambda i: (layer, 0, 0)),
            pl.BlockSpec((1, 1, LANES), lambda i: (layer, 0, 0)),
        ],
        out_specs=[
            pl.BlockSpec((tm, d), lambda i: (i, 0)),
            pl.BlockSpec((tm * _slab_rows(d), LANES), lambda i: (i, 0)),
            pl.BlockSpec((tm, LANES), lambda i: (i, 0)),
            pl.BlockSpec((tm, LANES), lambda i: (i, 0)),
        ],
        out_shape=[
            jax.ShapeDtypeStruct((t, d), F32),
            jax.ShapeDtypeStruct((t * _slab_rows(d), LANES), F32),
            jax.ShapeDtypeStruct((t, LANES), jnp.int32),
            jax.ShapeDtypeStruct((t, LANES), F32),
        ],
        scratch_shapes=[
            pltpu.VMEM((c + cm, d), BF16),
            pltpu.VMEM((stage_rows, d), F32),
            pltpu.SemaphoreType.DMA(()),
        ],
        compiler_params=_cparams("arbitrary"),
        name="out_proj_router",
    )(x, mix_rg, mix_ml, mod_p, mod_s, mod_p, mod_s, mod_p, mod_s, w_out, w_router, b_router)


def _slab_rows(d):
    return d // LANES


def _store_slabs(ref, val):
    rows, d = val.shape
    slab = _slab_rows(d)
    for s in range(slab):
        ref[pl.ds(s, rows, stride=slab), :] = val[:, s * LANES:(s + 1) * LANES]


def _load_slabs(ref, rows, d):
    slab = _slab_rows(d)
    return jnp.concatenate([ref[pl.ds(s, rows, stride=slab), :] for s in range(slab)], axis=-1)


def _gather_slabs(idx_ref, base, n_rows, src_hbm, dst, sem, slab, stride=1, offset=0):
    def body(g, carry):
        for s in range(SUBLANES):
            r = g * SUBLANES + s
            tok = idx_ref[base + r * stride + offset]
            pltpu.make_async_copy(src_hbm.at[pl.ds(pl.multiple_of(tok * slab, slab), slab)],
                                  dst.at[pl.ds(pl.multiple_of(r * slab, slab), slab)], sem).start()
        return carry

    lax.fori_loop(0, n_rows // SUBLANES, body, 0)


def _wait_rows(buf, sem):
    pltpu.make_async_copy(buf, buf, sem).wait()


def _moe_kernel(te_ref, nx_ref, nu_ref, src_ref, h2_hbm, w1_hbm, w3_hbm, w2_hbm, y_ref,
                xbuf, st1, st3, st2, w1_scr, w3_scr, w2_scr, slot_ref, gsem, wsem,
                *, layer):
    j = pl.program_id(0)
    nbuf = xbuf.shape[0]
    d = w1_scr.shape[0]
    slab = _slab_rows(d)
    tm = xbuf.shape[1] // slab
    n_used = nu_ref[0]

    def weight_copies(expert, slot):
        return [pltpu.make_async_copy(w_hbm.at[layer, expert], stage.at[slot], wsem.at[slot])
                for w_hbm, stage in ((w1_hbm, st1), (w3_hbm, st3), (w2_hbm, st2))]

    def issue(tile):
        _gather_slabs(src_ref, tile * tm, tm, h2_hbm, xbuf.at[tile % nbuf], gsem.at[tile % nbuf], slab)

    @pl.when(j == 0)
    def _():
        slot_ref[0] = 1
        for cp in weight_copies(te_ref[0], 0):
            cp.start()
        for ahead in range(nbuf - 1):
            @pl.when(ahead < n_used)
            def _():
                issue(ahead)

    @pl.when(j + (nbuf - 1) < n_used)
    def _():
        issue(j + (nbuf - 1))

    @pl.when((j < n_used) & ((j == 0) | (te_ref[j] != te_ref[jnp.maximum(j - 1, 0)])))
    def _():
        slot = 1 - slot_ref[0]
        slot_ref[0] = slot
        for cp in weight_copies(te_ref[j], slot):
            cp.wait()
        w1_scr[...] = st1[slot].astype(BF16)
        w3_scr[...] = st3[slot].astype(BF16)
        w2_scr[...] = st2[slot].astype(BF16)

        @pl.when(nx_ref[j] >= 0)
        def _():
            for cp in weight_copies(nx_ref[j], 1 - slot):
                cp.start()

    @pl.when(j < n_used)
    def _():
        slot = j % nbuf
        _wait_rows(xbuf.at[slot], gsem.at[slot])
        xb = _load_slabs(xbuf.at[slot], tm, d).astype(BF16)
        a = jnp.dot(xb, w1_scr[...], preferred_element_type=F32)
        b = jnp.dot(xb, w3_scr[...], preferred_element_type=F32)
        hid = (a * _sigmoid(a)) * b
        _store_slabs(y_ref, jnp.dot(hid.astype(BF16), w2_scr[...], preferred_element_type=F32))

    @pl.when(j >= n_used)
    def _():
        y_ref[...] = jnp.zeros_like(y_ref)


def _moe_call(tile_exp, next_exp, n_used, src_tok, h2_slabs, w1, w3, w2, layer, tm):
    _, ne, d, f = w1.shape
    slab = _slab_rows(d)
    n_tiles = tile_exp.shape[0]
    gather_bufs = 3
    grid_spec = pltpu.PrefetchScalarGridSpec(
        num_scalar_prefetch=4,
        grid=(n_tiles,),
        in_specs=[pl.BlockSpec(memory_space=pl.ANY)] * 4,
        out_specs=pl.BlockSpec((tm * slab, LANES), lambda j, *_: (j, 0)),
        scratch_shapes=[
            pltpu.VMEM((gather_bufs, tm * slab, LANES), F32),
            pltpu.VMEM((2, d, f), F32),
            pltpu.VMEM((2, d, f), F32),
            pltpu.VMEM((2, f, d), F32),
            pltpu.VMEM((d, f), BF16),
            pltpu.VMEM((d, f), BF16),
            pltpu.VMEM((f, d), BF16),
            pltpu.SMEM((1,), jnp.int32),
            pltpu.SemaphoreType.DMA((gather_bufs,)),
            pltpu.SemaphoreType.DMA((2,)),
        ],
    )
    return pl.pallas_call(
        functools.partial(_moe_kernel, layer=layer),
        grid_spec=grid_spec,
        out_shape=jax.ShapeDtypeStruct((n_tiles * tm * slab, LANES), F32),
        compiler_params=_cparams("arbitrary"),
        name="experts",
    )(tile_exp, next_exp, n_used, src_tok, h2_slabs, w1, w3, w2)


def _comb_kernel(dest_ref, xn_ref, wt_ref, gtp_ref, gts_ref, *rest, final, heads, n_prompt_tiles):
    if final:
        gf_ref, y_hbm, yp_ref, ys_ref, ybuf, sem = rest
    else:
        shp_ref, shs_ref, scp_ref, scs_ref, wg_ref, bg_ref, y_hbm, xo_ref, h_ref, g_ref, ybuf, sem = rest
    i = pl.program_id(0)
    n = pl.num_programs(0)
    tm, d = xn_ref.shape
    is_prompt = i < n_prompt_tiles

    def issue(tile, slot):
        for k in range(2):
            _gather_slabs(dest_ref, tile * tm * 2, tm, y_hbm, ybuf.at[slot, k], sem.at[slot], _slab_rows(d),
                          stride=2, offset=k)

    @pl.when(i == 0)
    def _():
        issue(0, 0)

    @pl.when(i + 1 < n)
    def _():
        issue(i + 1, (i + 1) % 2)

    slot = i % 2
    for k in range(2):
        _wait_rows(ybuf.at[slot, k], sem.at[slot])
    wt = wt_ref[...]
    y = wt[:, 0:1] * _load_slabs(ybuf.at[slot, 0], tm, d) + wt[:, 1:2] * _load_slabs(ybuf.at[slot, 1], tm, d)
    xo3 = _rows3(xn_ref[...]) + _pick_mod(is_prompt, gtp_ref, gts_ref) * _rows3(y)
    if final:
        xo = xo3.reshape(tm, d)
        yo = xo * lax.rsqrt(jnp.mean(xo * xo, axis=-1, keepdims=True) + EPS) * gf_ref[...]

        @pl.when(is_prompt)
        def _():
            yp_ref[...] = yo

        @pl.when(jnp.logical_not(is_prompt))
        def _():
            ys_ref[...] = yo
    else:
        xo_ref[...] = xo3.reshape(tm, d)
        _norm_gates(xo3, _pick_mod(is_prompt, scp_ref, scs_ref), _pick_mod(is_prompt, shp_ref, shs_ref),
                    wg_ref, bg_ref, h_ref, g_ref, heads)


def _comb_call(dest, xn, wts, mod_p, mod_s, w_gate, b_gate, g_final, y, layer, heads, tok, final):
    t, d = xn.shape
    tm = tok.tile
    full = lambda i, dst: (0, 0)
    in_specs = [
        pl.BlockSpec((tm, d), lambda i, dst: (i, 0)),
        pl.BlockSpec((tm, LANES), lambda i, dst: (i, 0)),
        *tok.mod_specs(layer, GT_F, d),
    ]
    args = [xn, wts, mod_p, mod_s]
    if final:
        n_p = tok.n_prompt_tiles
        in_specs.append(pl.BlockSpec((1, d), full))
        args.append(g_final)
        out_specs = [pl.BlockSpec((tm, d), lambda i, dst: (jnp.minimum(i, n_p - 1), 0)),
                     pl.BlockSpec((tm, d), lambda i, dst: (tok.sample_tile(i), 0))]
        out_shape = [jax.ShapeDtypeStruct((n_p * tm, d), F32), jax.ShapeDtypeStruct((t - n_p * tm, d), F32)]
    else:
        out_specs = [pl.BlockSpec((tm, d), lambda i, dst: (i, 0))]
        out_shape = [jax.ShapeDtypeStruct((t, d), F32)]
        nxt = layer + 1
        in_specs += [
            *tok.mod_specs(nxt, SH_A, d),
            *tok.mod_specs(nxt, SC_A, d),
            pl.BlockSpec((1, d, LANES), lambda i, dst: (nxt, 0, 0)),
            pl.BlockSpec((1, 1, LANES), lambda i, dst: (nxt, 0, 0)),
        ]
        args += [mod_p, mod_s, mod_p, mod_s, w_gate, b_gate]
        out_specs += [pl.BlockSpec((tm, d), lambda i, dst: (i, 0)), pl.BlockSpec((tm, LANES), lambda i, dst: (i, 0))]
        out_shape += [jax.ShapeDtypeStruct((t, d), BF16), jax.ShapeDtypeStruct((t, LANES), F32)]
    in_specs.append(pl.BlockSpec(memory_space=pl.ANY))
    args.append(y)
    grid_spec = pltpu.PrefetchScalarGridSpec(
        num_scalar_prefetch=1,
        grid=(t // tm,),
        in_specs=in_specs,
        out_specs=out_specs,
        scratch_shapes=[pltpu.VMEM((2, 2, tm * _slab_rows(d), LANES), F32), pltpu.SemaphoreType.DMA((2,))],
    )
    return pl.pallas_call(
        functools.partial(_comb_kernel, final=final, heads=heads, n_prompt_tiles=tok.n_prompt_tiles),
        grid_spec=grid_spec,
        out_shape=out_shape,
        compiler_params=_cparams("arbitrary"),
        name="combine",
    )(dest, *args)


def _invert_kernel(dest_ref, src_ref):
    n_pairs = dest_ref.shape[0]
    n_rows = src_ref.shape[0]

    def clear(i, carry):
        for u in range(INVERT_CLEAR_UNROLL):
            src_ref[i * INVERT_CLEAR_UNROLL + u] = 0
        return carry

    lax.fori_loop(0, n_rows // INVERT_CLEAR_UNROLL, clear, 0)

    def place(i, carry):
        for u in range(SUBLANES):
            pair = i * SUBLANES + u
            src_ref[dest_ref[pair]] = lax.shift_right_logical(pair, 1)
        return carry

    lax.fori_loop(0, n_pairs // SUBLANES, place, 0)


def _invert_call(dest, n_rows):
    assert dest.shape[0] % SUBLANES == 0 and n_rows % INVERT_CLEAR_UNROLL == 0
    return pl.pallas_call(
        _invert_kernel,
        in_specs=[pl.BlockSpec(memory_space=pltpu.SMEM)],
        out_specs=pl.BlockSpec(memory_space=pltpu.SMEM),
        out_shape=jax.ShapeDtypeStruct((n_rows,), jnp.int32),
        name="route_invert",
    )(dest)


def _route_plan(eid, n_experts, tm):
    t = eid.shape[0]
    e_flat = eid[:, :2].reshape(-1)
    n_pairs = 2 * t
    n_tiles = -(-n_pairs // tm) + n_experts
    onehot = (e_flat[:, None] == jnp.arange(n_experts, dtype=jnp.int32)[None, :]).astype(jnp.int32)
    csum = jnp.cumsum(onehot, axis=0)
    rank = jnp.sum(csum * onehot, axis=1) - 1
    counts = csum[-1]
    padded = ((counts + tm - 1) // tm) * tm
    ends = jnp.cumsum(padded)
    starts = ends - padded
    dest = (jnp.sum(starts[None, :] * onehot, axis=1) + rank).astype(jnp.int32)
    src_tok = _invert_call(dest, n_tiles * tm)
    n_used = (ends[-1] // tm).astype(jnp.int32)
    tile_id = jnp.minimum(jnp.arange(n_tiles, dtype=jnp.int32), n_used - 1)
    tile_exp = jnp.sum((ends[None, :] <= (tile_id * tm)[:, None]).astype(jnp.int32), axis=1)
    tile_exp = jnp.minimum(tile_exp, n_experts - 1).astype(jnp.int32)
    ids = jnp.arange(n_experts, dtype=jnp.int32)
    later = (ids[None, :] > ids[:, None]) & (padded[None, :] > 0)
    following = jnp.min(jnp.where(later, ids[None, :], n_experts), axis=1)
    following = jnp.where(following == n_experts, -1, following).astype(jnp.int32)
    return tile_exp, following[tile_exp], n_used.reshape(1), src_tok, dest


def _pick_tile(total, pref):
    tile = min(pref, total)
    while total % tile:
        tile //= 2
    return tile


def kernel(x_prompt, x_sample, c_prompt, c_sample, state_rg_conv, state_rg_h, state_mlstm_C, state_mlstm_n,
           state_mlstm_m, w_mod, b_mod, w_in, conv_w, conv_b, w_ra, b_ra, w_ri, b_ri, lam, g_rg, b_ig, b_fg,
           g_ml, w_out, w_grp, b_grp, w_er, b_er, w1, w3, w2, g_final):
    bp, seq, d = x_prompt.shape
    bs, dec_seq, _ = x_sample.shape
    depth = w_mod.shape[0]
    d_rg = conv_w.shape[2]
    conv_width = conv_w.shape[1]
    _, _, heads, dk, dv = state_mlstm_C.shape
    d_ml = heads * dv
    n_groups, per_group = w_er.shape[1], w_er.shape[3]
    n_experts = w1.shape[1]
    t_p, t_s = bp * seq, bs * dec_seq
    t = t_p + t_s
    n_main = 2 * d_rg + 2 * heads * dk + 2 * d_ml
    q_off, k_off, v_off, o_off = 2 * d_rg, 2 * d_rg + heads * dk, 2 * d_rg + 2 * heads * dk, n_main - d_ml
    assert dec_seq == SUBLANES and seq % LANES == 0 and conv_width - 1 <= SUBLANES
    assert 2 * heads <= LANES and n_groups * (per_group + 1) <= LANES
    assert q_off % (heads * dk) == 0 and k_off % (heads * dk) == 0 and v_off % d_ml == 0 and o_off % d_ml == 0

    tok = _Tokens(t_p, t_s, seq, _pick_tile(np.gcd(seq, t_s), 256))
    tok_out = _Tokens(t_p, t_s, seq, _pick_tile(np.gcd(seq, t_s), 512))
    tm_in = _pick_tile(np.gcd(t_p, t_s), 1024)
    tc = LANES
    assert t_s % tc == 0 and seq % tc == 0

    mc = -(-(bp + bs) // SUBLANES) * SUBLANES
    c_all = jnp.zeros((mc, d), F32).at[:bp].set(c_prompt).at[bp:bp + bs].set(c_sample)
    mod = _mod_call(c_all, w_mod, b_mod, _pick_tile(6 * d, 1024))
    mod_p = mod[:, :bp].reshape(depth, bp, 1, 6 * d)
    mod_s = mod[:, bp:bp + bs].reshape(depth, bs, 1, 6 * d)

    w_in_t = jnp.swapaxes(w_in, 1, 2)
    w_gate = jnp.zeros((depth, d, LANES), F32).at[:, :, :2 * heads].set(w_in[:, :, n_main:]).astype(BF16)
    b_gate = jnp.zeros((depth, 1, LANES), F32).at[:, 0, :heads].set(b_ig).at[:, 0, heads:2 * heads].set(b_fg)
    w_ra_b, w_ri_b = w_ra.astype(BF16), w_ri.astype(BF16)
    n_rt = n_groups * (per_group + 1)
    w_router = jnp.concatenate([w_grp, jnp.moveaxis(w_er, 1, 2).reshape(depth, d, n_groups * per_group)], axis=-1)
    w_router = jnp.zeros((depth, d, LANES), F32).at[:, :, :n_rt].set(w_router).astype(BF16)
    b_router = jnp.concatenate([b_grp, b_er.reshape(depth, -1)], axis=-1)
    b_router = jnp.zeros((depth, 1, LANES), F32).at[:, 0, :n_rt].set(b_router)
    vec = lambda p: p.reshape(depth, 1, -1)

    pad_rows = SUBLANES - (conv_width - 1)
    prev_s = jnp.pad(state_rg_conv, ((0, 0), (0, 0), (pad_rows, 0), (0, 0))).reshape(depth, bs * SUBLANES, d_rg)
    h0_s = state_rg_h.reshape(depth, bs, 1, d_rg)
    n_s = state_mlstm_n.reshape(depth, bs, heads, 1, dk)
    m_s_col = jnp.pad(jnp.repeat(state_mlstm_m, dec_seq, axis=1), ((0, 0), (0, 0), (0, LANES - heads)))

    x, h, gcol = _prenorm_call(x_prompt.reshape(t_p, d), x_sample.reshape(t_s, d), mod_p, mod_s, w_gate, b_gate,
                               heads, tok)
    outs_p = [[] for _ in range(5)]
    outs_s = [[] for _ in range(5)]
    cs_stack = jnp.zeros_like(state_mlstm_C)
    for l in range(depth):
        u = _in_call(h, w_in_t, l, n_main, tm_in, _pick_tile(n_main, IN_PROJ_TILE_COLS))
        hr, mix_rg = _rg_call(u, prev_s, h0_s, conv_w, vec(conv_b), w_ra_b, vec(b_ra), w_ri_b, vec(b_ri),
                              vec(lam), vec(g_rg), l, tok)
        grow = gcol[:, :2 * heads].T
        mix_ml, c_p, n_p, m_p, cs_stack, n_so, m_so = _ml_call(
            u, gcol, grow, state_mlstm_C, n_s, m_s_col, vec(g_ml), cs_stack, l, t_p, seq, dec_seq,
            q_off, k_off, v_off, o_off, tc)
        xn, h2, eid, wts = _out_call(x, mix_rg, mix_ml, mod_p, mod_s, w_out, w_router, b_router, l,
                                     n_groups, per_group, tok_out)
        tile_exp, next_exp, n_used, src_tok, dest = _route_plan(eid, n_experts, EXPERT_TILE_ROWS)
        y = _moe_call(tile_exp, next_exp, n_used, src_tok, h2, w1, w3, w2, l, EXPERT_TILE_ROWS)
        final = l == depth - 1
        res = _comb_call(dest, xn, wts, mod_p, mod_s, w_gate, b_gate, g_final[None], y, l, heads, tok, final)
        if final:
            y_p, y_s = res
        else:
            x, h, gcol = res

        tail = conv_width - 1
        outs_p[0].append(jnp.stack([u[(b + 1) * seq - tail:(b + 1) * seq, :d_rg] for b in range(bp)]))
        outs_p[1].append(jnp.stack([hr[(b + 1) * seq - 1] for b in range(bp)]))
        outs_p[2].append(c_p)
        outs_p[3].append(n_p.reshape(bp, heads, dk))
        outs_p[4].append(m_p[:, 0, :heads])
        outs_s[0].append(u[t_p:, :d_rg].reshape(bs, dec_seq, d_rg)[:, dec_seq - tail:])
        outs_s[1].append(hr[t_p:].reshape(bs, dec_seq, d_rg)[:, -1])
        outs_s[3].append(n_so.reshape(bs, heads, dk))
        outs_s[4].append(m_so.reshape(bs, dec_seq, LANES)[:, -1, :heads])

    stack = lambda parts: jnp.stack(parts)
    return (y_p.reshape(bp, seq, d), y_s.reshape(bs, dec_seq, d), *[stack(o) for o in outs_p],
            stack(outs_s[0]), stack(outs_s[1]), cs_stack, stack(outs_s[3]), stack(outs_s[4]))
```

```python
import functools

import numpy as np
import jax
import jax.numpy as jnp
from jax import lax
from jax.experimental import pallas as pl
from jax.experimental.pallas import tpu as pltpu

F32 = jnp.float32
BF16 = jnp.bfloat16
EPS = 1e-6
RG_C = 8.0
SUBLANES = 8
LANES = 128
VMEM_LIMIT = 56 * 1024 * 1024
HI = lax.Precision.HIGHEST
INVERT_CLEAR_UNROLL = 32
EXPERT_TILE_ROWS = 128
IN_PROJ_TILE_COLS = 1280
SH_A, SC_A, GT_A, SH_F, SC_F, GT_F = range(6)


def _cparams(*sem):
    return pltpu.CompilerParams(dimension_semantics=sem, vmem_limit_bytes=VMEM_LIMIT)


def _sigmoid(z):
    return 0.5 * jnp.tanh(0.5 * z) + 0.5


def _softplus(z):
    return jnp.maximum(z, 0.0) + jnp.log1p(jnp.exp(-jnp.abs(z)))


def _rows3(x):
    r, c = x.shape
    return x.reshape(r // SUBLANES, SUBLANES, c)


class _Tokens:
    def __init__(self, t_prompt, t_sample, seq, tile):
        self.tile = tile
        self.groups = tile // SUBLANES
        self.n_prompt_tiles = t_prompt // tile
        self.tiles_per_seq = seq // tile
        self.n_prompt_seqs = t_prompt // seq

    def prompt_seq(self, i):
        return jnp.minimum(i // self.tiles_per_seq, self.n_prompt_seqs - 1)

    def sample_tile(self, i):
        return jnp.maximum(i - self.n_prompt_tiles, 0)

    def mod_specs(self, layer, chunk, d, tile_axis=0):
        def prompt_map(*idx):
            return (layer, self.prompt_seq(idx[tile_axis]), 0, chunk)

        def sample_map(*idx):
            return (layer, self.sample_tile(idx[tile_axis]), 0, chunk)

        return [pl.BlockSpec((1, 1, 1, d), prompt_map), pl.BlockSpec((1, self.groups, 1, d), sample_map)]


def _pick_mod(is_prompt, mp_ref, ms_ref):
    return jnp.where(is_prompt, mp_ref[0], ms_ref[0])


def _mod_kernel(c_ref, w_ref, b_ref, o_ref):
    c = c_ref[...]
    s = (c * _sigmoid(c)).astype(BF16)
    o_ref[0] = jnp.dot(s, w_ref[0].astype(BF16), preferred_element_type=F32) + b_ref[0]


def _mod_call(c_all, w_mod, b_mod, tn):
    depth, d, n = w_mod.shape
    mc = c_all.shape[0]
    return pl.pallas_call(
        _mod_kernel,
        grid=(depth, n // tn),
        in_specs=[
            pl.BlockSpec((mc, d), lambda l, j: (0, 0)),
            pl.BlockSpec((1, d, tn), lambda l, j: (l, 0, j)),
            pl.BlockSpec((1, 1, tn), lambda l, j: (l, 0, j)),
        ],
        out_specs=pl.BlockSpec((1, mc, tn), lambda l, j: (l, 0, j)),
        out_shape=jax.ShapeDtypeStruct((depth, mc, n), F32),
        compiler_params=_cparams("arbitrary", "arbitrary"),
        name="mod",
    )(c_all, w_mod, b_mod.reshape(depth, 1, n))


def _norm_gates(x3, sc, sh, wg_ref, bg_ref, h_ref, g_ref, heads):
    groups, _, d = x3.shape
    xn = x3 * lax.rsqrt(jnp.mean(x3 * x3, axis=-1, keepdims=True) + EPS)
    hb = (xn * (1.0 + sc) + sh).reshape(groups * SUBLANES, d).astype(BF16)
    h_ref[...] = hb
    g = jnp.dot(hb, wg_ref[0], preferred_element_type=F32) + bg_ref[0]
    lane = lax.broadcasted_iota(jnp.int32, g.shape, 1)
    is_forget = (lane >= heads) & (lane < 2 * heads)
    g_ref[...] = jnp.where(is_forget, -_softplus(-g), g)


def _prenorm_kernel(xp_ref, xs_ref, shp_ref, shs_ref, scp_ref, scs_ref, wg_ref, bg_ref, x_ref, h_ref, g_ref,
                    *, heads, n_prompt_tiles):
    is_prompt = pl.program_id(0) < n_prompt_tiles
    x = jnp.where(is_prompt, xp_ref[...], xs_ref[...])
    x_ref[...] = x
    _norm_gates(_rows3(x), _pick_mod(is_prompt, scp_ref, scs_ref), _pick_mod(is_prompt, shp_ref, shs_ref),
                wg_ref, bg_ref, h_ref, g_ref, heads)


def _prenorm_call(x_p, x_s, mod_p, mod_s, w_gate, b_gate, heads, tok):
    d = x_p.shape[1]
    t = x_p.shape[0] + x_s.shape[0]
    tm = tok.tile
    n_p = tok.n_prompt_tiles
    return pl.pallas_call(
        functools.partial(_prenorm_kernel, heads=heads, n_prompt_tiles=n_p),
        grid=(t // tm,),
        in_specs=[
            pl.BlockSpec((tm, d), lambda i: (jnp.minimum(i, n_p - 1), 0)),
            pl.BlockSpec((tm, d), lambda i: (tok.sample_tile(i), 0)),
            *tok.mod_specs(0, SH_A, d),
            *tok.mod_specs(0, SC_A, d),
            pl.BlockSpec((1, d, LANES), lambda i: (0, 0, 0)),
            pl.BlockSpec((1, 1, LANES), lambda i: (0, 0, 0)),
        ],
        out_specs=[
            pl.BlockSpec((tm, d), lambda i: (i, 0)),
            pl.BlockSpec((tm, d), lambda i: (i, 0)),
            pl.BlockSpec((tm, LANES), lambda i: (i, 0)),
        ],
        out_shape=[
            jax.ShapeDtypeStruct((t, d), F32),
            jax.ShapeDtypeStruct((t, d), BF16),
            jax.ShapeDtypeStruct((t, LANES), F32),
        ],
        compiler_params=_cparams("arbitrary"),
        name="prenorm",
    )(x_p, x_s, mod_p, mod_s, mod_p, mod_s, w_gate, b_gate)


def _in_kernel(h_hbm, w_ref, u_ref, wb_scr, hbuf, sem):
    n_tiles = pl.num_programs(1)
    total = pl.num_programs(0) * n_tiles
    step = pl.program_id(0) * n_tiles + pl.program_id(1)
    ring, tm, _ = hbuf.shape

    def fetch(s):
        rows = pl.ds(pl.multiple_of(lax.rem(s, n_tiles) * tm, tm), tm)
        pltpu.make_async_copy(h_hbm.at[rows], hbuf.at[lax.rem(s, ring)], sem.at[lax.rem(s, ring)]).start()

    @pl.when(step == 0)
    def _():
        for ahead in range(ring - 1):
            @pl.when(ahead < total)
            def _():
                fetch(step + ahead)

    @pl.when(step + (ring - 1) < total)
    def _():
        fetch(step + (ring - 1))

    @pl.when(pl.program_id(1) == 0)
    def _():
        wb_scr[...] = w_ref[0].astype(BF16)

    slot = lax.rem(step, ring)
    pltpu.make_async_copy(hbuf.at[slot], hbuf.at[slot], sem.at[slot]).wait()
    u_ref[...] = lax.dot_general(hbuf[slot], wb_scr[...], (((1,), (1,)), ((), ())), preferred_element_type=F32)


def _in_call(h, w_in_t, layer, n_main, tm, tn):
    t, d = h.shape
    return pl.pallas_call(
        _in_kernel,
        grid=(n_main // tn, t // tm),
        in_specs=[
            pl.BlockSpec(memory_space=pl.ANY),
            pl.BlockSpec((1, tn, d), lambda j, i: (layer, j, 0)),
        ],
        out_specs=pl.BlockSpec((tm, tn), lambda j, i: (i, j)),
        out_shape=jax.ShapeDtypeStruct((t, n_main), F32),
        scratch_shapes=[pltpu.VMEM((tn, d), BF16), pltpu.VMEM((3, tm, d), BF16), pltpu.SemaphoreType.DMA((3,))],
        compiler_params=_cparams("arbitrary", "arbitrary"),
        name="in_proj",
    )(h, w_in_t)


def _rg_kernel(xr_ref, yr_ref, prev_ref, h0_ref, cw_ref, cb_ref, wra_ref, bra_ref, wri_ref, bri_ref,
               lam_ref, gain_ref, hr_ref, mix_ref, xprev_scr, hprev_scr, hstart_scr, a_scr, b_scr,
               *, n_prompt_tiles, tiles_per_seq):
    i = pl.program_id(0)
    tl, c = xr_ref.shape
    groups = tl // SUBLANES
    _, nblk, bw, _ = wra_ref.shape
    is_prompt = i < n_prompt_tiles

    @pl.when(is_prompt & (i % tiles_per_seq == 0))
    def _():
        xprev_scr[...] = jnp.zeros_like(xprev_scr)
        hprev_scr[...] = jnp.zeros_like(hprev_scr)

    x = xr_ref[...]
    prev_prompt = jnp.concatenate([xprev_scr[...], x[:tl - SUBLANES]], axis=0)
    prev = jnp.where(is_prompt, prev_prompt, prev_ref[0])
    xprev_scr[...] = x[tl - SUBLANES:]

    x3 = _rows3(x)
    p3 = _rows3(prev)
    row = lax.broadcasted_iota(jnp.int32, x3.shape, 1)
    cw = cw_ref[0]
    conv_w = cw.shape[0]
    xc = cb_ref[0] + x3 * cw[conv_w - 1:conv_w]
    for dlt in range(1, conv_w):
        shifted = jnp.where(row >= dlt, pltpu.roll(x3, dlt, 1), pltpu.roll(p3, dlt, 1))
        xc = xc + shifted * cw[conv_w - 1 - dlt:conv_w - dlt]

    xc2 = xc.reshape(tl, c)
    xcb = xc2.astype(BF16)
    r_parts, i_parts = [], []
    for nb in range(nblk):
        blk = xcb[:, nb * bw:(nb + 1) * bw]
        r_parts.append(jnp.dot(blk, wra_ref[0, nb], preferred_element_type=F32))
        i_parts.append(jnp.dot(blk, wri_ref[0, nb], preferred_element_type=F32))
    r = _sigmoid(jnp.concatenate(r_parts, axis=-1) + bra_ref[0])
    ig = _sigmoid(jnp.concatenate(i_parts, axis=-1) + bri_ref[0])
    log_a = (-RG_C) * r * _softplus(-lam_ref[0])
    a = jnp.exp(log_a)
    th = jnp.tanh(log_a)
    gap = -2.0 * th / (1.0 - th)
    mult = jnp.where(gap > 0.0, gap * lax.rsqrt(gap), 0.0)
    bx = mult * (ig * xc2)

    av = _rows3(a)
    bv = _rows3(bx)
    for s in (1, 2, 4):
        a_sh = jnp.where(row >= s, pltpu.roll(av, s, 1), 1.0)
        b_sh = jnp.where(row >= s, pltpu.roll(bv, s, 1), 0.0)
        bv = av * b_sh + bv
        av = av * a_sh

    @pl.when(is_prompt)
    def _():
        a_scr[...] = av.reshape(tl, c)
        b_scr[...] = bv.reshape(tl, c)

        def body(g, h):
            hstart_scr[g] = h
            last = g * SUBLANES + (SUBLANES - 1)
            return a_scr[pl.ds(last, 1), :] * h + b_scr[pl.ds(last, 1), :]

        hprev_scr[...] = lax.fori_loop(0, groups, body, hprev_scr[...])

    @pl.when(jnp.logical_not(is_prompt))
    def _():
        hstart_scr[...] = h0_ref[0]

    h3 = av * hstart_scr[...] + bv
    hr = h3.reshape(tl, c)
    hr_ref[...] = hr
    y = yr_ref[...]
    gelu = 0.5 * y * (1.0 + jnp.tanh(np.sqrt(2.0 / np.pi).astype(np.float32) * (y + 0.044715 * (y * y * y))))
    hn = hr * lax.rsqrt(jnp.mean(hr * hr, axis=-1, keepdims=True) + EPS)
    mix_ref[...] = (hn * gain_ref[0] * gelu).astype(BF16)


def _rg_call(u, prev_s, h0_s, conv_w, conv_b, w_ra, b_ra, w_ri, b_ri, lam, g_rg, layer, tok):
    t = u.shape[0]
    _, conv_width, c = conv_w.shape
    _, nblk, bw, _ = w_ra.shape
    tl, groups = tok.tile, tok.groups
    vec = pl.BlockSpec((1, 1, c), lambda i: (layer, 0, 0))
    return pl.pallas_call(
        functools.partial(_rg_kernel, n_prompt_tiles=tok.n_prompt_tiles, tiles_per_seq=tok.tiles_per_seq),
        grid=(t // tl,),
        in_specs=[
            pl.BlockSpec((tl, c), lambda i: (i, 0)),
            pl.BlockSpec((tl, c), lambda i: (i, 1)),
            pl.BlockSpec((1, tl, c), lambda i: (layer, tok.sample_tile(i), 0)),
            pl.BlockSpec((1, groups, 1, c), lambda i: (layer, tok.sample_tile(i), 0, 0)),
            pl.BlockSpec((1, conv_width, c), lambda i: (layer, 0, 0)),
            vec,
            pl.BlockSpec((1, nblk, bw, bw), lambda i: (layer, 0, 0, 0)),
            vec,
            pl.BlockSpec((1, nblk, bw, bw), lambda i: (layer, 0, 0, 0)),
            vec,
            vec,
            vec,
        ],
        out_specs=[
            pl.BlockSpec((tl, c), lambda i: (i, 0)),
            pl.BlockSpec((tl, c), lambda i: (i, 0)),
        ],
        out_shape=[
            jax.ShapeDtypeStruct((t, c), F32),
            jax.ShapeDtypeStruct((t, c), BF16),
        ],
        scratch_shapes=[
            pltpu.VMEM((SUBLANES, c), F32),
            pltpu.VMEM((1, c), F32),
            pltpu.VMEM((groups, 1, c), F32),
            pltpu.VMEM((tl, c), F32),
            pltpu.VMEM((tl, c), F32),
        ],
        compiler_params=_cparams("arbitrary"),
        name="rg_lru",
    )(u, u, prev_s, h0_s, conv_w, conv_b, w_ra, b_ra, w_ri, b_ri, lam, g_rg)


def _ml_chunk(nseq, heads, dk, dv, q_ref, k_ref, v_ref, o_ref, gc, gr, mprev, gain_ref, mix_ref,
              get_c, get_n, put_state):
    tc = q_ref.shape[0]
    ls = tc // nseq
    shift = int(np.log2(ls))
    t_idx = lax.broadcasted_iota(jnp.int32, (tc, tc), 0)
    s_idx = lax.broadcasted_iota(jnp.int32, (tc, tc), 1)
    mask = (lax.shift_right_logical(t_idx, shift) == lax.shift_right_logical(s_idx, shift)) & (s_idx <= t_idx)
    m_cum = mask.astype(F32)
    bcol = jnp.dot(m_cum, gc, precision=HI, preferred_element_type=F32)
    brow = lax.dot_general(gr, m_cum, (((1,), (1,)), ((), ())), precision=HI,
                           preferred_element_type=F32)
    lane = lax.broadcasted_iota(jnp.int32, (tc, LANES), 1)
    scale = np.float32(dk ** -0.5)

    def seg_last(col):
        if nseq == 1:
            return col[tc - 1:tc, :].reshape(1, 1, 1)
        return col.reshape(nseq, ls, 1)[:, ls - 1:ls, :]

    def seg_bcast(val):
        return jnp.broadcast_to(val, (nseq, ls, 1)).reshape(tc, 1)

    m_out = jnp.zeros((tc, LANES), F32)
    for h in range(heads):
        bc = bcol[:, heads + h:heads + h + 1]
        br = brow[heads + h:heads + h + 1, :]
        igr = gr[h:h + 1, :]
        igc = gc[:, h:h + 1]
        mp = mprev[:, h:h + 1]
        dmat = jnp.where(mask, bc - br + igr, -jnp.inf)
        m_inter = bc + mp
        m_t = jnp.maximum(m_inter, jnp.max(dmat, axis=-1, keepdims=True))
        p = jnp.exp(dmat - m_t)
        qh = q_ref[:, h * dk:(h + 1) * dk] * scale
        kh = k_ref[:, h * dk:(h + 1) * dk]
        vh = v_ref[:, h * dv:(h + 1) * dv]
        qb = qh.astype(BF16)
        kb = kh.astype(BF16)
        s = lax.dot_general(qb, kb, (((1,), (1,)), ((), ())), preferred_element_type=F32) * p
        w_inter = jnp.exp(m_inter - m_t)
        c_old = get_c(h)
        n_old = get_n(h)
        q3 = qh.reshape(nseq, ls, dk)
        inter = lax.dot_general(q3.astype(BF16), c_old.astype(BF16), (((2,), (1,)), ((0,), (0,))),
                                preferred_element_type=F32).reshape(tc, dv)
        qn = jnp.sum(q3 * n_old, axis=-1, keepdims=True).reshape(tc, 1)
        num = jnp.dot(s.astype(BF16), vh.astype(BF16), preferred_element_type=F32) + w_inter * inter
        den = jnp.sum(s, axis=-1, keepdims=True) + w_inter * qn
        hh = num / jnp.maximum(jnp.abs(den), jnp.exp(-m_t))
        hn = hh * lax.rsqrt(jnp.mean(hh * hh, axis=-1, keepdims=True) + EPS)
        og = _sigmoid(o_ref[:, h * dv:(h + 1) * dv])
        mix_ref[:, h * dv:(h + 1) * dv] = (hn * gain_ref[0, :, h * dv:(h + 1) * dv] * og).astype(BF16)

        m_new = seg_last(m_t)
        b_last = seg_last(bc)
        wl = jnp.exp(seg_bcast(b_last) - bc + igc - seg_bcast(m_new))
        decay = jnp.exp(b_last + seg_last(mp) - m_new)
        wv3 = (wl * vh).reshape(nseq, ls, dv).astype(BF16)
        k3 = kh.reshape(nseq, ls, dk).astype(BF16)
        upd = lax.dot_general(k3, wv3, (((1,), (1,)), ((0,), (0,))), preferred_element_type=F32)
        c_new = decay * c_old + upd
        n_new = decay * n_old + jnp.sum((wl * kh).reshape(nseq, ls, dk), axis=1, keepdims=True)
        put_state(h, c_new, n_new)
        m_out = jnp.where(lane == h, seg_bcast(m_new), m_out)
    return m_out


def _ml_kernel(q_ref, k_ref, v_ref, o_ref, gc_ref, gr_ref, cs_ref, ns_ref, ms_ref, gain_ref, cs_alias,
               mix_ref, cp_out, np_out, mp_out, cs_out, ns_out, ms_out, c_scr, n_scr, m_scr,
               *, n_prompt_chunks, chunks_per_seq, heads, dk, dv, nseq_sample):
    del cs_alias
    i = pl.program_id(0)
    tc = q_ref.shape[0]
    is_prompt = i < n_prompt_chunks

    @pl.when(is_prompt & (i % chunks_per_seq == 0))
    def _():
        c_scr[...] = jnp.zeros_like(c_scr)
        n_scr[...] = jnp.zeros_like(n_scr)
        m_scr[...] = jnp.zeros_like(m_scr)

    @pl.when(is_prompt)
    def _():
        def put(h, c_new, n_new):
            c_scr[h] = c_new[0]
            n_scr[h] = n_new[0]

        mprev = jnp.broadcast_to(m_scr[...], (tc, LANES))
        m_out = _ml_chunk(1, heads, dk, dv, q_ref, k_ref, v_ref, o_ref, gc_ref[...], gr_ref[...], mprev,
                          gain_ref, mix_ref, lambda h: c_scr[h][None], lambda h: n_scr[h][None], put)
        m_scr[...] = m_out[:1, :]
        cp_out[0] = c_scr[...]
        np_out[0] = n_scr[...]
        mp_out[0] = m_out[:1, :]

    @pl.when(jnp.logical_not(is_prompt))
    def _():
        def put(h, c_new, n_new):
            cs_out[0, :, h] = c_new
            ns_out[:, h] = n_new

        m_out = _ml_chunk(nseq_sample, heads, dk, dv, q_ref, k_ref, v_ref, o_ref, gc_ref[...], gr_ref[...],
                          ms_ref[0], gain_ref, mix_ref, lambda h: cs_ref[0, :, h], lambda h: ns_ref[0, :, h], put)
        ms_out[...] = m_out


def _ml_call(u, gcol, grow, c_s, n_s, m_s_col, g_ml, cs_stack, layer, t_prompt, seq, dec_seq,
             q_off, k_off, v_off, o_off, tc):
    t = u.shape[0]
    depth, bs, heads, dk, dv = c_s.shape
    bp = t_prompt // seq
    hdk, hdv = heads * dk, heads * dv
    n_p = t_prompt // tc
    cps = seq // tc
    nseq = tc // dec_seq
    t_s = t - t_prompt
    row = lambda i: i
    samp_in = samp_out = lambda i: jnp.maximum(i - n_p, 0)
    pseq = lambda i: jnp.minimum(i // cps, bp - 1)
    in_specs = [
        pl.BlockSpec((tc, hdk), lambda i: (row(i), q_off // hdk)),
        pl.BlockSpec((tc, hdk), lambda i: (row(i), k_off // hdk)),
        pl.BlockSpec((tc, hdv), lambda i: (row(i), v_off // hdv)),
        pl.BlockSpec((tc, hdv), lambda i: (row(i), o_off // hdv)),
        pl.BlockSpec((tc, LANES), lambda i: (row(i), 0)),
        pl.BlockSpec((2 * heads, tc), lambda i: (0, row(i))),
        pl.BlockSpec((1, nseq, heads, dk, dv), lambda i: (layer, samp_in(i), 0, 0, 0)),
        pl.BlockSpec((1, nseq, heads, 1, dk), lambda i: (layer, samp_in(i), 0, 0, 0)),
        pl.BlockSpec((1, tc, LANES), lambda i: (layer, samp_in(i), 0)),
        pl.BlockSpec((1, 1, hdv), lambda i: (layer, 0, 0)),
        pl.BlockSpec(memory_space=pl.ANY),
    ]
    args = [u, u, u, u, gcol, grow, c_s, n_s, m_s_col, g_ml, cs_stack]
    return pl.pallas_call(
        functools.partial(_ml_kernel, n_prompt_chunks=n_p, chunks_per_seq=cps, heads=heads, dk=dk, dv=dv,
                          nseq_sample=nseq),
        grid=(t // tc,),
        in_specs=in_specs,
        out_specs=[
            pl.BlockSpec((tc, hdv), lambda i: (row(i), 0)),
            pl.BlockSpec((1, heads, dk, dv), lambda i: (pseq(i), 0, 0, 0)),
            pl.BlockSpec((1, heads, 1, dk), lambda i: (pseq(i), 0, 0, 0)),
            pl.BlockSpec((1, 1, LANES), lambda i: (pseq(i), 0, 0)),
            pl.BlockSpec((1, nseq, heads, dk, dv), lambda i: (layer, samp_out(i), 0, 0, 0)),
            pl.BlockSpec((nseq, heads, 1, dk), lambda i: (samp_out(i), 0, 0, 0)),
            pl.BlockSpec((tc, LANES), lambda i: (samp_out(i), 0)),
        ],
        out_shape=[
            jax.ShapeDtypeStruct((t, hdv), BF16),
            jax.ShapeDtypeStruct((bp, heads, dk, dv), F32),
            jax.ShapeDtypeStruct((bp, heads, 1, dk), F32),
            jax.ShapeDtypeStruct((bp, 1, LANES), F32),
            jax.ShapeDtypeStruct((depth, bs, heads, dk, dv), F32),
            jax.ShapeDtypeStruct((bs, heads, 1, dk), F32),
            jax.ShapeDtypeStruct((t_s, LANES), F32),
        ],
        scratch_shapes=[
            pltpu.VMEM((heads, dk, dv), F32),
            pltpu.VMEM((heads, 1, dk), F32),
            pltpu.VMEM((1, LANES), F32),
        ],
        input_output_aliases={len(args) - 1: 4},
        compiler_params=_cparams("arbitrary"),
        name="mlstm",
    )(*args)


def _out_kernel(x_ref, mr_ref, mm_ref, gtp_ref, gts_ref, shp_ref, shs_ref, scp_ref, scs_ref, wo_hbm, wr_ref, br_ref,
                xn_ref, h2_ref, eid_ref, wt_ref, wo_scr, stage, sem, *, layer, n_groups, per_group, n_prompt_tiles):
    i = pl.program_id(0)
    tm, d = x_ref.shape
    c = mr_ref.shape[1]
    rows = stage.shape[0]

    @pl.when(i == 0)
    def _():
        for part in range(wo_scr.shape[0] // rows):
            cp = pltpu.make_async_copy(wo_hbm.at[layer, pl.ds(part * rows, rows)], stage, sem)
            cp.start()
            cp.wait()
            wo_scr[pl.ds(part * rows, rows), :] = stage[...].astype(BF16)

    is_prompt = i < n_prompt_tiles
    acc = jnp.dot(mr_ref[...], wo_scr[:c, :], preferred_element_type=F32)
    acc = acc + jnp.dot(mm_ref[...], wo_scr[c:, :], preferred_element_type=F32)
    xn3 = _rows3(x_ref[...]) + _pick_mod(is_prompt, gtp_ref, gts_ref) * _rows3(acc)
    rs = lax.rsqrt(jnp.mean(xn3 * xn3, axis=-1, keepdims=True) + EPS)
    h2 = (xn3 * rs * (1.0 + _pick_mod(is_prompt, scp_ref, scs_ref))
          + _pick_mod(is_prompt, shp_ref, shs_ref)).reshape(tm, d)
    xn_ref[...] = xn3.reshape(tm, d)
    _store_slabs(h2_ref, h2)

    logits = jnp.dot(h2.astype(BF16), wr_ref[0], preferred_element_type=F32) + br_ref[0]
    lane = lax.broadcasted_iota(jnp.int32, logits.shape, 1)
    lane_f = lane.astype(F32)
    big = np.float32(LANES)

    def first_max(vals):
        vmax = jnp.max(vals, axis=-1, keepdims=True)
        idx = jnp.min(jnp.where(vals == vmax, lane_f, big), axis=-1, keepdims=True)
        return vmax, idx.astype(jnp.int32)

    gl = jnp.where(lane < n_groups, logits, -jnp.inf)
    gmax, g_sel = first_max(gl)
    p_g = 1.0 / jnp.sum(jnp.exp(gl - gmax), axis=-1, keepdims=True)
    lo = n_groups + g_sel * per_group
    el = jnp.where((lane >= lo) & (lane < lo + per_group), logits, -jnp.inf)
    v1, i1 = first_max(el)
    v2, i2 = first_max(jnp.where(lane == i1, -jnp.inf, el))
    e21 = jnp.exp(v2 - v1)
    w1 = 1.0 / (1.0 + e21)
    w2 = e21 * w1
    eid_ref[...] = jnp.where(lane == 0, i1 - n_groups, jnp.where(lane == 1, i2 - n_groups, 0))
    wt_ref[...] = jnp.where(lane == 0, w1 * p_g, jnp.where(lane == 1, w2 * p_g, 0.0))


def _out_call(x, mix_rg, mix_ml, mod_p, mod_s, w_out, w_router, b_router, layer, n_groups, per_group, tok):
    t, d = x.shape
    c = mix_rg.shape[1]
    cm = mix_ml.shape[1]
    tm = tok.tile
    stage_rows = min(256, c + cm)
    return pl.pallas_call(
        functools.partial(_out_kernel, layer=layer, n_groups=n_groups, per_group=per_group,
                          n_prompt_tiles=tok.n_prompt_tiles),
        grid=(t // tm,),
        in_specs=[
            pl.BlockSpec((tm, d), lambda i: (i, 0)),
            pl.BlockSpec((tm, c), lambda i: (i, 0)),
            pl.BlockSpec((tm, cm), lambda i: (i, 0)),
            *tok.mod_specs(layer, GT_A, d),
            *tok.mod_specs(layer, SH_F, d),
            *tok.mod_specs(layer, SC_F, d),
            pl.BlockSpec(memory_space=pl.ANY),
            pl.BlockSpec((1, d, LANES), lambda i: (layer, 0, 0)),
            pl.BlockSpec((1, 1, LANES), lambda i: (layer, 0, 0)),
        ],
        out_specs=[
            pl.BlockSpec((tm, d), lambda i: (i, 0)),
            pl.BlockSpec((tm * _slab_rows(d), LANES), lambda i: (i, 0)),
            pl.BlockSpec((tm, LANES), lambda i: (i, 0)),
            pl.BlockSpec((tm, LANES), lambda i: (i, 0)),
        ],
        out_shape=[
            jax.ShapeDtypeStruct((t, d), F32),
            jax.ShapeDtypeStruct((t * _slab_rows(d), LANES), F32),
            jax.ShapeDtypeStruct((t, LANES), jnp.int32),
            jax.ShapeDtypeStruct((t, LANES), F32),
        ],
        scratch_shapes=[
            pltpu.VMEM((c + cm, d), BF16),
            pltpu.VMEM((stage_rows, d), F32),
            pltpu.SemaphoreType.DMA(()),
        ],
        compiler_params=_cparams("arbitrary"),
        name="out_proj_router",
    )(x, mix_rg, mix_ml, mod_p, mod_s, mod_p, mod_s, mod_p, mod_s, w_out, w_router, b_router)


def _slab_rows(d):
    return d // LANES


def _store_slabs(ref, val):
    rows, d = val.shape
    slab = _slab_rows(d)
    for s in range(slab):
        ref[pl.ds(s, rows, stride=slab), :] = val[:, s * LANES:(s + 1) * LANES]


def _load_slabs(ref, rows, d):
    slab = _slab_rows(d)
    return jnp.concatenate([ref[pl.ds(s, rows, stride=slab), :] for s in range(slab)], axis=-1)


def _gather_slabs(idx_ref, base, n_rows, src_hbm, dst, sem, slab, stride=1, offset=0):
    def body(g, carry):
        for s in range(SUBLANES):
            r = g * SUBLANES + s
            tok = idx_ref[base + r * stride + offset]
            pltpu.make_async_copy(src_hbm.at[pl.ds(pl.multiple_of(tok * slab, slab), slab)],
                                  dst.at[pl.ds(pl.multiple_of(r * slab, slab), slab)], sem).start()
        return carry

    lax.fori_loop(0, n_rows // SUBLANES, body, 0)


def _wait_rows(buf, sem):
    pltpu.make_async_copy(buf, buf, sem).wait()


def _moe_kernel(te_ref, nx_ref, nu_ref, src_ref, h2_hbm, w1_hbm, w3_hbm, w2_hbm, y_ref,
                xbuf, st1, st3, st2, w1_scr, w3_scr, w2_scr, slot_ref, gsem, wsem,
                *, layer):
    j = pl.program_id(0)
    nbuf = xbuf.shape[0]
    d = w1_scr.shape[0]
    slab = _slab_rows(d)
    tm = xbuf.shape[1] // slab
    n_used = nu_ref[0]

    def weight_copies(expert, slot):
        return [pltpu.make_async_copy(w_hbm.at[layer, expert], stage.at[slot], wsem.at[slot])
                for w_hbm, stage in ((w1_hbm, st1), (w3_hbm, st3), (w2_hbm, st2))]

    def issue(tile):
        _gather_slabs(src_ref, tile * tm, tm, h2_hbm, xbuf.at[tile % nbuf], gsem.at[tile % nbuf], slab)

    @pl.when(j == 0)
    def _():
        slot_ref[0] = 1
        for cp in weight_copies(te_ref[0], 0):
            cp.start()
        for ahead in range(nbuf - 1):
            @pl.when(ahead < n_used)
            def _():
                issue(ahead)

    @pl.when(j + (nbuf - 1) < n_used)
    def _():
        issue(j + (nbuf - 1))

    @pl.when((j < n_used) & ((j == 0) | (te_ref[j] != te_ref[jnp.maximum(j - 1, 0)])))
    def _():
        slot = 1 - slot_ref[0]
        slot_ref[0] = slot
        for cp in weight_copies(te_ref[j], slot):
            cp.wait()
        w1_scr[...] = st1[slot].astype(BF16)
        w3_scr[...] = st3[slot].astype(BF16)
        w2_scr[...] = st2[slot].astype(BF16)

        @pl.when(nx_ref[j] >= 0)
        def _():
            for cp in weight_copies(nx_ref[j], 1 - slot):
                cp.start()

    @pl.when(j < n_used)
    def _():
        slot = j % nbuf
        _wait_rows(xbuf.at[slot], gsem.at[slot])
        xb = _load_slabs(xbuf.at[slot], tm, d).astype(BF16)
        a = jnp.dot(xb, w1_scr[...], preferred_element_type=F32)
        b = jnp.dot(xb, w3_scr[...], preferred_element_type=F32)
        hid = (a * _sigmoid(a)) * b
        _store_slabs(y_ref, jnp.dot(hid.astype(BF16), w2_scr[...], preferred_element_type=F32))

    @pl.when(j >= n_used)
    def _():
        y_ref[...] = jnp.zeros_like(y_ref)


def _moe_call(tile_exp, next_exp, n_used, src_tok, h2_slabs, w1, w3, w2, layer, tm):
    _, ne, d, f = w1.shape
    slab = _slab_rows(d)
    n_tiles = tile_exp.shape[0]
    gather_bufs = 3
    grid_spec = pltpu.PrefetchScalarGridSpec(
        num_scalar_prefetch=4,
        grid=(n_tiles,),
        in_specs=[pl.BlockSpec(memory_space=pl.ANY)] * 4,
        out_specs=pl.BlockSpec((tm * slab, LANES), lambda j, *_: (j, 0)),
        scratch_shapes=[
            pltpu.VMEM((gather_bufs, tm * slab, LANES), F32),
            pltpu.VMEM((2, d, f), F32),
            pltpu.VMEM((2, d, f), F32),
            pltpu.VMEM((2, f, d), F32),
            pltpu.VMEM((d, f), BF16),
            pltpu.VMEM((d, f), BF16),
            pltpu.VMEM((f, d), BF16),
            pltpu.SMEM((1,), jnp.int32),
            pltpu.SemaphoreType.DMA((gather_bufs,)),
            pltpu.SemaphoreType.DMA((2,)),
        ],
    )
    return pl.pallas_call(
        functools.partial(_moe_kernel, layer=layer),
        grid_spec=grid_spec,
        out_shape=jax.ShapeDtypeStruct((n_tiles * tm * slab, LANES), F32),
        compiler_params=_cparams("arbitrary"),
        name="experts",
    )(tile_exp, next_exp, n_used, src_tok, h2_slabs, w1, w3, w2)


def _comb_kernel(dest_ref, xn_ref, wt_ref, gtp_ref, gts_ref, *rest, final, heads, n_prompt_tiles):
    if final:
        gf_ref, y_hbm, yp_ref, ys_ref, ybuf, sem = rest
    else:
        shp_ref, shs_ref, scp_ref, scs_ref, wg_ref, bg_ref, y_hbm, xo_ref, h_ref, g_ref, ybuf, sem = rest
    i = pl.program_id(0)
    n = pl.num_programs(0)
    tm, d = xn_ref.shape
    is_prompt = i < n_prompt_tiles

    def issue(tile, slot):
        for k in range(2):
            _gather_slabs(dest_ref, tile * tm * 2, tm, y_hbm, ybuf.at[slot, k], sem.at[slot], _slab_rows(d),
                          stride=2, offset=k)

    @pl.when(i == 0)
    def _():
        issue(0, 0)

    @pl.when(i + 1 < n)
    def _():
        issue(i + 1, (i + 1) % 2)

    slot = i % 2
    for k in range(2):
        _wait_rows(ybuf.at[slot, k], sem.at[slot])
    wt = wt_ref[...]
    y = wt[:, 0:1] * _load_slabs(ybuf.at[slot, 0], tm, d) + wt[:, 1:2] * _load_slabs(ybuf.at[slot, 1], tm, d)
    xo3 = _rows3(xn_ref[...]) + _pick_mod(is_prompt, gtp_ref, gts_ref) * _rows3(y)
    if final:
        xo = xo3.reshape(tm, d)
        yo = xo * lax.rsqrt(jnp.mean(xo * xo, axis=-1, keepdims=True) + EPS) * gf_ref[...]

        @pl.when(is_prompt)
        def _():
            yp_ref[...] = yo

        @pl.when(jnp.logical_not(is_prompt))
        def _():
            ys_ref[...] = yo
    else:
        xo_ref[...] = xo3.reshape(tm, d)
        _norm_gates(xo3, _pick_mod(is_prompt, scp_ref, scs_ref), _pick_mod(is_prompt, shp_ref, shs_ref),
                    wg_ref, bg_ref, h_ref, g_ref, heads)


def _comb_call(dest, xn, wts, mod_p, mod_s, w_gate, b_gate, g_final, y, layer, heads, tok, final):
    t, d = xn.shape
    tm = tok.tile
    full = lambda i, dst: (0, 0)
    in_specs = [
        pl.BlockSpec((tm, d), lambda i, dst: (i, 0)),
        pl.BlockSpec((tm, LANES), lambda i, dst: (i, 0)),
        *tok.mod_specs(layer, GT_F, d),
    ]
    args = [xn, wts, mod_p, mod_s]
    if final:
        n_p = tok.n_prompt_tiles
        in_specs.append(pl.BlockSpec((1, d), full))
        args.append(g_final)
        out_specs = [pl.BlockSpec((tm, d), lambda i, dst: (jnp.minimum(i, n_p - 1), 0)),
                     pl.BlockSpec((tm, d), lambda i, dst: (tok.sample_tile(i), 0))]
        out_shape = [jax.ShapeDtypeStruct((n_p * tm, d), F32), jax.ShapeDtypeStruct((t - n_p * tm, d), F32)]
    else:
        out_specs = [pl.BlockSpec((tm, d), lambda i, dst: (i, 0))]
        out_shape = [jax.ShapeDtypeStruct((t, d), F32)]
        nxt = layer + 1
        in_specs += [
            *tok.mod_specs(nxt, SH_A, d),
            *tok.mod_specs(nxt, SC_A, d),
            pl.BlockSpec((1, d, LANES), lambda i, dst: (nxt, 0, 0)),
            pl.BlockSpec((1, 1, LANES), lambda i, dst: (nxt, 0, 0)),
        ]
        args += [mod_p, mod_s, mod_p, mod_s, w_gate, b_gate]
        out_specs += [pl.BlockSpec((tm, d), lambda i, dst: (i, 0)), pl.BlockSpec((tm, LANES), lambda i, dst: (i, 0))]
        out_shape += [jax.ShapeDtypeStruct((t, d), BF16), jax.ShapeDtypeStruct((t, LANES), F32)]
    in_specs.append(pl.BlockSpec(memory_space=pl.ANY))
    args.append(y)
    grid_spec = pltpu.PrefetchScalarGridSpec(
        num_scalar_prefetch=1,
        grid=(t // tm,),
        in_specs=in_specs,
        out_specs=out_specs,
        scratch_shapes=[pltpu.VMEM((2, 2, tm * _slab_rows(d), LANES), F32), pltpu.SemaphoreType.DMA((2,))],
    )
    return pl.pallas_call(
        functools.partial(_comb_kernel, final=final, heads=heads, n_prompt_tiles=tok.n_prompt_tiles),
        grid_spec=grid_spec,
        out_shape=out_shape,
        compiler_params=_cparams("arbitrary"),
        name="combine",
    )(dest, *args)


def _invert_kernel(dest_ref, src_ref):
    n_pairs = dest_ref.shape[0]
    n_rows = src_ref.shape[0]

    def clear(i, carry):
        for u in range(INVERT_CLEAR_UNROLL):
            src_ref[i * INVERT_CLEAR_UNROLL + u] = 0
        return carry

    lax.fori_loop(0, n_rows // INVERT_CLEAR_UNROLL, clear, 0)

    def place(i, carry):
        for u in range(SUBLANES):
            pair = i * SUBLANES + u
            src_ref[dest_ref[pair]] = lax.shift_right_logical(pair, 1)
        return carry

    lax.fori_loop(0, n_pairs // SUBLANES, place, 0)


def _invert_call(dest, n_rows):
    assert dest.shape[0] % SUBLANES == 0 and n_rows % INVERT_CLEAR_UNROLL == 0
    return pl.pallas_call(
        _invert_kernel,
        in_specs=[pl.BlockSpec(memory_space=pltpu.SMEM)],
        out_specs=pl.BlockSpec(memory_space=pltpu.SMEM),
        out_shape=jax.ShapeDtypeStruct((n_rows,), jnp.int32),
        name="route_invert",
    )(dest)


def _route_plan(eid, n_experts, tm):
    t = eid.shape[0]
    e_flat = eid[:, :2].reshape(-1)
    n_pairs = 2 * t
    n_tiles = -(-n_pairs // tm) + n_experts
    onehot = (e_flat[:, None] == jnp.arange(n_experts, dtype=jnp.int32)[None, :]).astype(jnp.int32)
    csum = jnp.cumsum(onehot, axis=0)
    rank = jnp.sum(csum * onehot, axis=1) - 1
    counts = csum[-1]
    padded = ((counts + tm - 1) // tm) * tm
    ends = jnp.cumsum(padded)
    starts = ends - padded
    dest = (jnp.sum(starts[None, :] * onehot, axis=1) + rank).astype(jnp.int32)
    src_tok = _invert_call(dest, n_tiles * tm)
    n_used = (ends[-1] // tm).astype(jnp.int32)
    tile_id = jnp.minimum(jnp.arange(n_tiles, dtype=jnp.int32), n_used - 1)
    tile_exp = jnp.sum((ends[None, :] <= (tile_id * tm)[:, None]).astype(jnp.int32), axis=1)
    tile_exp = jnp.minimum(tile_exp, n_experts - 1).astype(jnp.int32)
    ids = jnp.arange(n_experts, dtype=jnp.int32)
    later = (ids[None, :] > ids[:, None]) & (padded[None, :] > 0)
    following = jnp.min(jnp.where(later, ids[None, :], n_experts), axis=1)
    following = jnp.where(following == n_experts, -1, following).astype(jnp.int32)
    return tile_exp, following[tile_exp], n_used.reshape(1), src_tok, dest


def _pick_tile(total, pref):
    tile = min(pref, total)
    while total % tile:
        tile //= 2
    return tile


def kernel(x_prompt, x_sample, c_prompt, c_sample, state_rg_conv, state_rg_h, state_mlstm_C, state_mlstm_n,
           state_mlstm_m, w_mod, b_mod, w_in, conv_w, conv_b, w_ra, b_ra, w_ri, b_ri, lam, g_rg, b_ig, b_fg,
           g_ml, w_out, w_grp, b_grp, w_er, b_er, w1, w3, w2, g_final):
    bp, seq, d = x_prompt.shape
    bs, dec_seq, _ = x_sample.shape
    depth = w_mod.shape[0]
    d_rg = conv_w.shape[2]
    conv_width = conv_w.shape[1]
    _, _, heads, dk, dv = state_mlstm_C.shape
    d_ml = heads * dv
    n_groups, per_group = w_er.shape[1], w_er.shape[3]
    n_experts = w1.shape[1]
    t_p, t_s = bp * seq, bs * dec_seq
    t = t_p + t_s
    n_main = 2 * d_rg + 2 * heads * dk + 2 * d_ml
    q_off, k_off, v_off, o_off = 2 * d_rg, 2 * d_rg + heads * dk, 2 * d_rg + 2 * heads * dk, n_main - d_ml
    assert dec_seq == SUBLANES and seq % LANES == 0 and conv_width - 1 <= SUBLANES
    assert 2 * heads <= LANES and n_groups * (per_group + 1) <= LANES
    assert q_off % (heads * dk) == 0 and k_off % (heads * dk) == 0 and v_off % d_ml == 0 and o_off % d_ml == 0

    tok = _Tokens(t_p, t_s, seq, _pick_tile(np.gcd(seq, t_s), 256))
    tok_out = _Tokens(t_p, t_s, seq, _pick_tile(np.gcd(seq, t_s), 512))
    tm_in = _pick_tile(np.gcd(t_p, t_s), 1024)
    tc = LANES
    assert t_s % tc == 0 and seq % tc == 0

    mc = -(-(bp + bs) // SUBLANES) * SUBLANES
    c_all = jnp.zeros((mc, d), F32).at[:bp].set(c_prompt).at[bp:bp + bs].set(c_sample)
    mod = _mod_call(c_all, w_mod, b_mod, _pick_tile(6 * d, 1024))
    mod_p = mod[:, :bp].reshape(depth, bp, 1, 6 * d)
    mod_s = mod[:, bp:bp + bs].reshape(depth, bs, 1, 6 * d)

    w_in_t = jnp.swapaxes(w_in, 1, 2)
    w_gate = jnp.zeros((depth, d, LANES), F32).at[:, :, :2 * heads].set(w_in[:, :, n_main:]).astype(BF16)
    b_gate = jnp.zeros((depth, 1, LANES), F32).at[:, 0, :heads].set(b_ig).at[:, 0, heads:2 * heads].set(b_fg)
    w_ra_b, w_ri_b = w_ra.astype(BF16), w_ri.astype(BF16)
    n_rt = n_groups * (per_group + 1)
    w_router = jnp.concatenate([w_grp, jnp.moveaxis(w_er, 1, 2).reshape(depth, d, n_groups * per_group)], axis=-1)
    w_router = jnp.zeros((depth, d, LANES), F32).at[:, :, :n_rt].set(w_router).astype(BF16)
    b_router = jnp.concatenate([b_grp, b_er.reshape(depth, -1)], axis=-1)
    b_router = jnp.zeros((depth, 1, LANES), F32).at[:, 0, :n_rt].set(b_router)
    vec = lambda p: p.reshape(depth, 1, -1)

    pad_rows = SUBLANES - (conv_width - 1)
    prev_s = jnp.pad(state_rg_conv, ((0, 0), (0, 0), (pad_rows, 0), (0, 0))).reshape(depth, bs * SUBLANES, d_rg)
    h0_s = state_rg_h.reshape(depth, bs, 1, d_rg)
    n_s = state_mlstm_n.reshape(depth, bs, heads, 1, dk)
    m_s_col = jnp.pad(jnp.repeat(state_mlstm_m, dec_seq, axis=1), ((0, 0), (0, 0), (0, LANES - heads)))

    x, h, gcol = _prenorm_call(x_prompt.reshape(t_p, d), x_sample.reshape(t_s, d), mod_p, mod_s, w_gate, b_gate,
                               heads, tok)
    outs_p = [[] for _ in range(5)]
    outs_s = [[] for _ in range(5)]
    cs_stack = jnp.zeros_like(state_mlstm_C)
    for l in range(depth):
        u = _in_call(h, w_in_t, l, n_main, tm_in, _pick_tile(n_main, IN_PROJ_TILE_COLS))
        hr, mix_rg = _rg_call(u, prev_s, h0_s, conv_w, vec(conv_b), w_ra_b, vec(b_ra), w_ri_b, vec(b_ri),
                              vec(lam), vec(g_rg), l, tok)
        grow = gcol[:, :2 * heads].T
        mix_ml, c_p, n_p, m_p, cs_stack, n_so, m_so = _ml_call(
            u, gcol, grow, state_mlstm_C, n_s, m_s_col, vec(g_ml), cs_stack, l, t_p, seq, dec_seq,
            q_off, k_off, v_off, o_off, tc)
        xn, h2, eid, wts = _out_call(x, mix_rg, mix_ml, mod_p, mod_s, w_out, w_router, b_router, l,
                                     n_groups, per_group, tok_out)
        tile_exp, next_exp, n_used, src_tok, dest = _route_plan(eid, n_experts, EXPERT_TILE_ROWS)
        y = _moe_call(tile_exp, next_exp, n_used, src_tok, h2, w1, w3, w2, l, EXPERT_TILE_ROWS)
        final = l == depth - 1
        res = _comb_call(dest, xn, wts, mod_p, mod_s, w_gate, b_gate, g_final[None], y, l, heads, tok, final)
        if final:
            y_p, y_s = res
        else:
            x, h, gcol = res

        tail = conv_width - 1
        outs_p[0].append(jnp.stack([u[(b + 1) * seq - tail:(b + 1) * seq, :d_rg] for b in range(bp)]))
        outs_p[1].append(jnp.stack([hr[(b + 1) * seq - 1] for b in range(bp)]))
        outs_p[2].append(c_p)
        outs_p[3].append(n_p.reshape(bp, heads, dk))
        outs_p[4].append(m_p[:, 0, :heads])
        outs_s[0].append(u[t_p:, :d_rg].reshape(bs, dec_seq, d_rg)[:, dec_seq - tail:])
        outs_s[1].append(hr[t_p:].reshape(bs, dec_seq, d_rg)[:, -1])
        outs_s[3].append(n_so.reshape(bs, heads, dk))
        outs_s[4].append(m_so.reshape(bs, dec_seq, LANES)[:, -1, :heads])

    stack = lambda parts: jnp.stack(parts)
    return (y_p.reshape(bp, seq, d), y_s.reshape(bs, dec_seq, d), *[stack(o) for o in outs_p],
            stack(outs_s[0]), stack(outs_s[1]), cs_stack, stack(outs_s[3]), stack(outs_s[4]))
```
